```python
import jax, jax.numpy as jnp
from jax import lax
import numpy as np

D_MODEL = 1024
BATCH = 8
SEQ = 4096
DEPTH = 1

ATTN_HEADS = 8
ATTN_WIDTH = D_MODEL // 2
ATTN_HEAD_DIM = ATTN_WIDTH // ATTN_HEADS
MLSTM_HEADS = 4
MLSTM_WIDTH = D_MODEL - ATTN_WIDTH
MLSTM_HEAD_DIM = MLSTM_WIDTH // MLSTM_HEADS
IN_COLS = 3 * ATTN_WIDTH + 2 * MLSTM_WIDTH
CONV_WIDTH = 4
MLSTM_CHUNK = 64
MOBA_BLOCK = 256
MOBA_TOPK = 3
MOBA_QCHUNK = 32
D_FF = ((8 * D_MODEL // 3 + 127) // 128) * 128
N_MOD = 9
EPS = 1e-6

kernel_name = 'hymba_moba_mlstm_macaron_adaln'


def rmsnorm(x, g):
    xf = x.astype(jnp.float32)
    y = xf * lax.rsqrt(jnp.mean(xf * xf, axis=-1, keepdims=True) + EPS)
    return (y * g.astype(jnp.float32)).astype(x.dtype)


def modulate(h, shift, scale):
    return h * (1 + scale) + shift


def swiglu(h, w_gate, w_up, w_down):
    return (jax.nn.silu(h @ w_gate) * (h @ w_up)) @ w_down


def causal_depthwise_conv(x, w, b):
    C = x.shape[-1]
    out = lax.conv_general_dilated(x, w[:, None, :], window_strides=(1,),
                                   padding=((CONV_WIDTH - 1, 0),),
                                   dimension_numbers=('NWC', 'WIO', 'NWC'),
                                   feature_group_count=C)
    return out + b


def moba_attention(q, k, v):
    B, H, S, d = q.shape
    s_pad = -(-S // MOBA_BLOCK) * MOBA_BLOCK
    pad = s_pad - S
    if pad:
        padw = ((0, 0), (0, 0), (0, pad), (0, 0))
        q, k, v = jnp.pad(q, padw), jnp.pad(k, padw), jnp.pad(v, padw)
    nb = s_pad // MOBA_BLOCK
    topk = min(MOBA_TOPK, nb)
    k_blocks = k.reshape(B, H, nb, MOBA_BLOCK, d)
    v_blocks = v.reshape(B, H, nb, MOBA_BLOCK, d)
    k_mean = jnp.mean(k_blocks.astype(jnp.float32), axis=3)
    gate = jnp.einsum('bhsd,bhnd->bhsn', q.astype(jnp.float32), k_mean)
    q_block = jnp.arange(s_pad) // MOBA_BLOCK
    fully_past = jnp.arange(nb)[None, :] < q_block[:, None]
    gate = jnp.where(fully_past, gate, -jnp.inf)
    top_val, top_idx = lax.top_k(gate, topk)
    sel_valid = jnp.isfinite(top_val)

    n_chunks = s_pad // MOBA_QCHUNK

    def to_chunks(a):
        return jnp.moveaxis(a.reshape(B, H, n_chunks, MOBA_QCHUNK, *a.shape[3:]), 2, 0)

    b_idx = jnp.arange(B)[:, None, None]
    h_idx = jnp.arange(H)[None, :, None]
    scale = d ** -0.5
    n_sel = topk * MOBA_BLOCK

    def attend_chunk(args):
        q_c, idx_c, valid_c, ci = args
        blk = ci * MOBA_QCHUNK // MOBA_BLOCK
        flat = idx_c.reshape(B, H, MOBA_QCHUNK * topk)
        k_sel = k_blocks[b_idx, h_idx, flat].reshape(B, H, MOBA_QCHUNK, n_sel, d)
        v_sel = v_blocks[b_idx, h_idx, flat].reshape(B, H, MOBA_QCHUNK, n_sel, d)
        s_sel = jnp.einsum('bhqd,bhqnd->bhqn', q_c, k_sel).astype(jnp.float32) * scale
        s_sel = jnp.where(jnp.repeat(valid_c, MOBA_BLOCK, axis=-1), s_sel, -jnp.inf)
        k_own = lax.dynamic_index_in_dim(k_blocks, blk, axis=2, keepdims=False)
        v_own = lax.dynamic_index_in_dim(v_blocks, blk, axis=2, keepdims=False)
        s_own = jnp.einsum('bhqd,bhkd->bhqk', q_c, k_own).astype(jnp.float32) * scale
        q_pos = ci * MOBA_QCHUNK + jnp.arange(MOBA_QCHUNK)
        k_pos = blk * MOBA_BLOCK + jnp.arange(MOBA_BLOCK)
        s_own = jnp.where(k_pos[None, :] <= q_pos[:, None], s_own, -jnp.inf)
        p = jax.nn.softmax(jnp.concatenate([s_sel, s_own], axis=-1), axis=-1).astype(v.dtype)
        return (jnp.einsum('bhqn,bhqnd->bhqd', p[..., :n_sel], v_sel)
                + jnp.einsum('bhqk,bhkd->bhqd', p[..., n_sel:], v_own))

    out = lax.map(attend_chunk, (to_chunks(q), to_chunks(top_idx), to_chunks(sel_valid),
                                 jnp.arange(n_chunks)))
    out = jnp.moveaxis(out, 0, 2).reshape(B, H, s_pad, d)
    return out[:, :, :S]


def mlstm_chunkwise(q, k, v, i_pre, log_f):
    B, H, S, d = q.shape
    L = MLSTM_CHUNK
    nc = S // L

    def chunks(a):
        return jnp.moveaxis(a.reshape(B, H, nc, L, *a.shape[3:]), 2, 0)

    qc, kc, vc, ic = chunks(q), chunks(k), chunks(v), chunks(i_pre)
    bc = jnp.cumsum(chunks(log_f), axis=-1)
    causal = jnp.tril(jnp.ones((L, L), dtype=bool))

    def step(carry, xs):
        C, n, m = carry
        q_, k_, v_, i_, b_ = xs
        log_d = jnp.where(causal, b_[..., :, None] - b_[..., None, :] + i_[..., None, :], -jnp.inf)
        log_inter = b_ + m[..., None]
        m_t = jnp.maximum(log_inter, jnp.max(log_d, axis=-1))
        w_intra = jnp.exp(log_d - m_t[..., None])
        w_inter = jnp.exp(log_inter - m_t)
        s = jnp.einsum('bhtd,bhsd->bhts', q_, k_) * w_intra
        num = (jnp.einsum('bhts,bhsv->bhtv', s, v_)
               + w_inter[..., None] * jnp.einsum('bhvk,bhtk->bhtv', C, q_))
        den = jnp.sum(s, axis=-1) + w_inter * jnp.einsum('bhk,bhtk->bht', n, q_)
        h = num / jnp.maximum(jnp.abs(den), jnp.exp(-m_t))[..., None]
        b_last = b_[..., -1]
        log_g = b_last[..., None] - b_ + i_
        m_new = jnp.maximum(b_last + m, jnp.max(log_g, axis=-1))
        w_g = jnp.exp(log_g - m_new[..., None])
        decay = jnp.exp(b_last + m - m_new)
        C = decay[..., None, None] * C + jnp.einsum('bhsv,bhsk->bhvk', v_ * w_g[..., None], k_)
        n = decay[..., None] * n + jnp.einsum('bhs,bhsk->bhk', w_g, k_)
        return (C, n, m_new), h

    init = (jnp.zeros((B, H, d, d), jnp.float32), jnp.zeros((B, H, d), jnp.float32),
            jnp.zeros((B, H), jnp.float32))
    _, h = lax.scan(step, init, (qc, kc, vc, ic, bc))
    return jnp.moveaxis(h, 0, 2).reshape(B, H, S, d)


def hybrid_mixer(h, w_in, conv_w, conv_b, w_q_m, w_k_m, w_v_m, w_if, b_if,
                 mlstm_norm, mlstm_skip, attn_norm, w_out):
    B, S, _ = h.shape
    proj = h @ w_in
    q_a, k_a, v_a, x_m, z = jnp.split(
        proj, [ATTN_WIDTH, 2 * ATTN_WIDTH, 3 * ATTN_WIDTH, 3 * ATTN_WIDTH + MLSTM_WIDTH], axis=-1)

    def heads_a(t):
        return t.reshape(B, S, ATTN_HEADS, ATTN_HEAD_DIM).transpose(0, 2, 1, 3)
    o_a = moba_attention(heads_a(q_a), heads_a(k_a), heads_a(v_a))
    o_a = rmsnorm(o_a.transpose(0, 2, 1, 3).reshape(B, S, ATTN_WIDTH), attn_norm)

    x_conv = jax.nn.silu(causal_depthwise_conv(x_m, conv_w, conv_b))
    def headwise(t, w):
        return jnp.einsum('bshd,hde->bshe', t.reshape(B, S, MLSTM_HEADS, MLSTM_HEAD_DIM), w)
    q_m = headwise(x_conv, w_q_m)
    k_m = headwise(x_conv, w_k_m)
    v_m = headwise(x_m, w_v_m)
    gates = (jnp.concatenate([q_m, k_m, v_m], axis=-1).reshape(B, S, 3 * MLSTM_WIDTH) @ w_if
             + b_if).astype(jnp.float32)
    i_pre = gates[..., :MLSTM_HEADS].transpose(0, 2, 1)
    log_f = jax.nn.log_sigmoid(gates[..., MLSTM_HEADS:]).transpose(0, 2, 1)
    to_h = lambda t: t.transpose(0, 2, 1, 3).astype(jnp.float32)
    h_m = mlstm_chunkwise(to_h(q_m), to_h(k_m) * MLSTM_HEAD_DIM ** -0.5, to_h(v_m), i_pre, log_f)
    h_m = h_m.transpose(0, 2, 1, 3)
    mu = jnp.mean(h_m, axis=-1, keepdims=True)
    var = jnp.mean(jnp.square(h_m - mu), axis=-1, keepdims=True)
    h_m = ((h_m - mu) * lax.rsqrt(var + EPS)).reshape(B, S, MLSTM_WIDTH)
    h_m = h_m * mlstm_norm.astype(jnp.float32)
    o_m = (h_m.astype(h.dtype) + mlstm_skip * x_conv) * jax.nn.silu(z)

    return jnp.concatenate([o_a, o_m], axis=-1) @ w_out


def setup_inputs(seed: int = 0) -> dict:
    key = jax.random.key(seed)
    ks = jax.random.split(key, 32)
    L, D, F = DEPTH, D_MODEL, D_FF
    H, W, dh = MLSTM_HEADS, MLSTM_WIDTH, MLSTM_HEAD_DIM

    def nrm(k, shape, scale):
        return jax.random.normal(k, shape, jnp.float32) * scale

    b_if = jnp.concatenate([nrm(ks[15], (L, H), 0.1),
                            jnp.broadcast_to(jnp.linspace(3.0, 6.0, H, dtype=jnp.float32), (L, H))
                            + nrm(ks[16], (L, H), 0.1)], axis=-1)
    return {
        'x': nrm(ks[0], (BATCH, SEQ, D), 1.0),
        'c': nrm(ks[1], (BATCH, D), 1.0),
        'w_ada': nrm(ks[2], (L, D, N_MOD * D), 0.2 * D ** -0.5),
        'b_ada': nrm(ks[3], (L, N_MOD * D), 0.02),
        'ffn1_norm': 1.0 + nrm(ks[4], (L, D), 0.02),
        'ffn1_w_gate': nrm(ks[5], (L, D, F), D ** -0.5),
        'ffn1_w_up': nrm(ks[6], (L, D, F), D ** -0.5),
        'ffn1_w_down': nrm(ks[7], (L, F, D), F ** -0.5),
        'mix_norm': 1.0 + nrm(ks[8], (L, D), 0.02),
        'w_in': nrm(ks[9], (L, D, IN_COLS), D ** -0.5),
        'conv_w': nrm(ks[10], (L, CONV_WIDTH, W), CONV_WIDTH ** -0.5),
        'conv_b': nrm(ks[11], (L, W), 0.02),
        'w_q_m': nrm(ks[12], (L, H, dh, dh), dh ** -0.5),
        'w_k_m': nrm(ks[13], (L, H, dh, dh), dh ** -0.5),
        'w_v_m': nrm(ks[14], (L, H, dh, dh), dh ** -0.5),
        'w_if': nrm(ks[17], (L, 3 * W, 2 * H), 0.1 * (3 * W) ** -0.5),
        'b_if': b_if,
        'mlstm_norm': 1.0 + nrm(ks[18], (L, W), 0.02),
        'mlstm_skip': 1.0 + nrm(ks[19], (L, W), 0.02),
        'attn_norm': 1.0 + nrm(ks[20], (L, ATTN_WIDTH), 0.02),
        'w_out': nrm(ks[21], (L, D, D), D ** -0.5),
        'ffn2_norm': 1.0 + nrm(ks[22], (L, D), 0.02),
        'ffn2_w_gate': nrm(ks[23], (L, D, F), D ** -0.5),
        'ffn2_w_up': nrm(ks[24], (L, D, F), D ** -0.5),
        'ffn2_w_down': nrm(ks[25], (L, F, D), F ** -0.5),
        'final_norm': 1.0 + nrm(ks[26], (D,), 0.02),
    }


def reference(x, c, w_ada, b_ada, ffn1_norm, ffn1_w_gate, ffn1_w_up, ffn1_w_down,
              mix_norm, w_in, conv_w, conv_b, w_q_m, w_k_m, w_v_m, w_if, b_if,
              mlstm_norm, mlstm_skip, attn_norm, w_out,
              ffn2_norm, ffn2_w_gate, ffn2_w_up, ffn2_w_down, final_norm):
    B = x.shape[0]
    for l in range(DEPTH):
        mod = (c @ w_ada[l] + b_ada[l]).reshape(B, N_MOD, D_MODEL)
        sh1, sc1, g1, sh2, sc2, g2, sh3, sc3, g3 = [mod[:, None, i] for i in range(N_MOD)]
        h = modulate(rmsnorm(x, ffn1_norm[l]), sh1, sc1)
        x = x + 0.5 * (1 + g1) * swiglu(h, ffn1_w_gate[l], ffn1_w_up[l], ffn1_w_down[l])
        h = modulate(rmsnorm(x, mix_norm[l]), sh2, sc2)
        x = x + (1 + g2) * hybrid_mixer(h, w_in[l], conv_w[l], conv_b[l], w_q_m[l], w_k_m[l],
                                        w_v_m[l], w_if[l], b_if[l], mlstm_norm[l], mlstm_skip[l],
                                        attn_norm[l], w_out[l])
        h = modulate(rmsnorm(x, ffn2_norm[l]), sh3, sc3)
        x = x + 0.5 * (1 + g3) * swiglu(h, ffn2_w_gate[l], ffn2_w_up[l], ffn2_w_down[l])
    return rmsnorm(x, final_norm)
```

```python
import functools

import jax
import jax.numpy as jnp
from jax import lax
from jax.experimental import pallas as pl
from jax.experimental.pallas import tpu as pltpu

F32 = jnp.float32
BF16 = jnp.bfloat16

D_MODEL = 1024
ATTN_HEADS = 8
ATTN_WIDTH = 512
ATTN_HEAD_DIM = 64
MLSTM_HEADS = 4
MLSTM_WIDTH = 512
MLSTM_HEAD_DIM = 128
IN_COLS = 3 * ATTN_WIDTH + 2 * MLSTM_WIDTH
CONV_WIDTH = 4
MOBA_BLOCK = 256
MOBA_TOPK = 3
D_FF = 2816
N_MOD = 9
EPS = 1e-6

LANES = 128
SUBLANES = 8
VMEM_LIMIT_BYTES = 56 * 1024 * 1024

FFN_TOKENS = 512
FFN_CHUNK = D_FF // 2
MLSTM_CHUNK = 256
MASK_VALUE = -1e30

_NT = (((1,), (1,)), ((), ()))
_TN = (((0,), (0,)), ((), ()))


def _sigmoid(x):
    return 1.0 / (1.0 + jnp.exp(-x))


def _rmsnorm(x, w):
    return x * lax.rsqrt(jnp.mean(x * x, axis=-1, keepdims=True) + EPS) * w


def _rms_mod(x, w, shift, scale):
    return _rmsnorm(x, w) * (1.0 + scale) + shift


def _swiglu(h_bf, wg_ref, wu_ref, wd_ref, a_ref):
    for c in range(D_FF // FFN_CHUNK):
        sl = slice(c * FFN_CHUNK, (c + 1) * FFN_CHUNK)
        g = jnp.dot(h_bf, wg_ref[:, sl], preferred_element_type=F32)
        u = jnp.dot(h_bf, wu_ref[:, sl], preferred_element_type=F32)
        a_ref[:, sl] = (g * _sigmoid(g) * u).astype(BF16)
    return jnp.dot(a_ref[...], wd_ref[...], preferred_element_type=F32)


def _adaln_kernel(c_ref, w_ref, b_ref, o_ref):
    o_ref[0] = jnp.dot(c_ref[...].astype(BF16), w_ref[...].astype(BF16),
                       preferred_element_type=F32) + b_ref[...]


def _adaln(c, w_ada, b_ada):
    B = c.shape[0]
    return pl.pallas_call(
        _adaln_kernel,
        out_shape=jax.ShapeDtypeStruct((N_MOD, B, D_MODEL), F32),
        grid=(N_MOD,),
        in_specs=[
            pl.BlockSpec((B, D_MODEL), lambda j: (0, 0)),
            pl.BlockSpec((D_MODEL, D_MODEL), lambda j: (0, j)),
            pl.BlockSpec((1, D_MODEL), lambda j: (0, j)),
        ],
        out_specs=pl.BlockSpec((1, B, D_MODEL), lambda j: (j, 0, 0)),
        compiler_params=pltpu.CompilerParams(dimension_semantics=("arbitrary",),
                                             vmem_limit_bytes=VMEM_LIMIT_BYTES),
        name="adaln",
    )(c, w_ada, b_ada)


def _resident(shape):
    nd = len(shape)
    return pl.BlockSpec(shape, lambda *_: (0,) * nd, pipeline_mode=pl.Buffered(1))


def _mod_spec(k, tiles_per_seq):
    return pl.BlockSpec((1, 1, 1, D_MODEL), lambda i: (k, i // tiles_per_seq, 0, 0))


def _ffn1_kernel(x_ref, sh_ref, sc_ref, g_ref, nw_ref, wg_ref, wu_ref, wd_ref, o_ref, a_ref):
    x = x_ref[...]
    h = _rms_mod(x, nw_ref[...], sh_ref[0, 0], sc_ref[0, 0])
    y = _swiglu(h.astype(BF16), wg_ref, wu_ref, wd_ref, a_ref)
    o_ref[...] = x + (0.5 * (1.0 + g_ref[0, 0])) * y


def _ffn1(x2d, mod4, norm_w, wg, wu, wd, seq):
    T = x2d.shape[0]
    tm = FFN_TOKENS
    tps = seq // tm
    tok = pl.BlockSpec((tm, D_MODEL), lambda i: (i, 0))
    return pl.pallas_call(
        _ffn1_kernel,
        out_shape=jax.ShapeDtypeStruct((T, D_MODEL), F32),
        grid=(T // tm,),
        in_specs=[tok, _mod_spec(0, tps), _mod_spec(1, tps), _mod_spec(2, tps),
                  _resident((1, D_MODEL)),
                  _resident((D_MODEL, D_FF)), _resident((D_MODEL, D_FF)), _resident((D_FF, D_MODEL))],
        out_specs=tok,
        scratch_shapes=[pltpu.VMEM((tm, D_FF), BF16)],
        compiler_params=pltpu.CompilerParams(dimension_semantics=("arbitrary",),
                                             vmem_limit_bytes=VMEM_LIMIT_BYTES),
        name="ffn1",
    )(x2d, mod4, mod4, mod4, norm_w, wg, wu, wd)


def _proj_kernel(x_ref, sh_ref, sc_ref, nw_ref, win_ref,
                 q_ref, k_ref, v_ref, xm_ref, z_ref, km_ref):
    h = _rms_mod(x_ref[...], nw_ref[...], sh_ref[0, 0], sc_ref[0, 0]).astype(BF16)
    proj = jnp.dot(h, win_ref[...], preferred_element_type=F32)
    aw = ATTN_WIDTH
    q_ref[...] = (proj[:, 0:aw] * (ATTN_HEAD_DIM ** -0.5)).astype(BF16)
    k = proj[:, aw:2 * aw]
    k_ref[...] = k.astype(BF16)
    v_ref[...] = proj[:, 2 * aw:3 * aw].astype(BF16)
    xm_ref[...] = proj[:, 3 * aw:3 * aw + MLSTM_WIDTH]
    z_ref[...] = proj[:, 3 * aw + MLSTM_WIDTH:]
    for r in range(FFN_TOKENS // MOBA_BLOCK):
        km_ref[r] = jnp.mean(k[r * MOBA_BLOCK:(r + 1) * MOBA_BLOCK], axis=0, keepdims=True)


def _proj(x1, mod4, norm_w, w_in, seq):
    T = x1.shape[0]
    tm = FFN_TOKENS
    tps = seq // tm
    nb = tm // MOBA_BLOCK
    tok = lambda w: pl.BlockSpec((tm, w), lambda i: (i, 0))
    return pl.pallas_call(
        _proj_kernel,
        out_shape=(jax.ShapeDtypeStruct((T, ATTN_WIDTH), BF16),
                   jax.ShapeDtypeStruct((T, ATTN_WIDTH), BF16),
                   jax.ShapeDtypeStruct((T, ATTN_WIDTH), BF16),
                   jax.ShapeDtypeStruct((T, MLSTM_WIDTH), F32),
                   jax.ShapeDtypeStruct((T, MLSTM_WIDTH), F32),
                   jax.ShapeDtypeStruct((T // MOBA_BLOCK, 1, ATTN_WIDTH), F32)),
        grid=(T // tm,),
        in_specs=[tok(D_MODEL), _mod_spec(3, tps), _mod_spec(4, tps),
                  _resident((1, D_MODEL)), _resident((D_MODEL, IN_COLS))],
        out_specs=(tok(ATTN_WIDTH), tok(ATTN_WIDTH), tok(ATTN_WIDTH),
                   tok(MLSTM_WIDTH), tok(MLSTM_WIDTH),
                   pl.BlockSpec((nb, 1, ATTN_WIDTH), lambda i: (i, 0, 0))),
        compiler_params=pltpu.CompilerParams(dimension_semantics=("arbitrary",),
                                             vmem_limit_bytes=VMEM_LIMIT_BYTES),
        name="proj",
    )(x1, mod4, mod4, norm_w, w_in)


def _gate_transform(g, gate_idx):
    log_sig = jnp.minimum(g, 0.0) - jnp.log1p(jnp.exp(-jnp.abs(g)))
    return jnp.where(gate_idx >= MLSTM_HEADS, log_sig, g)


def _mlstm_kernel(xm_ref, z_ref, cw_ref, cb_ref, wqk_ref, wv_ref, wif_ref, wift_ref,
                  bifc_ref, bifr_ref, mn_ref, ms_ref, o_ref, ct_ref, m_ref, carry_ref):
    L = MLSTM_CHUNK
    dh = MLSTM_HEAD_DIM
    H = MLSTM_HEADS

    @pl.when(pl.program_id(1) == 0)
    def _():
        ct_ref[...] = jnp.zeros_like(ct_ref)
        m_ref[...] = jnp.zeros_like(m_ref)
        carry_ref[...] = jnp.zeros_like(carry_ref)

    xm = xm_ref[0]
    z = z_ref[0]

    carry = carry_ref[...]
    cw = cw_ref[...]
    acc = xm * cw[CONV_WIDTH - 1:CONV_WIDTH] + cb_ref[...]
    row8 = lax.broadcasted_iota(jnp.int32, (SUBLANES, MLSTM_WIDTH), 0)
    for k in range(1, CONV_WIDTH):
        rolled = pltpu.roll(xm, k, 0)
        top = jnp.where(row8 < k, pltpu.roll(carry, k, 0), rolled[0:SUBLANES])
        shifted = jnp.concatenate([top, rolled[SUBLANES:]], axis=0)
        acc = acc + shifted * cw[CONV_WIDTH - 1 - k:CONV_WIDTH - k]
    carry_ref[...] = xm[L - SUBLANES:L]
    xc = acc * _sigmoid(acc)

    xc_bf = xc.astype(BF16)
    xm_bf = xm.astype(BF16)
    qk, vv, cat = [], [], []
    for h in range(H):
        hs = slice(h * dh, (h + 1) * dh)
        qk_h = jnp.dot(xc_bf[:, hs], wqk_ref[h], preferred_element_type=F32)
        v_h = jnp.dot(xm_bf[:, hs], wv_ref[h], preferred_element_type=F32)
        qk.append(qk_h)
        vv.append(v_h)
        cat += [qk_h.astype(BF16), v_h.astype(BF16)]
    cat = jnp.concatenate(cat, axis=1)

    g_col = jnp.dot(cat, wif_ref[...], preferred_element_type=F32) + bifc_ref[...]
    g_row = lax.dot_general(wift_ref[...], cat, _NT, preferred_element_type=F32) + bifr_ref[...]
    a_col = _gate_transform(g_col, lax.broadcasted_iota(jnp.int32, (L, 2 * H), 1))
    a_row = _gate_transform(g_row, lax.broadcasted_iota(jnp.int32, (2 * H, L), 0))

    t_idx = lax.broadcasted_iota(jnp.int32, (L, L), 0)
    s_idx = lax.broadcasted_iota(jnp.int32, (L, L), 1)
    causal = s_idx <= t_idx
    tri = causal.astype(F32)
    tri_t = (t_idx <= s_idx).astype(F32)
    b_col = jnp.dot(tri, a_col, preferred_element_type=F32, precision=lax.Precision.HIGHEST)
    b_row = jnp.dot(a_row, tri_t, preferred_element_type=F32, precision=lax.Precision.HIGHEST)

    ones_blk = (lax.broadcasted_iota(jnp.int32, (L, LANES), 1) == 0).astype(F32)
    outs = []
    for h in range(H):
        hs = slice(h * dh, (h + 1) * dh)
        q_bf = qk[h][:, :dh].astype(BF16)
        k_bf = (qk[h][:, dh:] * (dh ** -0.5)).astype(BF16)
        v_aug = jnp.concatenate([vv[h], ones_blk], axis=1)
        b_c = b_col[:, H + h:H + h + 1]
        i_c = a_col[:, h:h + 1]
        b_r = b_row[H + h:H + h + 1, :]
        i_r = a_row[h:h + 1, :]
        m_prev = m_ref[h, 0:1, 0:1]

        log_d = jnp.where(causal, b_c - b_r + i_r, -jnp.inf)
        log_inter = b_c + m_prev
        m_t = jnp.maximum(log_inter, jnp.max(log_d, axis=1, keepdims=True))
        w_intra = jnp.exp(log_d - m_t)
        w_inter = jnp.exp(log_inter - m_t)
        s = lax.dot_general(q_bf, k_bf, _NT, preferred_element_type=F32) * w_intra
        ct = ct_ref[h]
        comb = (jnp.dot(s.astype(BF16), v_aug.astype(BF16), preferred_element_type=F32)
                + w_inter * jnp.dot(q_bf, ct.astype(BF16), preferred_element_type=F32))
        num = comb[:, :dh]
        den = comb[:, dh:dh + 1]
        hh = num / jnp.maximum(jnp.abs(den), jnp.exp(-m_t))

        b_last = b_c[L - 1:L, :]
        log_g = b_last - b_c + i_c
        m_new = jnp.maximum(b_last + m_prev, jnp.max(log_g, axis=0, keepdims=True))
        w_g = jnp.exp(log_g - m_new)
        decay = jnp.exp(b_last + m_prev - m_new)
        upd = lax.dot_general(k_bf, (v_aug * w_g).astype(BF16), _TN, preferred_element_type=F32)
        ct_ref[h] = decay * ct + upd
        m_ref[h] = jnp.broadcast_to(m_new, (SUBLANES, LANES))

        mu = jnp.mean(hh, axis=-1, keepdims=True)
        dlt = hh - mu
        var = jnp.mean(dlt * dlt, axis=-1, keepdims=True)
        hn = dlt * lax.rsqrt(var + EPS) * mn_ref[:, hs]
        zh = z[:, hs]
        outs.append(((hn + ms_ref[:, hs] * xc[:, hs]) * (zh * _sigmoid(zh))).astype(BF16))
    o_ref[0] = jnp.concatenate(outs, axis=1)


def _mlstm(xm, z, conv_w, conv_b, wqk, wv, wif, wif_t, bif_c, bif_r, mnorm, mskip):
    B, S, W = xm.shape
    L = MLSTM_CHUNK
    tok = pl.BlockSpec((1, L, W), lambda b, j: (b, j, 0))
    return pl.pallas_call(
        _mlstm_kernel,
        out_shape=jax.ShapeDtypeStruct((B, S, W), BF16),
        grid=(B, S // L),
        in_specs=[tok, tok,
                  _resident(conv_w.shape), _resident(conv_b.shape), _resident(wqk.shape), _resident(wv.shape),
                  _resident(wif.shape), _resident(wif_t.shape), _resident(bif_c.shape), _resident(bif_r.shape),
                  _resident(mnorm.shape), _resident(mskip.shape)],
        out_specs=tok,
        scratch_shapes=[pltpu.VMEM((MLSTM_HEADS, MLSTM_HEAD_DIM, 2 * MLSTM_HEAD_DIM), F32),
                        pltpu.VMEM((MLSTM_HEADS, SUBLANES, LANES), F32),
                        pltpu.VMEM((SUBLANES, W), F32)],
        compiler_params=pltpu.CompilerParams(dimension_semantics=("arbitrary", "arbitrary"),
                                             vmem_limit_bytes=VMEM_LIMIT_BYTES),
        name="mlstm",
    )(xm, z, conv_w, conv_b, wqk, wv, wif, wif_t, bif_c, bif_r, mnorm, mskip)


def _moba_kernel(q_ref, k_ref, v_ref, km_ref, o_ref):
    BS = MOBA_BLOCK
    i = pl.program_id(2)
    nb = km_ref.shape[1]
    q = q_ref[0]
    km = jnp.concatenate([km_ref[0], jnp.zeros((LANES - nb, LANES), F32)], axis=0)
    km_hi = km.astype(BF16)
    km_lo = (km - km_hi.astype(F32)).astype(BF16)
    lane = lax.broadcasted_iota(jnp.int32, (BS, LANES), 1)
    row = lax.broadcasted_iota(jnp.int32, (BS, LANES), 0)
    ones_blk = (lane == 0).astype(BF16)
    k_own = k_ref[0, pl.ds(pl.multiple_of(i * BS, BS), BS), :]
    v_own = jnp.concatenate([v_ref[0, pl.ds(pl.multiple_of(i * BS, BS), BS), :], ones_blk], axis=1)
    own_mask = (lax.broadcasted_iota(jnp.int32, (BS, BS), 1)
                <= lax.broadcasted_iota(jnp.int32, (BS, BS), 0))

    outs = []
    for hh in range(2):
        in_head = (lane >= hh * ATTN_HEAD_DIM) & (lane < (hh + 1) * ATTN_HEAD_DIM)
        qh = jnp.where(in_head, q, jnp.zeros_like(q))

        gate = (lax.dot_general(qh, km_hi, _NT, preferred_element_type=F32)
                + lax.dot_general(qh, km_lo, _NT, preferred_element_type=F32))
        cur = jnp.where(lane < i, gate, -jnp.inf)
        bias = jnp.full((BS, LANES), MASK_VALUE, F32)
        for _ in range(MOBA_TOPK):
            mx = jnp.max(cur, axis=1, keepdims=True)
            first = jnp.min(jnp.where(cur == mx, lane, LANES), axis=1, keepdims=True)
            pick = (lane == first) & (mx > -jnp.inf)
            bias = jnp.where(pick, 0.0, bias)
            cur = jnp.where(pick, -jnp.inf, cur)
        bias = jnp.where(lane < nb, bias, 0.0)
        q_aug = jnp.concatenate([qh, bias.astype(BF16)], axis=1)

        s = lax.dot_general(qh, k_own, _NT, preferred_element_type=F32)
        s = jnp.where(own_mask, s, -jnp.inf)
        m0 = jnp.max(s, axis=1, keepdims=True)
        pv = jnp.dot(jnp.exp(s - m0).astype(BF16), v_own, preferred_element_type=F32)
        acc0 = pv[:, :LANES]
        l0 = pv[:, LANES:LANES + 1]

        def body(j, carry):
            m, l, acc = carry
            start = pl.multiple_of(j * BS, BS)
            k_aug = jnp.concatenate([k_ref[0, pl.ds(start, BS), :], (lane == j).astype(BF16)], axis=1)
            v_aug = jnp.concatenate([v_ref[0, pl.ds(start, BS), :], ones_blk], axis=1)
            sj = lax.dot_general(q_aug, k_aug, _NT, preferred_element_type=F32)
            m_new = jnp.maximum(m, jnp.max(sj, axis=1, keepdims=True))
            alpha = jnp.exp(m - m_new)
            pvj = jnp.dot(jnp.exp(sj - m_new).astype(BF16), v_aug, preferred_element_type=F32)
            return m_new, alpha * l + pvj[:, LANES:LANES + 1], alpha * acc + pvj[:, :LANES]

        _, l, acc = lax.fori_loop(0, i, body, (m0, l0, acc0))
        outs.append(acc / l)
    o_ref[0] = jnp.where(lane < ATTN_HEAD_DIM, outs[0], outs[1])


def _moba(q, k, v, kmean):
    B, S, W = q.shape
    BS = MOBA_BLOCK
    nb = S // BS
    npairs = W // LANES
    qspec = pl.BlockSpec((1, BS, LANES), lambda b, p, i: (b, i, p))
    kvspec = pl.BlockSpec((1, S, LANES), lambda b, p, i: (b, 0, p))
    return pl.pallas_call(
        _moba_kernel,
        out_shape=jax.ShapeDtypeStruct((B, S, W), F32),
        grid=(B, npairs, nb),
        in_specs=[qspec, kvspec, kvspec, pl.BlockSpec((1, nb, LANES), lambda b, p, i: (b, 0, p))],
        out_specs=qspec,
        compiler_params=pltpu.CompilerParams(dimension_semantics=("arbitrary",) * 3,
                                             vmem_limit_bytes=VMEM_LIMIT_BYTES),
        name="moba",
    )(q, k, v, kmean)


def _out_ffn2_kernel(x1_ref, oa_ref, om_ref, g2_ref, sh_ref, sc_ref, g3_ref, an_ref, wo_ref,
                     nw_ref, wg_ref, wu_ref, wd_ref, fn_ref, o_ref, a_ref):
    oa = _rmsnorm(oa_ref[...], an_ref[...]).astype(BF16)
    mix = (jnp.dot(oa, wo_ref[0:ATTN_WIDTH, :], preferred_element_type=F32)
           + jnp.dot(om_ref[...], wo_ref[ATTN_WIDTH:, :], preferred_element_type=F32))
    x2 = x1_ref[...] + (1.0 + g2_ref[0, 0]) * mix
    h = _rms_mod(x2, nw_ref[...], sh_ref[0, 0], sc_ref[0, 0])
    y = _swiglu(h.astype(BF16), wg_ref, wu_ref, wd_ref, a_ref)
    x3 = x2 + (0.5 * (1.0 + g3_ref[0, 0])) * y
    o_ref[...] = _rmsnorm(x3, fn_ref[...])


def _out_ffn2(x1, oa, om, mod4, attn_norm, w_out, norm_w, wg, wu, wd, final_norm, seq):
    T = x1.shape[0]
    tm = FFN_TOKENS
    tps = seq // tm
    tok = lambda w: pl.BlockSpec((tm, w), lambda i: (i, 0))
    return pl.pallas_call(
        _out_ffn2_kernel,
        out_shape=jax.ShapeDtypeStruct((T, D_MODEL), F32),
        grid=(T // tm,),
        in_specs=[tok(D_MODEL), tok(ATTN_WIDTH), tok(MLSTM_WIDTH),
                  _mod_spec(5, tps), _mod_spec(6, tps), _mod_spec(7, tps), _mod_spec(8, tps),
                  _resident((1, ATTN_WIDTH)), _resident((D_MODEL, D_MODEL)), _resident((1, D_MODEL)),
                  _resident((D_MODEL, D_FF)), _resident((D_MODEL, D_FF)), _resident((D_FF, D_MODEL)),
                  _resident((1, D_MODEL))],
        out_specs=tok(D_MODEL),
        scratch_shapes=[pltpu.VMEM((tm, D_FF), BF16)],
        compiler_params=pltpu.CompilerParams(dimension_semantics=("arbitrary",),
                                             vmem_limit_bytes=VMEM_LIMIT_BYTES),
        name="out_ffn2",
    )(x1, oa, om, mod4, mod4, mod4, mod4, attn_norm, w_out, norm_w, wg, wu, wd, final_norm)


def kernel(x, c, w_ada, b_ada, ffn1_norm, ffn1_w_gate, ffn1_w_up, ffn1_w_down, mix_norm, w_in, conv_w, conv_b,
           w_q_m, w_k_m, w_v_m, w_if, b_if, mlstm_norm, mlstm_skip, attn_norm, w_out,
           ffn2_norm, ffn2_w_gate, ffn2_w_up, ffn2_w_down, final_norm):
    B, S, D = x.shape
    T = B * S
    assert w_ada.shape[0] == 1, "only depth 1 is supported"
    bf = lambda a: a.astype(BF16)
    xf = x.reshape(T, D)
    for l in range(1):
        mod4 = _adaln(c, w_ada[l], b_ada[l][None, :]).reshape(N_MOD, B, 1, D)
        x1 = _ffn1(xf, mod4, ffn1_norm[l][None, :], bf(ffn1_w_gate[l]), bf(ffn1_w_up[l]), bf(ffn1_w_down[l]), S)
        q, k, v, xm, z, kmean = _proj(x1, mod4, mix_norm[l][None, :], bf(w_in[l]), S)
        o_m = _mlstm(xm.reshape(B, S, MLSTM_WIDTH), z.reshape(B, S, MLSTM_WIDTH),
                     conv_w[l], conv_b[l][None, :],
                     bf(jnp.concatenate([w_q_m[l], w_k_m[l]], axis=-1)), bf(w_v_m[l]),
                     bf(w_if[l]), bf(w_if[l].T), b_if[l][None, :], b_if[l][:, None],
                     mlstm_norm[l][None, :], mlstm_skip[l][None, :])
        o_a = _moba(q.reshape(B, S, ATTN_WIDTH), k.reshape(B, S, ATTN_WIDTH), v.reshape(B, S, ATTN_WIDTH),
                    kmean.reshape(B, S // MOBA_BLOCK, ATTN_WIDTH))
        xf = _out_ffn2(x1, o_a.reshape(T, ATTN_WIDTH), o_m.reshape(T, MLSTM_WIDTH), mod4,
                       attn_norm[l][None, :], bf(w_out[l]), ffn2_norm[l][None, :],
                       bf(ffn2_w_gate[l]), bf(ffn2_w_up[l]), bf(ffn2_w_down[l]), final_norm[None, :], S)
    return xf.reshape(B, S, D)
```

```python
import functools

import jax
import jax.numpy as jnp
from jax import lax
from jax.experimental import pallas as pl
from jax.experimental.pallas import tpu as pltpu

F32 = jnp.float32
BF16 = jnp.bfloat16

D_MODEL = 1024
ATTN_HEADS = 8
ATTN_WIDTH = 512
ATTN_HEAD_DIM = 64
MLSTM_HEADS = 4
MLSTM_WIDTH = 512
MLSTM_HEAD_DIM = 128
IN_COLS = 3 * ATTN_WIDTH + 2 * MLSTM_WIDTH
CONV_WIDTH = 4
MOBA_BLOCK = 256
MOBA_TOPK = 3
D_FF = 2816
N_MOD = 9
EPS = 1e-6

LANES = 128
SUBLANES = 8
VMEM_LIMIT_BYTES = 56 * 1024 * 1024

FFN_TOKENS = 512
MXU_WIDTH = 256
FFN_CHUNKS = ((0, 6 * MXU_WIDTH), (6 * MXU_WIDTH, D_FF))
MLSTM_CHUNK = 256
MOBA_GROUP = 1
MOBA_LOOKAHEAD = 2
VT_ROWS = LANES + 16
MASK_VALUE = -1e30

_NT = (((1,), (1,)), ((), ()))
_TN = (((0,), (0,)), ((), ()))


def _sigmoid(x):
    return 1.0 / (1.0 + jnp.exp(-x))


def _rmsnorm(x, w):
    return x * lax.rsqrt(jnp.mean(x * x, axis=-1, keepdims=True) + EPS) * w


def _rms_mod(x, w, shift, scale):
    return _rmsnorm(x, w) * (1.0 + scale) + shift


def _swiglu(h_bf, wg_ref, wu_ref, wd_ref, a_ref):
    for lo, hi in FFN_CHUNKS:
        sl = slice(lo, hi)
        g = jnp.dot(h_bf, wg_ref[:, sl], preferred_element_type=F32)
        u = jnp.dot(h_bf, wu_ref[:, sl], preferred_element_type=F32)
        a_ref[:, sl] = (g * _sigmoid(g) * u).astype(BF16)
    return jnp.dot(a_ref[...], wd_ref[...], preferred_element_type=F32)


def _adaln_kernel(c_ref, w_ref, b_ref, o_ref):
    o_ref[0] = jnp.dot(c_ref[...].astype(BF16), w_ref[...].astype(BF16),
                       preferred_element_type=F32) + b_ref[...]


def _adaln(c, w_ada, b_ada):
    B = c.shape[0]
    return pl.pallas_call(
        _adaln_kernel,
        out_shape=jax.ShapeDtypeStruct((N_MOD, B, D_MODEL), F32),
        grid=(N_MOD,),
        in_specs=[
            pl.BlockSpec((B, D_MODEL), lambda j: (0, 0)),
            pl.BlockSpec((D_MODEL, D_MODEL), lambda j: (0, j)),
            pl.BlockSpec((1, D_MODEL), lambda j: (0, j)),
        ],
        out_specs=pl.BlockSpec((1, B, D_MODEL), lambda j: (j, 0, 0)),
        compiler_params=pltpu.CompilerParams(dimension_semantics=("arbitrary",),
                                             vmem_limit_bytes=VMEM_LIMIT_BYTES),
        name="adaln",
    )(c, w_ada, b_ada)


def _resident(shape):
    nd = len(shape)
    return pl.BlockSpec(shape, lambda *_: (0,) * nd, pipeline_mode=pl.Buffered(1))


def _mod_spec(k, tiles_per_seq):
    return pl.BlockSpec((1, 1, 1, D_MODEL), lambda i: (k, i // tiles_per_seq, 0, 0))


def _ffn1_kernel(x_ref, sh_ref, sc_ref, g_ref, nw_ref, wg_ref, wu_ref, wd_ref, o_ref, a_ref):
    x = x_ref[...]
    h = _rms_mod(x, nw_ref[...], sh_ref[0, 0], sc_ref[0, 0])
    y = _swiglu(h.astype(BF16), wg_ref, wu_ref, wd_ref, a_ref)
    o_ref[...] = x + (0.5 * (1.0 + g_ref[0, 0])) * y


def _ffn1(x2d, mod4, norm_w, wg, wu, wd, seq):
    T = x2d.shape[0]
    tm = FFN_TOKENS
    tps = seq // tm
    tok = pl.BlockSpec((tm, D_MODEL), lambda i: (i, 0))
    return pl.pallas_call(
        _ffn1_kernel,
        out_shape=jax.ShapeDtypeStruct((T, D_MODEL), F32),
        grid=(T // tm,),
        in_specs=[tok, _mod_spec(0, tps), _mod_spec(1, tps), _mod_spec(2, tps),
                  _resident((1, D_MODEL)),
                  _resident((D_MODEL, D_FF)), _resident((D_MODEL, D_FF)), _resident((D_FF, D_MODEL))],
        out_specs=tok,
        scratch_shapes=[pltpu.VMEM((tm, D_FF), BF16)],
        compiler_params=pltpu.CompilerParams(dimension_semantics=("arbitrary",),
                                             vmem_limit_bytes=VMEM_LIMIT_BYTES),
        name="ffn1",
    )(x2d, mod4, mod4, mod4, norm_w, wg, wu, wd)


def _proj_kernel(x_ref, sh_ref, sc_ref, nw_ref, win_ref,
                 q_ref, k_ref, v_ref, xm_ref, z_ref, km_ref):
    h = _rms_mod(x_ref[...], nw_ref[...], sh_ref[0, 0], sc_ref[0, 0]).astype(BF16)
    proj = jnp.dot(h, win_ref[...], preferred_element_type=F32)
    aw = ATTN_WIDTH
    q_ref[...] = (proj[:, 0:aw] * (ATTN_HEAD_DIM ** -0.5)).astype(BF16)
    k = proj[:, aw:2 * aw]
    k_ref[...] = k.astype(BF16)
    v_ref[...] = proj[:, 2 * aw:3 * aw].astype(BF16)
    xm_ref[...] = proj[:, 3 * aw:3 * aw + MLSTM_WIDTH]
    z_ref[...] = proj[:, 3 * aw + MLSTM_WIDTH:]
    for r in range(FFN_TOKENS // MOBA_BLOCK):
        km_ref[r] = jnp.mean(k[r * MOBA_BLOCK:(r + 1) * MOBA_BLOCK], axis=0, keepdims=True)


def _proj(x1, mod4, norm_w, w_in, seq):
    T = x1.shape[0]
    tm = FFN_TOKENS
    tps = seq // tm
    nb = tm // MOBA_BLOCK
    tok = lambda w: pl.BlockSpec((tm, w), lambda i: (i, 0))
    return pl.pallas_call(
        _proj_kernel,
        out_shape=(jax.ShapeDtypeStruct((T, ATTN_WIDTH), BF16),
                   jax.ShapeDtypeStruct((T, ATTN_WIDTH), BF16),
                   jax.ShapeDtypeStruct((T, ATTN_WIDTH), BF16),
                   jax.ShapeDtypeStruct((T, MLSTM_WIDTH), F32),
                   jax.ShapeDtypeStruct((T, MLSTM_WIDTH), F32),
                   jax.ShapeDtypeStruct((T // MOBA_BLOCK, 1, ATTN_WIDTH), F32)),
        grid=(T // tm,),
        in_specs=[tok(D_MODEL), _mod_spec(3, tps), _mod_spec(4, tps),
                  _resident((1, D_MODEL)), _resident((D_MODEL, IN_COLS))],
        out_specs=(tok(ATTN_WIDTH), tok(ATTN_WIDTH), tok(ATTN_WIDTH),
                   tok(MLSTM_WIDTH), tok(MLSTM_WIDTH),
                   pl.BlockSpec((nb, 1, ATTN_WIDTH), lambda i: (i, 0, 0))),
        compiler_params=pltpu.CompilerParams(dimension_semantics=("arbitrary",),
                                             vmem_limit_bytes=VMEM_LIMIT_BYTES),
        name="proj",
    )(x1, mod4, mod4, norm_w, w_in)


def _gate_transform(g, gate_idx):
    log_sig = jnp.minimum(g, 0.0) - jnp.log1p(jnp.exp(-jnp.abs(g)))
    return jnp.where(gate_idx >= MLSTM_HEADS, log_sig, g)


def _mlstm_kernel(xm_ref, z_ref, cw_ref, cb_ref, wqk_ref, wv_ref, wif_ref, wift_ref,
                  bifc_ref, bifr_ref, mn_ref, ms_ref, o_ref, ct_ref, m_ref, carry_ref):
    L = MLSTM_CHUNK
    dh = MLSTM_HEAD_DIM
    H = MLSTM_HEADS

    @pl.when(pl.program_id(1) == 0)
    def _():
        ct_ref[...] = jnp.zeros_like(ct_ref)
        m_ref[...] = jnp.zeros_like(m_ref)
        carry_ref[...] = jnp.zeros_like(carry_ref)

    xm = xm_ref[0]
    z = z_ref[0]

    carry = carry_ref[...]
    cw = cw_ref[...]
    acc = xm * cw[CONV_WIDTH - 1:CONV_WIDTH] + cb_ref[...]
    row8 = lax.broadcasted_iota(jnp.int32, (SUBLANES, MLSTM_WIDTH), 0)
    for k in range(1, CONV_WIDTH):
        rolled = pltpu.roll(xm, k, 0)
        top = jnp.where(row8 < k, pltpu.roll(carry, k, 0), rolled[0:SUBLANES])
        shifted = jnp.concatenate([top, rolled[SUBLANES:]], axis=0)
        acc = acc + shifted * cw[CONV_WIDTH - 1 - k:CONV_WIDTH - k]
    carry_ref[...] = xm[L - SUBLANES:L]
    xc = acc * _sigmoid(acc)

    xc_bf = xc.astype(BF16)
    xm_bf = xm.astype(BF16)
    qk, vv, cat = [], [], []
    for h in range(H):
        hs = slice(h * dh, (h + 1) * dh)
        qk_h = jnp.dot(xc_bf[:, hs], wqk_ref[h], preferred_element_type=F32)
        v_h = jnp.dot(xm_bf[:, hs], wv_ref[h], preferred_element_type=F32)
        qk.append(qk_h)
        vv.append(v_h)
        cat += [qk_h.astype(BF16), v_h.astype(BF16)]
    cat = jnp.concatenate(cat, axis=1)

    g_col = jnp.dot(cat, wif_ref[...], preferred_element_type=F32) + bifc_ref[...]
    g_row = lax.dot_general(wift_ref[...], cat, _NT, preferred_element_type=F32) + bifr_ref[...]
    a_col = _gate_transform(g_col, lax.broadcasted_iota(jnp.int32, (L, 2 * H), 1))
    a_row = _gate_transform(g_row, lax.broadcasted_iota(jnp.int32, (2 * H, L), 0))

    t_idx = lax.broadcasted_iota(jnp.int32, (L, L), 0)
    s_idx = lax.broadcasted_iota(jnp.int32, (L, L), 1)
    causal = s_idx <= t_idx
    tri = causal.astype(F32)
    tri_t = (t_idx <= s_idx).astype(F32)
    b_col = jnp.dot(tri, a_col, preferred_element_type=F32, precision=lax.Precision.HIGHEST)
    b_row = jnp.dot(a_row, tri_t, preferred_element_type=F32, precision=lax.Precision.HIGHEST)

    ones_blk = (lax.broadcasted_iota(jnp.int32, (L, LANES), 1) == 0).astype(F32)
    outs = []
    for h in range(H):
        hs = slice(h * dh, (h + 1) * dh)
        q_bf = qk[h][:, :dh].astype(BF16)
        k_bf = (qk[h][:, dh:] * (dh ** -0.5)).astype(BF16)
        v_aug = jnp.concatenate([vv[h], ones_blk], axis=1)
        b_c = b_col[:, H + h:H + h + 1]
        i_c = a_col[:, h:h + 1]
        b_r = b_row[H + h:H + h + 1, :]
        i_r = a_row[h:h + 1, :]
        m_prev = m_ref[h, 0:1, 0:1]

        log_d = jnp.where(causal, b_c - b_r + i_r, -jnp.inf)
        log_inter = b_c + m_prev
        m_t = jnp.maximum(log_inter, jnp.max(log_d, axis=1, keepdims=True))
        w_intra = jnp.exp(log_d - m_t)
        w_inter = jnp.exp(log_inter - m_t)
        s = lax.dot_general(q_bf, k_bf, _NT, preferred_element_type=F32) * w_intra
        ct = ct_ref[h]
        comb = (jnp.dot(s.astype(BF16), v_aug.astype(BF16), preferred_element_type=F32)
                + w_inter * jnp.dot(q_bf, ct.astype(BF16), preferred_element_type=F32))
        num = comb[:, :dh]
        den = comb[:, dh:dh + 1]
        hh = num / jnp.maximum(jnp.abs(den), jnp.exp(-m_t))

        b_last = b_c[L - 1:L, :]
        log_g = b_last - b_c + i_c
        m_new = jnp.maximum(b_last + m_prev, jnp.max(log_g, axis=0, keepdims=True))
        w_g = jnp.exp(log_g - m_new)
        decay = jnp.exp(b_last + m_prev - m_new)
        upd = lax.dot_general(k_bf, (v_aug * w_g).astype(BF16), _TN, preferred_element_type=F32)
        ct_ref[h] = decay * ct + upd
        m_ref[h] = jnp.broadcast_to(m_new, (SUBLANES, LANES))

        mu = jnp.mean(hh, axis=-1, keepdims=True)
        dlt = hh - mu
        var = jnp.mean(dlt * dlt, axis=-1, keepdims=True)
        hn = dlt * lax.rsqrt(var + EPS) * mn_ref[:, hs]
        zh = z[:, hs]
        outs.append(((hn + ms_ref[:, hs] * xc[:, hs]) * (zh * _sigmoid(zh))).astype(BF16))
    o_ref[0] = jnp.concatenate(outs, axis=1)


def _mlstm(xm, z, conv_w, conv_b, wqk, wv, wif, wif_t, bif_c, bif_r, mnorm, mskip):
    B, S, W = xm.shape
    L = MLSTM_CHUNK
    tok = pl.BlockSpec((1, L, W), lambda b, j: (b, j, 0))
    return pl.pallas_call(
        _mlstm_kernel,
        out_shape=jax.ShapeDtypeStruct((B, S, W), BF16),
        grid=(B, S // L),
        in_specs=[tok, tok,
                  _resident(conv_w.shape), _resident(conv_b.shape), _resident(wqk.shape), _resident(wv.shape),
                  _resident(wif.shape), _resident(wif_t.shape), _resident(bif_c.shape), _resident(bif_r.shape),
                  _resident(mnorm.shape), _resident(mskip.shape)],
        out_specs=tok,
        scratch_shapes=[pltpu.VMEM((MLSTM_HEADS, MLSTM_HEAD_DIM, 2 * MLSTM_HEAD_DIM), F32),
                        pltpu.VMEM((MLSTM_HEADS, SUBLANES, LANES), F32),
                        pltpu.VMEM((SUBLANES, W), F32)],
        compiler_params=pltpu.CompilerParams(dimension_semantics=("arbitrary", "arbitrary"),
                                             vmem_limit_bytes=VMEM_LIMIT_BYTES),
        name="mlstm",
    )(xm, z, conv_w, conv_b, wqk, wv, wif, wif_t, bif_c, bif_r, mnorm, mskip)


def _moba_kernel(q_ref, k_ref, v_ref, km_ref, o_ref, kaug_ref, vt_ref, qaug_ref):
    BS = MOBA_BLOCK
    GW = MOBA_GROUP * BS
    S = k_ref.shape[1]
    nb = km_ref.shape[1]
    i = pl.program_id(2)

    hd = ATTN_HEAD_DIM

    @pl.when(i == 0)
    def _():
        lane_s = lax.broadcasted_iota(jnp.int32, (S, LANES), 1)
        blk_s = lax.shift_right_logical(lax.broadcasted_iota(jnp.int32, (S, LANES), 0), BS.bit_length() - 1)
        kaug_ref[:, 0:LANES] = k_ref[0]
        kaug_ref[:, LANES:] = (lane_s == blk_s).astype(BF16)
        vt_ref[0:LANES, :] = jnp.transpose(v_ref[0].astype(F32)).astype(BF16)
        vt_ref[LANES:, :] = (lax.broadcasted_iota(jnp.int32, (VT_ROWS - LANES, S), 0) == 0).astype(BF16)

    q_t = jnp.transpose(q_ref[0].astype(F32))
    km = km_ref[0]
    km_hi = km.astype(BF16)
    km_lo = (km - km_hi.astype(F32)).astype(BF16)
    dim_t = lax.broadcasted_iota(jnp.int32, (LANES, BS), 0)
    blk_t = lax.broadcasted_iota(jnp.int32, (nb, BS), 0)

    for hh in range(2):
        qh_t = jnp.where((dim_t >= hh * hd) & (dim_t < (hh + 1) * hd), q_t, 0.0).astype(BF16)
        gate = (jnp.dot(km_hi, qh_t, preferred_element_type=F32)
                + jnp.dot(km_lo, qh_t, preferred_element_type=F32))
        cur = jnp.where(blk_t < i, gate, -jnp.inf)
        bias_t = jnp.full((nb, BS), MASK_VALUE, F32)
        for _ in range(MOBA_TOPK):
            mx = jnp.max(cur, axis=0, keepdims=True)
            first = jnp.min(jnp.where(cur == mx, blk_t, nb), axis=0, keepdims=True)
            pick = (blk_t == first) & (mx > -jnp.inf)
            bias_t = jnp.where(pick, 0.0, bias_t)
            cur = jnp.where(pick, -jnp.inf, cur)
        bias_t = jnp.where(blk_t == i, 0.0, bias_t)
        qaug_ref[:, hh * BS:(hh + 1) * BS] = jnp.concatenate(
            [qh_t, bias_t.astype(BF16), jnp.zeros((LANES - nb, BS), BF16)], axis=0)

    g_last = lax.shift_right_logical(i, MOBA_GROUP.bit_length() - 1)
    rel = (i - g_last * MOBA_GROUP) * BS
    key_minus_query = (lax.broadcasted_iota(jnp.int32, (GW, 2 * BS), 0)
                       - (lax.broadcasted_iota(jnp.int32, (GW, 2 * BS), 1) & (BS - 1)))
    for c in range(nb // MOBA_GROUP):
        @pl.when(g_last == c)
        def _():
            def scores(g):
                return jnp.dot(kaug_ref[g * GW:(g + 1) * GW, :], qaug_ref[...],
                               preferred_element_type=F32)

            m = pv = None
            order = list(range(c, -1, -1))
            pending = [scores(g) for g in order[:MOBA_LOOKAHEAD]]
            for idx, g in enumerate(order):
                ks = slice(g * GW, (g + 1) * GW)
                s = pending.pop(0)
                if idx + MOBA_LOOKAHEAD < len(order):
                    pending.append(scores(order[idx + MOBA_LOOKAHEAD]))
                if g == c:
                    s = jnp.where(key_minus_query <= rel, s, MASK_VALUE)
                cm = jnp.max(s, axis=0, keepdims=True)
                if m is None:
                    m = cm
                    pv = jnp.dot(vt_ref[:, ks], jnp.exp(s - m).astype(BF16), preferred_element_type=F32)
                else:
                    m_new = jnp.maximum(m, cm)
                    pv = (jnp.exp(m - m_new) * pv
                          + jnp.dot(vt_ref[:, ks], jnp.exp(s - m_new).astype(BF16), preferred_element_type=F32))
                    m = m_new
            out_t = jnp.concatenate([pv[0:hd, 0:BS] / pv[LANES:LANES + 1, 0:BS],
                                     pv[hd:LANES, BS:] / pv[LANES:LANES + 1, BS:]], axis=0)
            o_ref[0] = jnp.transpose(out_t)


def _moba(q, k, v, kmean):
    B, S, W = q.shape
    BS = MOBA_BLOCK
    nb = S // BS
    npairs = W // LANES
    assert nb % MOBA_GROUP == 0 and nb <= LANES
    qspec = pl.BlockSpec((1, BS, LANES), lambda b, p, i: (b, i, p))
    kvspec = pl.BlockSpec((1, S, LANES), lambda b, p, i: (b, 0, p))
    return pl.pallas_call(
        _moba_kernel,
        out_shape=jax.ShapeDtypeStruct((B, S, W), F32),
        grid=(B, npairs, nb),
        in_specs=[qspec, kvspec, kvspec, pl.BlockSpec((1, nb, LANES), lambda b, p, i: (b, 0, p))],
        out_specs=qspec,
        scratch_shapes=[pltpu.VMEM((S, 2 * LANES), BF16),
                        pltpu.VMEM((VT_ROWS, S), BF16),
                        pltpu.VMEM((2 * LANES, 2 * BS), BF16)],
        compiler_params=pltpu.CompilerParams(dimension_semantics=("arbitrary",) * 3,
                                             vmem_limit_bytes=VMEM_LIMIT_BYTES),
        name="moba",
    )(q, k, v, kmean)


def _out_ffn2_kernel(x1_ref, oa_ref, om_ref, g2_ref, sh_ref, sc_ref, g3_ref, an_ref, wo_ref,
                     nw_ref, wg_ref, wu_ref, wd_ref, fn_ref, o_ref, a_ref):
    oa = _rmsnorm(oa_ref[...], an_ref[...]).astype(BF16)
    mix = (jnp.dot(oa, wo_ref[0:ATTN_WIDTH, :], preferred_element_type=F32)
           + jnp.dot(om_ref[...], wo_ref[ATTN_WIDTH:, :], preferred_element_type=F32))
    x2 = x1_ref[...] + (1.0 + g2_ref[0, 0]) * mix
    h = _rms_mod(x2, nw_ref[...], sh_ref[0, 0], sc_ref[0, 0])
    y = _swiglu(h.astype(BF16), wg_ref, wu_ref, wd_ref, a_ref)
    x3 = x2 + (0.5 * (1.0 + g3_ref[0, 0])) * y
    o_ref[...] = _rmsnorm(x3, fn_ref[...])


def _out_ffn2(x1, oa, om, mod4, attn_norm, w_out, norm_w, wg, wu, wd, final_norm, seq):
    T = x1.shape[0]
    tm = FFN_TOKENS
    tps = seq // tm
    tok = lambda w: pl.BlockSpec((tm, w), lambda i: (i, 0))
    return pl.pallas_call(
        _out_ffn2_kernel,
        out_shape=jax.ShapeDtypeStruct((T, D_MODEL), F32),
        grid=(T // tm,),
        in_specs=[tok(D_MODEL), tok(ATTN_WIDTH), tok(MLSTM_WIDTH),
                  _mod_spec(5, tps), _mod_spec(6, tps), _mod_spec(7, tps), _mod_spec(8, tps),
                  _resident((1, ATTN_WIDTH)), _resident((D_MODEL, D_MODEL)), _resident((1, D_MODEL)),
                  _resident((D_MODEL, D_FF)), _resident((D_MODEL, D_FF)), _resident((D_FF, D_MODEL)),
                  _resident((1, D_MODEL))],
        out_specs=tok(D_MODEL),
        scratch_shapes=[pltpu.VMEM((tm, D_FF), BF16)],
        compiler_params=pltpu.CompilerParams(dimension_semantics=("arbitrary",),
                                             vmem_limit_bytes=VMEM_LIMIT_BYTES),
        name="out_ffn2",
    )(x1, oa, om, mod4, mod4, mod4, mod4, attn_norm, w_out, norm_w, wg, wu, wd, final_norm)


def kernel(x, c, w_ada, b_ada, ffn1_norm, ffn1_w_gate, ffn1_w_up, ffn1_w_down, mix_norm, w_in, conv_w, conv_b,
           w_q_m, w_k_m, w_v_m, w_if, b_if, mlstm_norm, mlstm_skip, attn_norm, w_out,
           ffn2_norm, ffn2_w_gate, ffn2_w_up, ffn2_w_down, final_norm):
    B, S, D = x.shape
    T = B * S
    assert w_ada.shape[0] == 1, "only depth 1 is supported"
    bf = lambda a: a.astype(BF16)
    xf = x.reshape(T, D)
    for l in range(1):
        mod4 = _adaln(c, w_ada[l], b_ada[l][None, :]).reshape(N_MOD, B, 1, D)
        x1 = _ffn1(xf, mod4, ffn1_norm[l][None, :], bf(ffn1_w_gate[l]), bf(ffn1_w_up[l]), bf(ffn1_w_down[l]), S)
        q, k, v, xm, z, kmean = _proj(x1, mod4, mix_norm[l][None, :], bf(w_in[l]), S)
        o_m = _mlstm(xm.reshape(B, S, MLSTM_WIDTH), z.reshape(B, S, MLSTM_WIDTH),
                     conv_w[l], conv_b[l][None, :],
                     bf(jnp.concatenate([w_q_m[l], w_k_m[l]], axis=-1)), bf(w_v_m[l]),
                     bf(w_if[l]), bf(w_if[l].T), b_if[l][None, :], b_if[l][:, None],
                     mlstm_norm[l][None, :], mlstm_skip[l][None, :])
        o_a = _moba(q.reshape(B, S, ATTN_WIDTH), k.reshape(B, S, ATTN_WIDTH), v.reshape(B, S, ATTN_WIDTH),
                    kmean.reshape(B, S // MOBA_BLOCK, ATTN_WIDTH))
        xf = _out_ffn2(x1, o_a.reshape(T, ATTN_WIDTH), o_m.reshape(T, MLSTM_WIDTH), mod4,
                       attn_norm[l][None, :], bf(w_out[l]), ffn2_norm[l][None, :],
                       bf(ffn2_w_gate[l]), bf(ffn2_w_up[l]), bf(ffn2_w_down[l]), final_norm[None, :], S)
    return xf.reshape(B, S, D)
```

```python
import functools

import jax
import jax.numpy as jnp
from jax import lax
from jax.experimental import pallas as pl
from jax.experimental.pallas import tpu as pltpu

F32 = jnp.float32
BF16 = jnp.bfloat16

D_MODEL = 1024
ATTN_HEADS = 8
ATTN_WIDTH = 512
ATTN_HEAD_DIM = 64
MLSTM_HEADS = 4
MLSTM_WIDTH = 512
MLSTM_HEAD_DIM = 128
IN_COLS = 3 * ATTN_WIDTH + 2 * MLSTM_WIDTH
CONV_WIDTH = 4
MOBA_BLOCK = 256
MOBA_TOPK = 3
D_FF = 2816
N_MOD = 9
EPS = 1e-6

LANES = 128
SUBLANES = 8
VMEM_LIMIT_BYTES = 56 * 1024 * 1024

FFN_TOKENS = 512
MXU_WIDTH = 256
FFN_CHUNKS = ((0, 6 * MXU_WIDTH), (6 * MXU_WIDTH, D_FF))
MLSTM_CHUNK = 256
MOBA_LOOKAHEAD = 2
MOBA_BOUND_SLACK = 1.0 + 2.0 ** -6
MOBA_MIN_DENOM = 1e-25
VT_ROWS = LANES + 16
MASK_VALUE = -1e30

_NT = (((1,), (1,)), ((), ()))
_TN = (((0,), (0,)), ((), ()))


def _sigmoid(x):
    return 1.0 / (1.0 + jnp.exp(-x))


def _rmsnorm(x, w):
    return x * lax.rsqrt(jnp.mean(x * x, axis=-1, keepdims=True) + EPS) * w


def _rms_mod(x, w, shift, scale):
    return _rmsnorm(x, w) * (1.0 + scale) + shift


def _swiglu(h_bf, wg_ref, wu_ref, wd_ref, a_ref):
    for lo, hi in FFN_CHUNKS:
        sl = slice(lo, hi)
        g = jnp.dot(h_bf, wg_ref[:, sl], preferred_element_type=F32)
        u = jnp.dot(h_bf, wu_ref[:, sl], preferred_element_type=F32)
        a_ref[:, sl] = (g * _sigmoid(g) * u).astype(BF16)
    return jnp.dot(a_ref[...], wd_ref[...], preferred_element_type=F32)


def _adaln_kernel(c_ref, w_ref, b_ref, o_ref):
    o_ref[0] = jnp.dot(c_ref[...].astype(BF16), w_ref[...].astype(BF16),
                       preferred_element_type=F32) + b_ref[...]


def _adaln(c, w_ada, b_ada):
    B = c.shape[0]
    return pl.pallas_call(
        _adaln_kernel,
        out_shape=jax.ShapeDtypeStruct((N_MOD, B, D_MODEL), F32),
        grid=(N_MOD,),
        in_specs=[
            pl.BlockSpec((B, D_MODEL), lambda j: (0, 0)),
            pl.BlockSpec((D_MODEL, D_MODEL), lambda j: (0, j)),
            pl.BlockSpec((1, D_MODEL), lambda j: (0, j)),
        ],
        out_specs=pl.BlockSpec((1, B, D_MODEL), lambda j: (j, 0, 0)),
        compiler_params=pltpu.CompilerParams(dimension_semantics=("arbitrary",),
                                             vmem_limit_bytes=VMEM_LIMIT_BYTES),
        name="adaln",
    )(c, w_ada, b_ada)


def _resident(shape):
    nd = len(shape)
    return pl.BlockSpec(shape, lambda *_: (0,) * nd, pipeline_mode=pl.Buffered(1))


def _mod_spec(k, tiles_per_seq):
    return pl.BlockSpec((1, 1, 1, D_MODEL), lambda i: (k, i // tiles_per_seq, 0, 0))


def _ffn1_kernel(x_ref, sh_ref, sc_ref, g_ref, nw_ref, wg_ref, wu_ref, wd_ref, o_ref, a_ref):
    x = x_ref[...]
    h = _rms_mod(x, nw_ref[...], sh_ref[0, 0], sc_ref[0, 0])
    y = _swiglu(h.astype(BF16), wg_ref, wu_ref, wd_ref, a_ref)
    o_ref[...] = x + (0.5 * (1.0 + g_ref[0, 0])) * y


def _ffn1(x2d, mod4, norm_w, wg, wu, wd, seq):
    T = x2d.shape[0]
    tm = FFN_TOKENS
    tps = seq // tm
    tok = pl.BlockSpec((tm, D_MODEL), lambda i: (i, 0))
    return pl.pallas_call(
        _ffn1_kernel,
        out_shape=jax.ShapeDtypeStruct((T, D_MODEL), F32),
        grid=(T // tm,),
        in_specs=[tok, _mod_spec(0, tps), _mod_spec(1, tps), _mod_spec(2, tps),
                  _resident((1, D_MODEL)),
                  _resident((D_MODEL, D_FF)), _resident((D_MODEL, D_FF)), _resident((D_FF, D_MODEL))],
        out_specs=tok,
        scratch_shapes=[pltpu.VMEM((tm, D_FF), BF16)],
        compiler_params=pltpu.CompilerParams(dimension_semantics=("arbitrary",),
                                             vmem_limit_bytes=VMEM_LIMIT_BYTES),
        name="ffn1",
    )(x2d, mod4, mod4, mod4, norm_w, wg, wu, wd)


def _proj_kernel(x_ref, sh_ref, sc_ref, nw_ref, win_ref,
                 q_ref, k_ref, v_ref, xm_ref, z_ref, km_ref):
    h = _rms_mod(x_ref[...], nw_ref[...], sh_ref[0, 0], sc_ref[0, 0]).astype(BF16)
    proj = jnp.dot(h, win_ref[...], preferred_element_type=F32)
    aw = ATTN_WIDTH
    q_ref[...] = (proj[:, 0:aw] * (ATTN_HEAD_DIM ** -0.5)).astype(BF16)
    k = proj[:, aw:2 * aw]
    k_ref[...] = k.astype(BF16)
    v_ref[...] = proj[:, 2 * aw:3 * aw].astype(BF16)
    xm_ref[...] = proj[:, 3 * aw:3 * aw + MLSTM_WIDTH]
    z_ref[...] = proj[:, 3 * aw + MLSTM_WIDTH:]
    for r in range(FFN_TOKENS // MOBA_BLOCK):
        km_ref[r] = jnp.mean(k[r * MOBA_BLOCK:(r + 1) * MOBA_BLOCK], axis=0, keepdims=True)


def _proj(x1, mod4, norm_w, w_in, seq):
    T = x1.shape[0]
    tm = FFN_TOKENS
    tps = seq // tm
    nb = tm // MOBA_BLOCK
    tok = lambda w: pl.BlockSpec((tm, w), lambda i: (i, 0))
    return pl.pallas_call(
        _proj_kernel,
        out_shape=(jax.ShapeDtypeStruct((T, ATTN_WIDTH), BF16),
                   jax.ShapeDtypeStruct((T, ATTN_WIDTH), BF16),
                   jax.ShapeDtypeStruct((T, ATTN_WIDTH), BF16),
                   jax.ShapeDtypeStruct((T, MLSTM_WIDTH), F32),
                   jax.ShapeDtypeStruct((T, MLSTM_WIDTH), F32),
                   jax.ShapeDtypeStruct((T // MOBA_BLOCK, 1, ATTN_WIDTH), F32)),
        grid=(T // tm,),
        in_specs=[tok(D_MODEL), _mod_spec(3, tps), _mod_spec(4, tps),
                  _resident((1, D_MODEL)), _resident((D_MODEL, IN_COLS))],
        out_specs=(tok(ATTN_WIDTH), tok(ATTN_WIDTH), tok(ATTN_WIDTH),
                   tok(MLSTM_WIDTH), tok(MLSTM_WIDTH),
                   pl.BlockSpec((nb, 1, ATTN_WIDTH), lambda i: (i, 0, 0))),
        compiler_params=pltpu.CompilerParams(dimension_semantics=("arbitrary",),
                                             vmem_limit_bytes=VMEM_LIMIT_BYTES),
        name="proj",
    )(x1, mod4, mod4, norm_w, w_in)


def _gate_transform(g, gate_idx):
    log_sig = jnp.minimum(g, 0.0) - jnp.log1p(jnp.exp(-jnp.abs(g)))
    return jnp.where(gate_idx >= MLSTM_HEADS, log_sig, g)


def _mlstm_kernel(xm_ref, z_ref, cw_ref, cb_ref, wqk_ref, wv_ref, wif_ref, wift_ref,
                  bifc_ref, bifr_ref, mn_ref, ms_ref, o_ref, ct_ref, m_ref, carry_ref):
    L = MLSTM_CHUNK
    dh = MLSTM_HEAD_DIM
    H = MLSTM_HEADS

    @pl.when(pl.program_id(1) == 0)
    def _():
        ct_ref[...] = jnp.zeros_like(ct_ref)
        m_ref[...] = jnp.zeros_like(m_ref)
        carry_ref[...] = jnp.zeros_like(carry_ref)

    xm = xm_ref[0]
    z = z_ref[0]

    carry = carry_ref[...]
    cw = cw_ref[...]
    acc = xm * cw[CONV_WIDTH - 1:CONV_WIDTH] + cb_ref[...]
    row8 = lax.broadcasted_iota(jnp.int32, (SUBLANES, MLSTM_WIDTH), 0)
    for k in range(1, CONV_WIDTH):
        rolled = pltpu.roll(xm, k, 0)
        top = jnp.where(row8 < k, pltpu.roll(carry, k, 0), rolled[0:SUBLANES])
        shifted = jnp.concatenate([top, rolled[SUBLANES:]], axis=0)
        acc = acc + shifted * cw[CONV_WIDTH - 1 - k:CONV_WIDTH - k]
    carry_ref[...] = xm[L - SUBLANES:L]
    xc = acc * _sigmoid(acc)

    xc_bf = xc.astype(BF16)
    xm_bf = xm.astype(BF16)
    qk, vv, cat = [], [], []
    for h in range(H):
        hs = slice(h * dh, (h + 1) * dh)
        qk_h = jnp.dot(xc_bf[:, hs], wqk_ref[h], preferred_element_type=F32)
        v_h = jnp.dot(xm_bf[:, hs], wv_ref[h], preferred_element_type=F32)
        qk.append(qk_h)
        vv.append(v_h)
        cat += [qk_h.astype(BF16), v_h.astype(BF16)]
    cat = jnp.concatenate(cat, axis=1)

    g_col = jnp.dot(cat, wif_ref[...], preferred_element_type=F32) + bifc_ref[...]
    g_row = lax.dot_general(wift_ref[...], cat, _NT, preferred_element_type=F32) + bifr_ref[...]
    a_col = _gate_transform(g_col, lax.broadcasted_iota(jnp.int32, (L, 2 * H), 1))
    a_row = _gate_transform(g_row, lax.broadcasted_iota(jnp.int32, (2 * H, L), 0))

    t_idx = lax.broadcasted_iota(jnp.int32, (L, L), 0)
    s_idx = lax.broadcasted_iota(jnp.int32, (L, L), 1)
    causal = s_idx <= t_idx
    tri = causal.astype(F32)
    tri_t = (t_idx <= s_idx).astype(F32)
    b_col = jnp.dot(tri, a_col, preferred_element_type=F32, precision=lax.Precision.HIGHEST)
    b_row = jnp.dot(a_row, tri_t, preferred_element_type=F32, precision=lax.Precision.HIGHEST)

    ones_blk = (lax.broadcasted_iota(jnp.int32, (L, LANES), 1) == 0).astype(F32)
    outs = []
    for h in range(H):
        hs = slice(h * dh, (h + 1) * dh)
        q_bf = qk[h][:, :dh].astype(BF16)
        k_bf = (qk[h][:, dh:] * (dh ** -0.5)).astype(BF16)
        v_aug = jnp.concatenate([vv[h], ones_blk], axis=1)
        b_c = b_col[:, H + h:H + h + 1]
        i_c = a_col[:, h:h + 1]
        b_r = b_row[H + h:H + h + 1, :]
        i_r = a_row[h:h + 1, :]
        m_prev = m_ref[h, 0:1, 0:1]

        log_d = jnp.where(causal, b_c - b_r + i_r, -jnp.inf)
        log_inter = b_c + m_prev
        m_t = jnp.maximum(log_inter, jnp.max(log_d, axis=1, keepdims=True))
        w_intra = jnp.exp(log_d - m_t)
        w_inter = jnp.exp(log_inter - m_t)
        s = lax.dot_general(q_bf, k_bf, _NT, preferred_element_type=F32) * w_intra
        ct = ct_ref[h]
        comb = (jnp.dot(s.astype(BF16), v_aug.astype(BF16), preferred_element_type=F32)
                + w_inter * jnp.dot(q_bf, ct.astype(BF16), preferred_element_type=F32))
        num = comb[:, :dh]
        den = comb[:, dh:dh + 1]
        hh = num / jnp.maximum(jnp.abs(den), jnp.exp(-m_t))

        b_last = b_c[L - 1:L, :]
        log_g = b_last - b_c + i_c
        m_new = jnp.maximum(b_last + m_prev, jnp.max(log_g, axis=0, keepdims=True))
        w_g = jnp.exp(log_g - m_new)
        decay = jnp.exp(b_last + m_prev - m_new)
        upd = lax.dot_general(k_bf, (v_aug * w_g).astype(BF16), _TN, preferred_element_type=F32)
        ct_ref[h] = decay * ct + upd
        m_ref[h] = jnp.broadcast_to(m_new, (SUBLANES, LANES))

        mu = jnp.mean(hh, axis=-1, keepdims=True)
        dlt = hh - mu
        var = jnp.mean(dlt * dlt, axis=-1, keepdims=True)
        hn = dlt * lax.rsqrt(var + EPS) * mn_ref[:, hs]
        zh = z[:, hs]
        outs.append(((hn + ms_ref[:, hs] * xc[:, hs]) * (zh * _sigmoid(zh))).astype(BF16))
    o_ref[0] = jnp.concatenate(outs, axis=1)


def _mlstm(xm, z, conv_w, conv_b, wqk, wv, wif, wif_t, bif_c, bif_r, mnorm, mskip):
    B, S, W = xm.shape
    L = MLSTM_CHUNK
    tok = pl.BlockSpec((1, L, W), lambda b, j: (b, j, 0))
    return pl.pallas_call(
        _mlstm_kernel,
        out_shape=jax.ShapeDtypeStruct((B, S, W), BF16),
        grid=(B, S // L),
        in_specs=[tok, tok,
                  _resident(conv_w.shape), _resident(conv_b.shape), _resident(wqk.shape), _resident(wv.shape),
                  _resident(wif.shape), _resident(wif_t.shape), _resident(bif_c.shape), _resident(bif_r.shape),
                  _resident(mnorm.shape), _resident(mskip.shape)],
        out_specs=tok,
        scratch_shapes=[pltpu.VMEM((MLSTM_HEADS, MLSTM_HEAD_DIM, 2 * MLSTM_HEAD_DIM), F32),
                        pltpu.VMEM((MLSTM_HEADS, SUBLANES, LANES), F32),
                        pltpu.VMEM((SUBLANES, W), F32)],
        compiler_params=pltpu.CompilerParams(dimension_semantics=("arbitrary", "arbitrary"),
                                             vmem_limit_bytes=VMEM_LIMIT_BYTES),
        name="mlstm",
    )(xm, z, conv_w, conv_b, wqk, wv, wif, wif_t, bif_c, bif_r, mnorm, mskip)


def _moba_kernel(q_ref, k_ref, v_ref, km_ref, o_ref, lmin_ref, kaug_ref, vt_ref, qaug_ref, kn_ref, *, bounded):
    BS = MOBA_BLOCK
    S = k_ref.shape[1]
    nb = km_ref.shape[1]
    hd = ATTN_HEAD_DIM
    i = pl.program_id(2)

    @pl.when(i == 0)
    def _():
        lane_s = lax.broadcasted_iota(jnp.int32, (S, LANES), 1)
        blk_s = lax.shift_right_logical(lax.broadcasted_iota(jnp.int32, (S, LANES), 0), BS.bit_length() - 1)
        kaug_ref[:, 0:LANES] = k_ref[0]
        kaug_ref[:, LANES:] = ((lane_s == blk_s) | (lane_s == nb)).astype(BF16)
        vt_ref[0:LANES, :] = jnp.transpose(v_ref[0].astype(F32)).astype(BF16)
        vt_ref[LANES:, :] = (lax.broadcasted_iota(jnp.int32, (VT_ROWS - LANES, S), 0) == 0).astype(BF16)
        if bounded:
            kf = k_ref[0].astype(F32)
            head_of_dim = lax.shift_right_logical(lax.broadcasted_iota(jnp.int32, (LANES, LANES), 0),
                                                  hd.bit_length() - 1)
            ind = (head_of_dim == lax.broadcasted_iota(jnp.int32, (LANES, LANES), 1)).astype(BF16)
            ksq = jnp.dot((kf * kf).astype(BF16), ind, preferred_element_type=F32)
            kn_ref[...] = jnp.broadcast_to(jnp.max(ksq, axis=0, keepdims=True), kn_ref.shape)

    q_t = jnp.transpose(q_ref[0].astype(F32))
    km = km_ref[0]
    km_hi = km.astype(BF16)
    km_lo = (km - km_hi.astype(F32)).astype(BF16)
    dim_t = lax.broadcasted_iota(jnp.int32, (LANES, BS), 0)
    blk_t = lax.broadcasted_iota(jnp.int32, (nb, BS), 0)
    pad_rows = lax.broadcasted_iota(jnp.int32, (LANES - nb, BS), 0)

    for hh in range(2):
        in_head = (dim_t >= hh * hd) & (dim_t < (hh + 1) * hd)
        qh_t = jnp.where(in_head, q_t, 0.0).astype(BF16)
        gate = (jnp.dot(km_hi, qh_t, preferred_element_type=F32)
                + jnp.dot(km_lo, qh_t, preferred_element_type=F32))
        cur = jnp.where(blk_t < i, gate, -jnp.inf)
        bias_t = jnp.full((nb, BS), MASK_VALUE, F32)
        for _ in range(MOBA_TOPK):
            mx = jnp.max(cur, axis=0, keepdims=True)
            first = jnp.min(jnp.where(cur == mx, blk_t, nb), axis=0, keepdims=True)
            pick = (blk_t == first) & (mx > -jnp.inf)
            bias_t = jnp.where(pick, 0.0, bias_t)
            cur = jnp.where(pick, -jnp.inf, cur)
        bias_t = jnp.where(blk_t == i, 0.0, bias_t)
        if bounded:
            qsq = jnp.sum(jnp.where(in_head, q_t * q_t, 0.0), axis=0, keepdims=True)
            bound = jnp.sqrt(qsq * kn_ref[0:1, hh:hh + 1]) * MOBA_BOUND_SLACK
            tail = jnp.where(pad_rows == 0, -bound, 0.0)
        else:
            tail = jnp.zeros((LANES - nb, BS), F32)
        qaug_ref[:, hh * BS:(hh + 1) * BS] = jnp.concatenate(
            [qh_t, bias_t.astype(BF16), tail.astype(BF16)], axis=0)

    future_key = (lax.broadcasted_iota(jnp.int32, (BS, 2 * BS), 0)
                  > (lax.broadcasted_iota(jnp.int32, (BS, 2 * BS), 1) & (BS - 1)))

    def scores(j):
        return jnp.dot(kaug_ref[j * BS:(j + 1) * BS, :], qaug_ref[...], preferred_element_type=F32)

    for c in range(nb):
        @pl.when(i == c)
        def _():
            m = pv = None
            order = list(range(c, -1, -1))
            pending = [scores(j) for j in order[:MOBA_LOOKAHEAD]]
            for idx, j in enumerate(order):
                s = pending.pop(0)
                if idx + MOBA_LOOKAHEAD < len(order):
                    pending.append(scores(order[idx + MOBA_LOOKAHEAD]))
                if j == c:
                    s = jnp.where(future_key, MASK_VALUE, s)
                v_j = vt_ref[:, j * BS:(j + 1) * BS]
                if bounded:
                    d = jnp.dot(v_j, jnp.exp(s).astype(BF16), preferred_element_type=F32)
                    pv = d if pv is None else pv + d
                    continue
                cm = jnp.max(s, axis=0, keepdims=True)
                if m is None:
                    m = cm
                    pv = jnp.dot(v_j, jnp.exp(s - m).astype(BF16), preferred_element_type=F32)
                else:
                    m_new = jnp.maximum(m, cm)
                    pv = (jnp.exp(m - m_new) * pv
                          + jnp.dot(v_j, jnp.exp(s - m_new).astype(BF16), preferred_element_type=F32))
                    m = m_new
            denom = pv[LANES:LANES + 1]
            out_t = jnp.concatenate([pv[0:hd, 0:BS] / denom[:, 0:BS],
                                     pv[hd:LANES, BS:] / denom[:, BS:]], axis=0)
            o_ref[0] = jnp.transpose(out_t)
            lmin_ref[0] = jnp.broadcast_to(jnp.min(denom, axis=1, keepdims=True), lmin_ref.shape[1:])


def _moba_call(q, k, v, kmean, bounded):
    B, S, W = q.shape
    BS = MOBA_BLOCK
    nb = S // BS
    npairs = W // LANES
    assert nb < LANES - 1
    qspec = pl.BlockSpec((1, BS, LANES), lambda b, p, i: (b, i, p))
    kvspec = pl.BlockSpec((1, S, LANES), lambda b, p, i: (b, 0, p))
    return pl.pallas_call(
        functools.partial(_moba_kernel, bounded=bounded),
        out_shape=(jax.ShapeDtypeStruct((B, S, W), F32),
                   jax.ShapeDtypeStruct((B * npairs * nb, SUBLANES, LANES), F32)),
        grid=(B, npairs, nb),
        in_specs=[qspec, kvspec, kvspec, pl.BlockSpec((1, nb, LANES), lambda b, p, i: (b, 0, p))],
        out_specs=(qspec, pl.BlockSpec((1, SUBLANES, LANES), lambda b, p, i: ((b * npairs + p) * nb + i, 0, 0))),
        scratch_shapes=[pltpu.VMEM((S, 2 * LANES), BF16),
                        pltpu.VMEM((VT_ROWS, S), BF16),
                        pltpu.VMEM((2 * LANES, 2 * BS), BF16),
                        pltpu.VMEM((SUBLANES, LANES), F32)],
        compiler_params=pltpu.CompilerParams(dimension_semantics=("arbitrary",) * 3,
                                             vmem_limit_bytes=VMEM_LIMIT_BYTES),
        name="moba_bounded" if bounded else "moba_exact",
    )(q, k, v, kmean)


def _moba(q, k, v, kmean):
    o_fast, lmin = _moba_call(q, k, v, kmean, True)
    ok = jnp.all(lmin[:, 0, 0] > MOBA_MIN_DENOM)
    return lax.cond(ok, lambda: o_fast, lambda: _moba_call(q, k, v, kmean, False)[0])


def _out_ffn2_kernel(x1_ref, oa_ref, om_ref, g2_ref, sh_ref, sc_ref, g3_ref, an_ref, wo_ref,
                     nw_ref, wg_ref, wu_ref, wd_ref, fn_ref, o_ref, a_ref):
    oa = _rmsnorm(oa_ref[...], an_ref[...]).astype(BF16)
    mix = (jnp.dot(oa, wo_ref[0:ATTN_WIDTH, :], preferred_element_type=F32)
           + jnp.dot(om_ref[...], wo_ref[ATTN_WIDTH:, :], preferred_element_type=F32))
    x2 = x1_ref[...] + (1.0 + g2_ref[0, 0]) * mix
    h = _rms_mod(x2, nw_ref[...], sh_ref[0, 0], sc_ref[0, 0])
    y = _swiglu(h.astype(BF16), wg_ref, wu_ref, wd_ref, a_ref)
    x3 = x2 + (0.5 * (1.0 + g3_ref[0, 0])) * y
    o_ref[...] = _rmsnorm(x3, fn_ref[...])


def _out_ffn2(x1, oa, om, mod4, attn_norm, w_out, norm_w, wg, wu, wd, final_norm, seq):
    T = x1.shape[0]
    tm = FFN_TOKENS
    tps = seq // tm
    tok = lambda w: pl.BlockSpec((tm, w), lambda i: (i, 0))
    return pl.pallas_call(
        _out_ffn2_kernel,
        out_shape=jax.ShapeDtypeStruct((T, D_MODEL), F32),
        grid=(T // tm,),
        in_specs=[tok(D_MODEL), tok(ATTN_WIDTH), tok(MLSTM_WIDTH),
                  _mod_spec(5, tps), _mod_spec(6, tps), _mod_spec(7, tps), _mod_spec(8, tps),
                  _resident((1, ATTN_WIDTH)), _resident((D_MODEL, D_MODEL)), _resident((1, D_MODEL)),
                  _resident((D_MODEL, D_FF)), _resident((D_MODEL, D_FF)), _resident((D_FF, D_MODEL)),
                  _resident((1, D_MODEL))],
        out_specs=tok(D_MODEL),
        scratch_shapes=[pltpu.VMEM((tm, D_FF), BF16)],
        compiler_params=pltpu.CompilerParams(dimension_semantics=("arbitrary",),
                                             vmem_limit_bytes=VMEM_LIMIT_BYTES),
        name="out_ffn2",
    )(x1, oa, om, mod4, mod4, mod4, mod4, attn_norm, w_out, norm_w, wg, wu, wd, final_norm)


def kernel(x, c, w_ada, b_ada, ffn1_norm, ffn1_w_gate, ffn1_w_up, ffn1_w_down, mix_norm, w_in, conv_w, conv_b,
           w_q_m, w_k_m, w_v_m, w_if, b_if, mlstm_norm, mlstm_skip, attn_norm, w_out,
           ffn2_norm, ffn2_w_gate, ffn2_w_up, ffn2_w_down, final_norm):
    B, S, D = x.shape
    T = B * S
    assert w_ada.shape[0] == 1, "only depth 1 is supported"
    bf = lambda a: a.astype(BF16)
    xf = x.reshape(T, D)
    for l in range(1):
        mod4 = _adaln(c, w_ada[l], b_ada[l][None, :]).reshape(N_MOD, B, 1, D)
        x1 = _ffn1(xf, mod4, ffn1_norm[l][None, :], bf(ffn1_w_gate[l]), bf(ffn1_w_up[l]), bf(ffn1_w_down[l]), S)
        q, k, v, xm, z, kmean = _proj(x1, mod4, mix_norm[l][None, :], bf(w_in[l]), S)
        o_m = _mlstm(xm.reshape(B, S, MLSTM_WIDTH), z.reshape(B, S, MLSTM_WIDTH),
                     conv_w[l], conv_b[l][None, :],
                     bf(jnp.concatenate([w_q_m[l], w_k_m[l]], axis=-1)), bf(w_v_m[l]),
                     bf(w_if[l]), bf(w_if[l].T), b_if[l][None, :], b_if[l][:, None],
                     mlstm_norm[l][None, :], mlstm_skip[l][None, :])
        o_a = _moba(q.reshape(B, S, ATTN_WIDTH), k.reshape(B, S, ATTN_WIDTH), v.reshape(B, S, ATTN_WIDTH),
                    kmean.reshape(B, S // MOBA_BLOCK, ATTN_WIDTH))
        xf = _out_ffn2(x1, o_a.reshape(T, ATTN_WIDTH), o_m.reshape(T, MLSTM_WIDTH), mod4,
                       attn_norm[l][None, :], bf(w_out[l]), ffn2_norm[l][None, :],
                       bf(ffn2_w_gate[l]), bf(ffn2_w_up[l]), bf(ffn2_w_down[l]), final_norm[None, :], S)
    return xf.reshape(B, S, D)
```

```python
import functools

import jax
import jax.numpy as jnp
from jax import lax
from jax.experimental import pallas as pl
from jax.experimental.pallas import tpu as pltpu

F32 = jnp.float32
BF16 = jnp.bfloat16

D_MODEL = 1024
ATTN_HEADS = 8
ATTN_WIDTH = 512
ATTN_HEAD_DIM = 64
MLSTM_HEADS = 4
MLSTM_WIDTH = 512
MLSTM_HEAD_DIM = 128
IN_COLS = 3 * ATTN_WIDTH + 2 * MLSTM_WIDTH
CONV_WIDTH = 4
MOBA_BLOCK = 256
MOBA_TOPK = 3
D_FF = 2816
N_MOD = 9
EPS = 1e-6

LANES = 128
SUBLANES = 8
VMEM_LIMIT_BYTES = 56 * 1024 * 1024

FFN_TOKENS = 512
MXU_WIDTH = 256
FFN_CHUNKS = ((0, 6 * MXU_WIDTH), (6 * MXU_WIDTH, D_FF))
MLSTM_CHUNK = 256
MOBA_PAIRS = 2
MOBA_LOOKAHEAD = 2
MOBA_BOUND_SLACK = 1.0 + 2.0 ** -6
MOBA_MIN_DENOM = 1e-25
VT_ROWS = LANES + 16
MASK_VALUE = -1e30

_NT = (((1,), (1,)), ((), ()))
_TN = (((0,), (0,)), ((), ()))


def _sigmoid(x):
    return 1.0 / (1.0 + jnp.exp(-x))


def _rmsnorm(x, w):
    return x * lax.rsqrt(jnp.mean(x * x, axis=-1, keepdims=True) + EPS) * w


def _rms_mod(x, w, shift, scale):
    return _rmsnorm(x, w) * (1.0 + scale) + shift


def _swiglu(h_bf, wg_ref, wu_ref, wd_ref, a_ref):
    for lo, hi in FFN_CHUNKS:
        sl = slice(lo, hi)
        g = jnp.dot(h_bf, wg_ref[:, sl], preferred_element_type=F32)
        u = jnp.dot(h_bf, wu_ref[:, sl], preferred_element_type=F32)
        a_ref[:, sl] = (g * _sigmoid(g) * u).astype(BF16)
    return jnp.dot(a_ref[...], wd_ref[...], preferred_element_type=F32)


def _adaln_kernel(c_ref, w_ref, b_ref, o_ref):
    o_ref[0] = jnp.dot(c_ref[...].astype(BF16), w_ref[...].astype(BF16),
                       preferred_element_type=F32) + b_ref[...]


def _adaln(c, w_ada, b_ada):
    B = c.shape[0]
    return pl.pallas_call(
        _adaln_kernel,
        out_shape=jax.ShapeDtypeStruct((N_MOD, B, D_MODEL), F32),
        grid=(N_MOD,),
        in_specs=[
            pl.BlockSpec((B, D_MODEL), lambda j: (0, 0)),
            pl.BlockSpec((D_MODEL, D_MODEL), lambda j: (0, j)),
            pl.BlockSpec((1, D_MODEL), lambda j: (0, j)),
        ],
        out_specs=pl.BlockSpec((1, B, D_MODEL), lambda j: (j, 0, 0)),
        compiler_params=pltpu.CompilerParams(dimension_semantics=("arbitrary",),
                                             vmem_limit_bytes=VMEM_LIMIT_BYTES),
        name="adaln",
    )(c, w_ada, b_ada)


def _resident(shape):
    nd = len(shape)
    return pl.BlockSpec(shape, lambda *_: (0,) * nd, pipeline_mode=pl.Buffered(1))


def _mod_spec(k, tiles_per_seq):
    return pl.BlockSpec((1, 1, 1, D_MODEL), lambda i: (k, i // tiles_per_seq, 0, 0))


def _ffn1_kernel(x_ref, sh_ref, sc_ref, g_ref, nw_ref, wg_ref, wu_ref, wd_ref, o_ref, a_ref):
    x = x_ref[...]
    h = _rms_mod(x, nw_ref[...], sh_ref[0, 0], sc_ref[0, 0])
    y = _swiglu(h.astype(BF16), wg_ref, wu_ref, wd_ref, a_ref)
    o_ref[...] = x + (0.5 * (1.0 + g_ref[0, 0])) * y


def _ffn1(x2d, mod4, norm_w, wg, wu, wd, seq):
    T = x2d.shape[0]
    tm = FFN_TOKENS
    tps = seq // tm
    tok = pl.BlockSpec((tm, D_MODEL), lambda i: (i, 0))
    return pl.pallas_call(
        _ffn1_kernel,
        out_shape=jax.ShapeDtypeStruct((T, D_MODEL), F32),
        grid=(T // tm,),
        in_specs=[tok, _mod_spec(0, tps), _mod_spec(1, tps), _mod_spec(2, tps),
                  _resident((1, D_MODEL)),
                  _resident((D_MODEL, D_FF)), _resident((D_MODEL, D_FF)), _resident((D_FF, D_MODEL))],
        out_specs=tok,
        scratch_shapes=[pltpu.VMEM((tm, D_FF), BF16)],
        compiler_params=pltpu.CompilerParams(dimension_semantics=("arbitrary",),
                                             vmem_limit_bytes=VMEM_LIMIT_BYTES),
        name="ffn1",
    )(x2d, mod4, mod4, mod4, norm_w, wg, wu, wd)


def _proj_kernel(x_ref, sh_ref, sc_ref, nw_ref, win_ref,
                 q_ref, k_ref, v_ref, xm_ref, z_ref, km_ref):
    h = _rms_mod(x_ref[...], nw_ref[...], sh_ref[0, 0], sc_ref[0, 0]).astype(BF16)
    proj = jnp.dot(h, win_ref[...], preferred_element_type=F32)
    aw = ATTN_WIDTH
    q_ref[...] = (proj[:, 0:aw] * (ATTN_HEAD_DIM ** -0.5)).astype(BF16)
    k = proj[:, aw:2 * aw]
    k_ref[...] = k.astype(BF16)
    v_ref[...] = proj[:, 2 * aw:3 * aw].astype(BF16)
    xm_ref[...] = proj[:, 3 * aw:3 * aw + MLSTM_WIDTH]
    z_ref[...] = proj[:, 3 * aw + MLSTM_WIDTH:]
    for r in range(FFN_TOKENS // MOBA_BLOCK):
        km_ref[r] = jnp.mean(k[r * MOBA_BLOCK:(r + 1) * MOBA_BLOCK], axis=0, keepdims=True)


def _proj(x1, mod4, norm_w, w_in, seq):
    T = x1.shape[0]
    tm = FFN_TOKENS
    tps = seq // tm
    nb = tm // MOBA_BLOCK
    tok = lambda w: pl.BlockSpec((tm, w), lambda i: (i, 0))
    return pl.pallas_call(
        _proj_kernel,
        out_shape=(jax.ShapeDtypeStruct((T, ATTN_WIDTH), BF16),
                   jax.ShapeDtypeStruct((T, ATTN_WIDTH), BF16),
                   jax.ShapeDtypeStruct((T, ATTN_WIDTH), BF16),
                   jax.ShapeDtypeStruct((T, MLSTM_WIDTH), F32),
                   jax.ShapeDtypeStruct((T, MLSTM_WIDTH), F32),
                   jax.ShapeDtypeStruct((T // MOBA_BLOCK, 1, ATTN_WIDTH), F32)),
        grid=(T // tm,),
        in_specs=[tok(D_MODEL), _mod_spec(3, tps), _mod_spec(4, tps),
                  _resident((1, D_MODEL)), _resident((D_MODEL, IN_COLS))],
        out_specs=(tok(ATTN_WIDTH), tok(ATTN_WIDTH), tok(ATTN_WIDTH),
                   tok(MLSTM_WIDTH), tok(MLSTM_WIDTH),
                   pl.BlockSpec((nb, 1, ATTN_WIDTH), lambda i: (i, 0, 0))),
        compiler_params=pltpu.CompilerParams(dimension_semantics=("arbitrary",),
                                             vmem_limit_bytes=VMEM_LIMIT_BYTES),
        name="proj",
    )(x1, mod4, mod4, norm_w, w_in)


def _gate_transform(g, gate_idx):
    log_sig = jnp.minimum(g, 0.0) - jnp.log1p(jnp.exp(-jnp.abs(g)))
    return jnp.where(gate_idx >= MLSTM_HEADS, log_sig, g)


def _mlstm_kernel(xm_ref, z_ref, cw_ref, cb_ref, wqk_ref, wv_ref, wif_ref, wift_ref,
                  bifc_ref, bifr_ref, mn_ref, ms_ref, o_ref, ct_ref, m_ref, carry_ref):
    L = MLSTM_CHUNK
    dh = MLSTM_HEAD_DIM
    H = MLSTM_HEADS

    @pl.when(pl.program_id(1) == 0)
    def _():
        ct_ref[...] = jnp.zeros_like(ct_ref)
        m_ref[...] = jnp.zeros_like(m_ref)
        carry_ref[...] = jnp.zeros_like(carry_ref)

    xm = xm_ref[0]
    z = z_ref[0]

    carry = carry_ref[...]
    cw = cw_ref[...]
    acc = xm * cw[CONV_WIDTH - 1:CONV_WIDTH] + cb_ref[...]
    row8 = lax.broadcasted_iota(jnp.int32, (SUBLANES, MLSTM_WIDTH), 0)
    for k in range(1, CONV_WIDTH):
        rolled = pltpu.roll(xm, k, 0)
        top = jnp.where(row8 < k, pltpu.roll(carry, k, 0), rolled[0:SUBLANES])
        shifted = jnp.concatenate([top, rolled[SUBLANES:]], axis=0)
        acc = acc + shifted * cw[CONV_WIDTH - 1 - k:CONV_WIDTH - k]
    carry_ref[...] = xm[L - SUBLANES:L]
    xc = acc * _sigmoid(acc)

    xc_bf = xc.astype(BF16)
    xm_bf = xm.astype(BF16)
    qk, vv, cat = [], [], []
    for h in range(H):
        hs = slice(h * dh, (h + 1) * dh)
        qk_h = jnp.dot(xc_bf[:, hs], wqk_ref[h], preferred_element_type=F32)
        v_h = jnp.dot(xm_bf[:, hs], wv_ref[h], preferred_element_type=F32)
        qk.append(qk_h)
        vv.append(v_h)
        cat += [qk_h.astype(BF16), v_h.astype(BF16)]
    cat = jnp.concatenate(cat, axis=1)

    g_col = jnp.dot(cat, wif_ref[...], preferred_element_type=F32) + bifc_ref[...]
    g_row = lax.dot_general(wift_ref[...], cat, _NT, preferred_element_type=F32) + bifr_ref[...]
    a_col = _gate_transform(g_col, lax.broadcasted_iota(jnp.int32, (L, 2 * H), 1))
    a_row = _gate_transform(g_row, lax.broadcasted_iota(jnp.int32, (2 * H, L), 0))

    t_idx = lax.broadcasted_iota(jnp.int32, (L, L), 0)
    s_idx = lax.broadcasted_iota(jnp.int32, (L, L), 1)
    causal = s_idx <= t_idx
    tri = causal.astype(F32)
    tri_t = (t_idx <= s_idx).astype(F32)
    b_col = jnp.dot(tri, a_col, preferred_element_type=F32, precision=lax.Precision.HIGHEST)
    b_row = jnp.dot(a_row, tri_t, preferred_element_type=F32, precision=lax.Precision.HIGHEST)

    ones_blk = (lax.broadcasted_iota(jnp.int32, (L, LANES), 1) == 0).astype(F32)
    outs = []
    for h in range(H):
        hs = slice(h * dh, (h + 1) * dh)
        q_bf = qk[h][:, :dh].astype(BF16)
        k_bf = (qk[h][:, dh:] * (dh ** -0.5)).astype(BF16)
        v_aug = jnp.concatenate([vv[h], ones_blk], axis=1)
        b_c = b_col[:, H + h:H + h + 1]
        i_c = a_col[:, h:h + 1]
        b_r = b_row[H + h:H + h + 1, :]
        i_r = a_row[h:h + 1, :]
        m_prev = m_ref[h, 0:1, 0:1]

        log_d = jnp.where(causal, b_c - b_r + i_r, -jnp.inf)
        log_inter = b_c + m_prev
        m_t = jnp.maximum(log_inter, jnp.max(log_d, axis=1, keepdims=True))
        w_intra = jnp.exp(log_d - m_t)
        w_inter = jnp.exp(log_inter - m_t)
        s = lax.dot_general(q_bf, k_bf, _NT, preferred_element_type=F32) * w_intra
        ct = ct_ref[h]
        comb = (jnp.dot(s.astype(BF16), v_aug.astype(BF16), preferred_element_type=F32)
                + w_inter * jnp.dot(q_bf, ct.astype(BF16), preferred_element_type=F32))
        num = comb[:, :dh]
        den = comb[:, dh:dh + 1]
        hh = num / jnp.maximum(jnp.abs(den), jnp.exp(-m_t))

        b_last = b_c[L - 1:L, :]
        log_g = b_last - b_c + i_c
        m_new = jnp.maximum(b_last + m_prev, jnp.max(log_g, axis=0, keepdims=True))
        w_g = jnp.exp(log_g - m_new)
        decay = jnp.exp(b_last + m_prev - m_new)
        upd = lax.dot_general(k_bf, (v_aug * w_g).astype(BF16), _TN, preferred_element_type=F32)
        ct_ref[h] = decay * ct + upd
        m_ref[h] = jnp.broadcast_to(m_new, (SUBLANES, LANES))

        mu = jnp.mean(hh, axis=-1, keepdims=True)
        dlt = hh - mu
        var = jnp.mean(dlt * dlt, axis=-1, keepdims=True)
        hn = dlt * lax.rsqrt(var + EPS) * mn_ref[:, hs]
        zh = z[:, hs]
        outs.append(((hn + ms_ref[:, hs] * xc[:, hs]) * (zh * _sigmoid(zh))).astype(BF16))
    o_ref[0] = jnp.concatenate(outs, axis=1)


def _mlstm(xm, z, conv_w, conv_b, wqk, wv, wif, wif_t, bif_c, bif_r, mnorm, mskip):
    B, S, W = xm.shape
    L = MLSTM_CHUNK
    tok = pl.BlockSpec((1, L, W), lambda b, j: (b, j, 0))
    return pl.pallas_call(
        _mlstm_kernel,
        out_shape=jax.ShapeDtypeStruct((B, S, W), BF16),
        grid=(B, S // L),
        in_specs=[tok, tok,
                  _resident(conv_w.shape), _resident(conv_b.shape), _resident(wqk.shape), _resident(wv.shape),
                  _resident(wif.shape), _resident(wif_t.shape), _resident(bif_c.shape), _resident(bif_r.shape),
                  _resident(mnorm.shape), _resident(mskip.shape)],
        out_specs=tok,
        scratch_shapes=[pltpu.VMEM((MLSTM_HEADS, MLSTM_HEAD_DIM, 2 * MLSTM_HEAD_DIM), F32),
                        pltpu.VMEM((MLSTM_HEADS, SUBLANES, LANES), F32),
                        pltpu.VMEM((SUBLANES, W), F32)],
        compiler_params=pltpu.CompilerParams(dimension_semantics=("arbitrary", "arbitrary"),
                                             vmem_limit_bytes=VMEM_LIMIT_BYTES),
        name="mlstm",
    )(xm, z, conv_w, conv_b, wqk, wv, wif, wif_t, bif_c, bif_r, mnorm, mskip)


def _moba_kernel(q_ref, k_ref, v_ref, km_ref, o_ref, lmin_ref, kaug_ref, vt_ref, qaug_ref, kn_ref, *, bounded):
    BS = MOBA_BLOCK
    NP = MOBA_PAIRS
    S = k_ref.shape[1]
    nb = km_ref.shape[1]
    hd = ATTN_HEAD_DIM
    i = pl.program_id(2)

    @pl.when(i == 0)
    def _():
        lane_s = lax.broadcasted_iota(jnp.int32, (S, LANES), 1)
        blk_s = lax.shift_right_logical(lax.broadcasted_iota(jnp.int32, (S, LANES), 0), BS.bit_length() - 1)
        k_tail = ((lane_s == blk_s) | (lane_s == nb)).astype(BF16)
        v_tail = (lax.broadcasted_iota(jnp.int32, (VT_ROWS - LANES, S), 0) == 0).astype(BF16)
        for p in range(NP):
            ps = slice(p * LANES, (p + 1) * LANES)
            kaug_ref[p, :, 0:LANES] = k_ref[0, :, ps]
            kaug_ref[p, :, LANES:] = k_tail
            vt_ref[p, 0:LANES, :] = jnp.transpose(v_ref[0, :, ps].astype(F32)).astype(BF16)
            vt_ref[p, LANES:, :] = v_tail
            if bounded:
                kf = k_ref[0, :, ps].astype(F32)
                head_of_dim = lax.shift_right_logical(lax.broadcasted_iota(jnp.int32, (LANES, LANES), 0),
                                                      hd.bit_length() - 1)
                ind = (head_of_dim == lax.broadcasted_iota(jnp.int32, (LANES, LANES), 1)).astype(BF16)
                ksq = jnp.dot((kf * kf).astype(BF16), ind, preferred_element_type=F32)
                kn_ref[p] = jnp.broadcast_to(jnp.max(ksq, axis=0, keepdims=True), kn_ref.shape[1:])

    q_t = jnp.transpose(q_ref[0].astype(F32))
    dim_t = lax.broadcasted_iota(jnp.int32, (LANES, 2 * BS), 0)
    col_t = lax.broadcasted_iota(jnp.int32, (LANES, 2 * BS), 1)
    in_head = lax.shift_right_logical(dim_t, hd.bit_length() - 1) == lax.shift_right_logical(col_t, BS.bit_length() - 1)
    qq, gates = [], []
    for p in range(NP):
        q_p = q_t[p * LANES:(p + 1) * LANES]
        qq_p = jnp.where(in_head, jnp.concatenate([q_p, q_p], axis=1), 0.0)
        km = km_ref[0, :, p * LANES:(p + 1) * LANES]
        km_hi = km.astype(BF16)
        km_lo = (km - km_hi.astype(F32)).astype(BF16)
        qq_bf = qq_p.astype(BF16)
        gates.append(jnp.dot(km_hi, qq_bf, preferred_element_type=F32)
                     + jnp.dot(km_lo, qq_bf, preferred_element_type=F32))
        qq.append(qq_p)

    ncol = NP * 2 * BS
    blk_t = lax.broadcasted_iota(jnp.int32, (nb, ncol), 0)
    cur = jnp.where(blk_t < i, jnp.concatenate(gates, axis=1), -jnp.inf)
    bias_t = jnp.full((nb, ncol), MASK_VALUE, F32)
    for _ in range(MOBA_TOPK):
        mx = jnp.max(cur, axis=0, keepdims=True)
        first = jnp.min(jnp.where(cur == mx, blk_t, nb), axis=0, keepdims=True)
        pick = (blk_t == first) & (mx > -jnp.inf)
        bias_t = jnp.where(pick, 0.0, bias_t)
        cur = jnp.where(pick, -jnp.inf, cur)
    bias_t = jnp.where(blk_t == i, 0.0, bias_t).astype(BF16)

    pad_rows = lax.broadcasted_iota(jnp.int32, (LANES - nb, 2 * BS), 0)
    pad_cols = lax.broadcasted_iota(jnp.int32, (LANES - nb, 2 * BS), 1)
    for p in range(NP):
        if bounded:
            qsq = jnp.sum(qq[p] * qq[p], axis=0, keepdims=True)
            ksq_max = jnp.where(pad_cols[0:1] < BS, kn_ref[p, 0:1, 0:1], kn_ref[p, 0:1, 1:2])
            bound = jnp.sqrt(qsq * ksq_max) * MOBA_BOUND_SLACK
            tail = jnp.where(pad_rows == 0, -bound, 0.0)
        else:
            tail = jnp.zeros((LANES - nb, 2 * BS), F32)
        qaug_ref[p] = jnp.concatenate(
            [qq[p].astype(BF16), bias_t[:, p * 2 * BS:(p + 1) * 2 * BS], tail.astype(BF16)], axis=0)

    future_key = (lax.broadcasted_iota(jnp.int32, (BS, 2 * BS), 0)
                  > (lax.broadcasted_iota(jnp.int32, (BS, 2 * BS), 1) & (BS - 1)))

    def scores(j):
        return [jnp.dot(kaug_ref[p, j * BS:(j + 1) * BS, :], qaug_ref[p], preferred_element_type=F32)
                for p in range(NP)]

    for c in range(nb):
        @pl.when(i == c)
        def _():
            m = [None] * NP
            pv = [None] * NP
            order = list(range(c, -1, -1))
            pending = [scores(j) for j in order[:MOBA_LOOKAHEAD]]
            for idx, j in enumerate(order):
                s_all = pending.pop(0)
                if idx + MOBA_LOOKAHEAD < len(order):
                    pending.append(scores(order[idx + MOBA_LOOKAHEAD]))
                for p in range(NP):
                    s = s_all[p]
                    if j == c:
                        s = jnp.where(future_key, MASK_VALUE, s)
                    v_j = vt_ref[p, :, j * BS:(j + 1) * BS]
                    if bounded:
                        d = jnp.dot(v_j, jnp.exp(s).astype(BF16), preferred_element_type=F32)
                        pv[p] = d if pv[p] is None else pv[p] + d
                        continue
                    cm = jnp.max(s, axis=0, keepdims=True)
                    if m[p] is None:
                        m[p] = cm
                        pv[p] = jnp.dot(v_j, jnp.exp(s - cm).astype(BF16), preferred_element_type=F32)
                    else:
                        m_new = jnp.maximum(m[p], cm)
                        pv[p] = (jnp.exp(m[p] - m_new) * pv[p]
                                 + jnp.dot(v_j, jnp.exp(s - m_new).astype(BF16), preferred_element_type=F32))
                        m[p] = m_new
            outs, lmin = [], None
            for p in range(NP):
                denom = pv[p][LANES:LANES + 1]
                outs += [pv[p][0:hd, 0:BS] / denom[:, 0:BS], pv[p][hd:LANES, BS:] / denom[:, BS:]]
                dmin = jnp.min(denom, axis=1, keepdims=True)
                lmin = dmin if lmin is None else jnp.minimum(lmin, dmin)
            o_ref[0] = jnp.transpose(jnp.concatenate(outs, axis=0))
            lmin_ref[0] = jnp.broadcast_to(lmin, lmin_ref.shape[1:])


def _moba_call(q, k, v, kmean, bounded):
    B, S, W = q.shape
    BS = MOBA_BLOCK
    NP = MOBA_PAIRS
    nb = S // BS
    ngroups = W // (NP * LANES)
    assert nb < LANES - 1 and W % (NP * LANES) == 0
    qspec = pl.BlockSpec((1, BS, NP * LANES), lambda b, g, i: (b, i, g))
    kvspec = pl.BlockSpec((1, S, NP * LANES), lambda b, g, i: (b, 0, g))
    return pl.pallas_call(
        functools.partial(_moba_kernel, bounded=bounded),
        out_shape=(jax.ShapeDtypeStruct((B, S, W), F32),
                   jax.ShapeDtypeStruct((B * ngroups * nb, SUBLANES, LANES), F32)),
        grid=(B, ngroups, nb),
        in_specs=[qspec, kvspec, kvspec, pl.BlockSpec((1, nb, NP * LANES), lambda b, g, i: (b, 0, g))],
        out_specs=(qspec, pl.BlockSpec((1, SUBLANES, LANES), lambda b, g, i: ((b * ngroups + g) * nb + i, 0, 0))),
        scratch_shapes=[pltpu.VMEM((NP, S, 2 * LANES), BF16),
                        pltpu.VMEM((NP, VT_ROWS, S), BF16),
                        pltpu.VMEM((NP, 2 * LANES, 2 * BS), BF16),
                        pltpu.VMEM((NP, SUBLANES, LANES), F32)],
        compiler_params=pltpu.CompilerParams(dimension_semantics=("arbitrary",) * 3,
                                             vmem_limit_bytes=VMEM_LIMIT_BYTES),
        name="moba_bounded" if bounded else "moba_exact",
    )(q, k, v, kmean)


def _moba(q, k, v, kmean):
    o_fast, lmin = _moba_call(q, k, v, kmean, True)
    ok = jnp.all(lmin[:, 0, 0] > MOBA_MIN_DENOM)
    return lax.cond(ok, lambda: o_fast, lambda: _moba_call(q, k, v, kmean, False)[0])


def _out_ffn2_kernel(x1_ref, oa_ref, om_ref, g2_ref, sh_ref, sc_ref, g3_ref, an_ref, wo_ref,
                     nw_ref, wg_ref, wu_ref, wd_ref, fn_ref, o_ref, a_ref):
    oa = _rmsnorm(oa_ref[...], an_ref[...]).astype(BF16)
    mix = (jnp.dot(oa, wo_ref[0:ATTN_WIDTH, :], preferred_element_type=F32)
           + jnp.dot(om_ref[...], wo_ref[ATTN_WIDTH:, :], preferred_element_type=F32))
    x2 = x1_ref[...] + (1.0 + g2_ref[0, 0]) * mix
    h = _rms_mod(x2, nw_ref[...], sh_ref[0, 0], sc_ref[0, 0])
    y = _swiglu(h.astype(BF16), wg_ref, wu_ref, wd_ref, a_ref)
    x3 = x2 + (0.5 * (1.0 + g3_ref[0, 0])) * y
    o_ref[...] = _rmsnorm(x3, fn_ref[...])


def _out_ffn2(x1, oa, om, mod4, attn_norm, w_out, norm_w, wg, wu, wd, final_norm, seq):
    T = x1.shape[0]
    tm = FFN_TOKENS
    tps = seq // tm
    tok = lambda w: pl.BlockSpec((tm, w), lambda i: (i, 0))
    return pl.pallas_call(
        _out_ffn2_kernel,
        out_shape=jax.ShapeDtypeStruct((T, D_MODEL), F32),
        grid=(T // tm,),
        in_specs=[tok(D_MODEL), tok(ATTN_WIDTH), tok(MLSTM_WIDTH),
                  _mod_spec(5, tps), _mod_spec(6, tps), _mod_spec(7, tps), _mod_spec(8, tps),
                  _resident((1, ATTN_WIDTH)), _resident((D_MODEL, D_MODEL)), _resident((1, D_MODEL)),
                  _resident((D_MODEL, D_FF)), _resident((D_MODEL, D_FF)), _resident((D_FF, D_MODEL)),
                  _resident((1, D_MODEL))],
        out_specs=tok(D_MODEL),
        scratch_shapes=[pltpu.VMEM((tm, D_FF), BF16)],
        compiler_params=pltpu.CompilerParams(dimension_semantics=("arbitrary",),
                                             vmem_limit_bytes=VMEM_LIMIT_BYTES),
        name="out_ffn2",
    )(x1, oa, om, mod4, mod4, mod4, mod4, attn_norm, w_out, norm_w, wg, wu, wd, final_norm)


def kernel(x, c, w_ada, b_ada, ffn1_norm, ffn1_w_gate, ffn1_w_up, ffn1_w_down, mix_norm, w_in, conv_w, conv_b,
           w_q_m, w_k_m, w_v_m, w_if, b_if, mlstm_norm, mlstm_skip, attn_norm, w_out,
           ffn2_norm, ffn2_w_gate, ffn2_w_up, ffn2_w_down, final_norm):
    B, S, D = x.shape
    T = B * S
    assert w_ada.shape[0] == 1, "only depth 1 is supported"
    bf = lambda a: a.astype(BF16)
    xf = x.reshape(T, D)
    for l in range(1):
        mod4 = _adaln(c, w_ada[l], b_ada[l][None, :]).reshape(N_MOD, B, 1, D)
        x1 = _ffn1(xf, mod4, ffn1_norm[l][None, :], bf(ffn1_w_gate[l]), bf(ffn1_w_up[l]), bf(ffn1_w_down[l]), S)
        q, k, v, xm, z, kmean = _proj(x1, mod4, mix_norm[l][None, :], bf(w_in[l]), S)
        o_m = _mlstm(xm.reshape(B, S, MLSTM_WIDTH), z.reshape(B, S, MLSTM_WIDTH),
                     conv_w[l], conv_b[l][None, :],
                     bf(jnp.concatenate([w_q_m[l], w_k_m[l]], axis=-1)), bf(w_v_m[l]),
                     bf(w_if[l]), bf(w_if[l].T), b_if[l][None, :], b_if[l][:, None],
                     mlstm_norm[l][None, :], mlstm_skip[l][None, :])
        o_a = _moba(q.reshape(B, S, ATTN_WIDTH), k.reshape(B, S, ATTN_WIDTH), v.reshape(B, S, ATTN_WIDTH),
                    kmean.reshape(B, S // MOBA_BLOCK, ATTN_WIDTH))
        xf = _out_ffn2(x1, o_a.reshape(T, ATTN_WIDTH), o_m.reshape(T, MLSTM_WIDTH), mod4,
                       attn_norm[l][None, :], bf(w_out[l]), ffn2_norm[l][None, :],
                       bf(ffn2_w_gate[l]), bf(ffn2_w_up[l]), bf(ffn2_w_down[l]), final_norm[None, :], S)
    return xf.reshape(B, S, D)
```

```python
import functools

import jax
import jax.numpy as jnp
from jax import lax
from jax.experimental import pallas as pl
from jax.experimental.pallas import tpu as pltpu

F32 = jnp.float32
BF16 = jnp.bfloat16

D_MODEL = 1024
ATTN_HEADS = 8
ATTN_WIDTH = 512
ATTN_HEAD_DIM = 64
MLSTM_HEADS = 4
MLSTM_WIDTH = 512
MLSTM_HEAD_DIM = 128
IN_COLS = 3 * ATTN_WIDTH + 2 * MLSTM_WIDTH
CONV_WIDTH = 4
MOBA_BLOCK = 256
MOBA_TOPK = 3
D_FF = 2816
N_MOD = 9
EPS = 1e-6

LANES = 128
SUBLANES = 8
VMEM_LIMIT_BYTES = 56 * 1024 * 1024

FFN_TOKENS = 512
MXU_WIDTH = 256
FFN_CHUNKS = ((0, 6 * MXU_WIDTH), (6 * MXU_WIDTH, D_FF))
MLSTM_CHUNK = 256
MLSTM_STATE_ROWS = MLSTM_HEAD_DIM + 16
MLSTM_SEQS = 2
MOBA_PAIRS = 2
MOBA_LOOKAHEAD = 2
MOBA_BOUND_SLACK = 1.0 + 2.0 ** -6
MOBA_MIN_DENOM = 1e-25
VT_ROWS = LANES + 16
MASK_VALUE = -1e30

_NT = (((1,), (1,)), ((), ()))
_TN = (((0,), (0,)), ((), ()))


def _sigmoid(x):
    return 1.0 / (1.0 + jnp.exp(-x))


def _rmsnorm(x, w):
    return x * lax.rsqrt(jnp.mean(x * x, axis=-1, keepdims=True) + EPS) * w


def _rms_mod(x, w, shift, scale):
    return _rmsnorm(x, w) * (1.0 + scale) + shift


def _swiglu(h_bf, wg_ref, wu_ref, wd_ref, a_ref):
    for lo, hi in FFN_CHUNKS:
        sl = slice(lo, hi)
        g = jnp.dot(h_bf, wg_ref[:, sl], preferred_element_type=F32)
        u = jnp.dot(h_bf, wu_ref[:, sl], preferred_element_type=F32)
        a_ref[:, sl] = (g * _sigmoid(g) * u).astype(BF16)
    return jnp.dot(a_ref[...], wd_ref[...], preferred_element_type=F32)


def _adaln_kernel(c_ref, w_ref, b_ref, o_ref):
    o_ref[0] = jnp.dot(c_ref[...].astype(BF16), w_ref[...].astype(BF16),
                       preferred_element_type=F32) + b_ref[...]


def _adaln(c, w_ada, b_ada):
    B = c.shape[0]
    return pl.pallas_call(
        _adaln_kernel,
        out_shape=jax.ShapeDtypeStruct((N_MOD, B, D_MODEL), F32),
        grid=(N_MOD,),
        in_specs=[
            pl.BlockSpec((B, D_MODEL), lambda j: (0, 0)),
            pl.BlockSpec((D_MODEL, D_MODEL), lambda j: (0, j)),
            pl.BlockSpec((1, D_MODEL), lambda j: (0, j)),
        ],
        out_specs=pl.BlockSpec((1, B, D_MODEL), lambda j: (j, 0, 0)),
        compiler_params=pltpu.CompilerParams(dimension_semantics=("arbitrary",),
                                             vmem_limit_bytes=VMEM_LIMIT_BYTES),
        name="adaln",
    )(c, w_ada, b_ada)


def _resident(shape):
    nd = len(shape)
    return pl.BlockSpec(shape, lambda *_: (0,) * nd, pipeline_mode=pl.Buffered(1))


def _mod_spec(k, tiles_per_seq):
    return pl.BlockSpec((1, 1, 1, D_MODEL), lambda i: (k, i // tiles_per_seq, 0, 0))


def _ffn1_kernel(x_ref, sh_ref, sc_ref, g_ref, nw_ref, wg_ref, wu_ref, wd_ref, o_ref, a_ref):
    x = x_ref[...]
    h = _rms_mod(x, nw_ref[...], sh_ref[0, 0], sc_ref[0, 0])
    y = _swiglu(h.astype(BF16), wg_ref, wu_ref, wd_ref, a_ref)
    o_ref[...] = x + (0.5 * (1.0 + g_ref[0, 0])) * y


def _ffn1(x2d, mod4, norm_w, wg, wu, wd, seq):
    T = x2d.shape[0]
    tm = FFN_TOKENS
    tps = seq // tm
    tok = pl.BlockSpec((tm, D_MODEL), lambda i: (i, 0))
    return pl.pallas_call(
        _ffn1_kernel,
        out_shape=jax.ShapeDtypeStruct((T, D_MODEL), F32),
        grid=(T // tm,),
        in_specs=[tok, _mod_spec(0, tps), _mod_spec(1, tps), _mod_spec(2, tps),
                  _resident((1, D_MODEL)),
                  _resident((D_MODEL, D_FF)), _resident((D_MODEL, D_FF)), _resident((D_FF, D_MODEL))],
        out_specs=tok,
        scratch_shapes=[pltpu.VMEM((tm, D_FF), BF16)],
        compiler_params=pltpu.CompilerParams(dimension_semantics=("arbitrary",),
                                             vmem_limit_bytes=VMEM_LIMIT_BYTES),
        name="ffn1",
    )(x2d, mod4, mod4, mod4, norm_w, wg, wu, wd)


def _proj_kernel(x_ref, sh_ref, sc_ref, nw_ref, win_ref,
                 q_ref, k_ref, v_ref, xm_ref, z_ref, km_ref):
    h = _rms_mod(x_ref[...], nw_ref[...], sh_ref[0, 0], sc_ref[0, 0]).astype(BF16)
    proj = jnp.dot(h, win_ref[...], preferred_element_type=F32)
    aw = ATTN_WIDTH
    q_ref[...] = (proj[:, 0:aw] * (ATTN_HEAD_DIM ** -0.5)).astype(BF16)
    k = proj[:, aw:2 * aw]
    k_ref[...] = k.astype(BF16)
    v_ref[...] = proj[:, 2 * aw:3 * aw].astype(BF16)
    xm_ref[...] = proj[:, 3 * aw:3 * aw + MLSTM_WIDTH]
    z_ref[...] = proj[:, 3 * aw + MLSTM_WIDTH:]
    for r in range(FFN_TOKENS // MOBA_BLOCK):
        km_ref[r] = jnp.mean(k[r * MOBA_BLOCK:(r + 1) * MOBA_BLOCK], axis=0, keepdims=True)


def _proj(x1, mod4, norm_w, w_in, seq):
    T = x1.shape[0]
    tm = FFN_TOKENS
    tps = seq // tm
    nb = tm // MOBA_BLOCK
    tok = lambda w: pl.BlockSpec((tm, w), lambda i: (i, 0))
    return pl.pallas_call(
        _proj_kernel,
        out_shape=(jax.ShapeDtypeStruct((T, ATTN_WIDTH), BF16),
                   jax.ShapeDtypeStruct((T, ATTN_WIDTH), BF16),
                   jax.ShapeDtypeStruct((T, ATTN_WIDTH), BF16),
                   jax.ShapeDtypeStruct((T, MLSTM_WIDTH), F32),
                   jax.ShapeDtypeStruct((T, MLSTM_WIDTH), F32),
                   jax.ShapeDtypeStruct((T // MOBA_BLOCK, 1, ATTN_WIDTH), F32)),
        grid=(T // tm,),
        in_specs=[tok(D_MODEL), _mod_spec(3, tps), _mod_spec(4, tps),
                  _resident((1, D_MODEL)), _resident((D_MODEL, IN_COLS))],
        out_specs=(tok(ATTN_WIDTH), tok(ATTN_WIDTH), tok(ATTN_WIDTH),
                   tok(MLSTM_WIDTH), tok(MLSTM_WIDTH),
                   pl.BlockSpec((nb, 1, ATTN_WIDTH), lambda i: (i, 0, 0))),
        compiler_params=pltpu.CompilerParams(dimension_semantics=("arbitrary",),
                                             vmem_limit_bytes=VMEM_LIMIT_BYTES),
        name="proj",
    )(x1, mod4, mod4, norm_w, w_in)


def _gate_transform(g, gate_idx):
    log_sig = jnp.minimum(g, 0.0) - jnp.log1p(jnp.exp(-jnp.abs(g)))
    return jnp.where(gate_idx >= MLSTM_HEADS, log_sig, g)


def _mlstm_kernel(xm_ref, z_ref, cw_ref, cb_ref, wqk_ref, wv_ref, wift_ref,
                  bifr_ref, mn_ref, ms_ref, o_ref, ct_ref, m_ref, carry_ref):
    L = MLSTM_CHUNK
    dh = MLSTM_HEAD_DIM
    H = MLSTM_HEADS

    @pl.when(pl.program_id(1) == 0)
    def _():
        ct_ref[...] = jnp.zeros_like(ct_ref)
        m_ref[...] = jnp.zeros_like(m_ref)
        carry_ref[...] = jnp.zeros_like(carry_ref)

    cw = cw_ref[...]
    row8 = lax.broadcasted_iota(jnp.int32, (SUBLANES, MLSTM_WIDTH), 0)
    s_idx = lax.broadcasted_iota(jnp.int32, (L, L), 0)
    t_idx = lax.broadcasted_iota(jnp.int32, (L, L), 1)
    causal_t = s_idx <= t_idx
    tri_t = causal_t.astype(F32)
    ones_rows = (lax.broadcasted_iota(jnp.int32, (MLSTM_STATE_ROWS - dh, L), 0) == 0).astype(F32)

    def front(sq):
        xm = xm_ref[sq]
        carry = carry_ref[sq]
        acc = xm * cw[CONV_WIDTH - 1:CONV_WIDTH] + cb_ref[...]
        for k in range(1, CONV_WIDTH):
            rolled = pltpu.roll(xm, k, 0)
            top = jnp.where(row8 < k, pltpu.roll(carry, k, 0), rolled[0:SUBLANES])
            shifted = jnp.concatenate([top, rolled[SUBLANES:]], axis=0)
            acc = acc + shifted * cw[CONV_WIDTH - 1 - k:CONV_WIDTH - k]
        carry_ref[sq] = xm[L - SUBLANES:L]
        xc = acc * _sigmoid(acc)

        xc_bf = xc.astype(BF16)
        xm_bf = xm.astype(BF16)
        qk, vv, cat = [], [], []
        for h in range(H):
            hs = slice(h * dh, (h + 1) * dh)
            qk_h = jnp.dot(xc_bf[:, hs], wqk_ref[h], preferred_element_type=F32)
            v_h = jnp.dot(xm_bf[:, hs], wv_ref[h], preferred_element_type=F32)
            qk.append(qk_h)
            vv.append(v_h)
            cat += [qk_h.astype(BF16), v_h.astype(BF16)]
        cat = jnp.concatenate(cat, axis=1)

        g_row = lax.dot_general(wift_ref[...], cat, _NT, preferred_element_type=F32) + bifr_ref[...]
        a_row = _gate_transform(g_row, lax.broadcasted_iota(jnp.int32, (2 * H, L), 0))
        b_row = jnp.dot(a_row, tri_t, preferred_element_type=F32, precision=lax.Precision.HIGHEST)
        c_row = jnp.concatenate([a_row[0:H] - b_row[H:2 * H], jnp.zeros((SUBLANES - H, L), F32)], axis=0)
        return dict(xc=xc, qk=qk, vv=vv, a_row=a_row, b_row=b_row, c_col=jnp.transpose(c_row))

    def head(sq, h, f):
        hs = slice(h * dh, (h + 1) * dh)
        qt_bf = jnp.transpose(f["qk"][h][:, :dh]).astype(BF16)
        k_bf = (f["qk"][h][:, dh:] * (dh ** -0.5)).astype(BF16)
        v_aug_t = jnp.concatenate([jnp.transpose(f["vv"][h]), ones_rows], axis=0)
        c_c = f["c_col"][:, h:h + 1]
        b_r = f["b_row"][H + h:H + h + 1, :]
        i_r = f["a_row"][h:h + 1, :]
        m_prev = m_ref[sq, h, 0:1, 0:1]

        log_d = jnp.where(causal_t, c_c + b_r, -jnp.inf)
        log_inter = b_r + m_prev
        m_t = jnp.maximum(log_inter, jnp.max(log_d, axis=0, keepdims=True))
        w_intra = jnp.exp(log_d - m_t)
        w_inter = jnp.exp(log_inter - m_t)
        s_t = jnp.dot(k_bf, qt_bf, preferred_element_type=F32) * w_intra
        ct = ct_ref[sq, h]
        comb = (jnp.dot(v_aug_t.astype(BF16), s_t.astype(BF16), preferred_element_type=F32)
                + w_inter * jnp.dot(ct.astype(BF16), qt_bf, preferred_element_type=F32))
        hh_t = comb[:dh] / jnp.maximum(jnp.abs(comb[dh:dh + 1]), jnp.exp(-m_t))

        b_last = b_r[:, L - 1:L]
        log_g = b_last - b_r + i_r
        m_new = jnp.maximum(b_last + m_prev, jnp.max(log_g, axis=1, keepdims=True))
        w_g = jnp.exp(log_g - m_new)
        decay = jnp.exp(b_last + m_prev - m_new)
        upd = jnp.dot((v_aug_t * w_g).astype(BF16), k_bf, preferred_element_type=F32)
        ct_ref[sq, h] = decay * ct + upd
        m_ref[sq, h] = jnp.broadcast_to(m_new, (SUBLANES, LANES))

        mu = jnp.mean(hh_t, axis=0, keepdims=True)
        dlt = hh_t - mu
        var = jnp.mean(dlt * dlt, axis=0, keepdims=True)
        hn = jnp.transpose(dlt * lax.rsqrt(var + EPS)) * mn_ref[:, hs]
        zh = z_ref[sq, :, hs]
        return ((hn + ms_ref[:, hs] * f["xc"][:, hs]) * (zh * _sigmoid(zh))).astype(BF16)

    fronts = [front(sq) for sq in range(MLSTM_SEQS)]
    outs = [[] for _ in range(MLSTM_SEQS)]
    for h in range(H):
        for sq in range(MLSTM_SEQS):
            outs[sq].append(head(sq, h, fronts[sq]))
    for sq in range(MLSTM_SEQS):
        o_ref[sq] = jnp.concatenate(outs[sq], axis=1)


def _mlstm(xm, z, conv_w, conv_b, wqk, wv, wif_t, bif_r, mnorm, mskip):
    B, S, W = xm.shape
    L = MLSTM_CHUNK
    NS = MLSTM_SEQS
    assert B % NS == 0 and S % L == 0
    tok = pl.BlockSpec((NS, L, W), lambda b, j: (b, j, 0))
    return pl.pallas_call(
        _mlstm_kernel,
        out_shape=jax.ShapeDtypeStruct((B, S, W), BF16),
        grid=(B // NS, S // L),
        in_specs=[tok, tok,
                  _resident(conv_w.shape), _resident(conv_b.shape), _resident(wqk.shape), _resident(wv.shape),
                  _resident(wif_t.shape), _resident(bif_r.shape),
                  _resident(mnorm.shape), _resident(mskip.shape)],
        out_specs=tok,
        scratch_shapes=[pltpu.VMEM((NS, MLSTM_HEADS, MLSTM_STATE_ROWS, MLSTM_HEAD_DIM), F32),
                        pltpu.VMEM((NS, MLSTM_HEADS, SUBLANES, LANES), F32),
                        pltpu.VMEM((NS, SUBLANES, W), F32)],
        compiler_params=pltpu.CompilerParams(dimension_semantics=("arbitrary", "arbitrary"),
                                             vmem_limit_bytes=VMEM_LIMIT_BYTES),
        name="mlstm",
    )(xm, z, conv_w, conv_b, wqk, wv, wif_t, bif_r, mnorm, mskip)


def _moba_kernel(q_ref, k_ref, v_ref, km_ref, o_ref, lmin_ref, kaug_ref, vt_ref, qaug_ref, kn_ref, *, bounded):
    BS = MOBA_BLOCK
    NP = MOBA_PAIRS
    S = k_ref.shape[1]
    nb = km_ref.shape[1]
    hd = ATTN_HEAD_DIM
    i = pl.program_id(2)

    @pl.when(i == 0)
    def _():
        lane_s = lax.broadcasted_iota(jnp.int32, (S, LANES), 1)
        blk_s = lax.shift_right_logical(lax.broadcasted_iota(jnp.int32, (S, LANES), 0), BS.bit_length() - 1)
        k_tail = ((lane_s == blk_s) | (lane_s == nb)).astype(BF16)
        v_tail = (lax.broadcasted_iota(jnp.int32, (VT_ROWS - LANES, S), 0) == 0).astype(BF16)
        for p in range(NP):
            ps = slice(p * LANES, (p + 1) * LANES)
            kaug_ref[p, :, 0:LANES] = k_ref[0, :, ps]
            kaug_ref[p, :, LANES:] = k_tail
            vt_ref[p, 0:LANES, :] = jnp.transpose(v_ref[0, :, ps].astype(F32)).astype(BF16)
            vt_ref[p, LANES:, :] = v_tail
            if bounded:
                kf = k_ref[0, :, ps].astype(F32)
                head_of_dim = lax.shift_right_logical(lax.broadcasted_iota(jnp.int32, (LANES, LANES), 0),
                                                      hd.bit_length() - 1)
                ind = (head_of_dim == lax.broadcasted_iota(jnp.int32, (LANES, LANES), 1)).astype(BF16)
                ksq = jnp.dot((kf * kf).astype(BF16), ind, preferred_element_type=F32)
                kn_ref[p] = jnp.broadcast_to(jnp.max(ksq, axis=0, keepdims=True), kn_ref.shape[1:])

    q_t = jnp.transpose(q_ref[0].astype(F32))
    dim_t = lax.broadcasted_iota(jnp.int32, (LANES, 2 * BS), 0)
    col_t = lax.broadcasted_iota(jnp.int32, (LANES, 2 * BS), 1)
    in_head = lax.shift_right_logical(dim_t, hd.bit_length() - 1) == lax.shift_right_logical(col_t, BS.bit_length() - 1)
    qq, gates = [], []
    for p in range(NP):
        q_p = q_t[p * LANES:(p + 1) * LANES]
        qq_p = jnp.where(in_head, jnp.concatenate([q_p, q_p], axis=1), 0.0)
        km = km_ref[0, :, p * LANES:(p + 1) * LANES]
        km_hi = km.astype(BF16)
        km_lo = (km - km_hi.astype(F32)).astype(BF16)
        qq_bf = qq_p.astype(BF16)
        gates.append(jnp.dot(km_hi, qq_bf, preferred_element_type=F32)
                     + jnp.dot(km_lo, qq_bf, preferred_element_type=F32))
        qq.append(qq_p)

    ncol = NP * 2 * BS
    blk_t = lax.broadcasted_iota(jnp.int32, (nb, ncol), 0)
    cur = jnp.where(blk_t < i, jnp.concatenate(gates, axis=1), -jnp.inf)
    bias_t = jnp.full((nb, ncol), MASK_VALUE, F32)
    for _ in range(MOBA_TOPK):
        mx = jnp.max(cur, axis=0, keepdims=True)
        first = jnp.min(jnp.where(cur == mx, blk_t, nb), axis=0, keepdims=True)
        pick = (blk_t == first) & (mx > -jnp.inf)
        bias_t = jnp.where(pick, 0.0, bias_t)
        cur = jnp.where(pick, -jnp.inf, cur)
    bias_t = jnp.where(blk_t == i, 0.0, bias_t).astype(BF16)

    pad_rows = lax.broadcasted_iota(jnp.int32, (LANES - nb, 2 * BS), 0)
    pad_cols = lax.broadcasted_iota(jnp.int32, (LANES - nb, 2 * BS), 1)
    for p in range(NP):
        if bounded:
            qsq = jnp.sum(qq[p] * qq[p], axis=0, keepdims=True)
            ksq_max = jnp.where(pad_cols[0:1] < BS, kn_ref[p, 0:1, 0:1], kn_ref[p, 0:1, 1:2])
            bound = jnp.sqrt(qsq * ksq_max) * MOBA_BOUND_SLACK
            tail = jnp.where(pad_rows == 0, -bound, 0.0)
        else:
            tail = jnp.zeros((LANES - nb, 2 * BS), F32)
        qaug_ref[p] = jnp.concatenate(
            [qq[p].astype(BF16), bias_t[:, p * 2 * BS:(p + 1) * 2 * BS], tail.astype(BF16)], axis=0)

    future_key = (lax.broadcasted_iota(jnp.int32, (BS, 2 * BS), 0)
                  > (lax.broadcasted_iota(jnp.int32, (BS, 2 * BS), 1) & (BS - 1)))

    def scores(j):
        return [jnp.dot(kaug_ref[p, j * BS:(j + 1) * BS, :], qaug_ref[p], preferred_element_type=F32)
                for p in range(NP)]

    for c in range(nb):
        @pl.when(i == c)
        def _():
            m = [None] * NP
            pv = [None] * NP
            order = list(range(c, -1, -1))
            pending = [scores(j) for j in order[:MOBA_LOOKAHEAD]]
            for idx, j in enumerate(order):
                s_all = pending.pop(0)
                if idx + MOBA_LOOKAHEAD < len(order):
                    pending.append(scores(order[idx + MOBA_LOOKAHEAD]))
                for p in range(NP):
                    s = s_all[p]
                    if j == c:
                        s = jnp.where(future_key, MASK_VALUE, s)
                    v_j = vt_ref[p, :, j * BS:(j + 1) * BS]
                    if bounded:
                        d = jnp.dot(v_j, jnp.exp(s).astype(BF16), preferred_element_type=F32)
                        pv[p] = d if pv[p] is None else pv[p] + d
                        continue
                    cm = jnp.max(s, axis=0, keepdims=True)
                    if m[p] is None:
                        m[p] = cm
                        pv[p] = jnp.dot(v_j, jnp.exp(s - cm).astype(BF16), preferred_element_type=F32)
                    else:
                        m_new = jnp.maximum(m[p], cm)
                        pv[p] = (jnp.exp(m[p] - m_new) * pv[p]
                                 + jnp.dot(v_j, jnp.exp(s - m_new).astype(BF16), preferred_element_type=F32))
                        m[p] = m_new
            outs, lmin = [], None
            for p in range(NP):
                denom = pv[p][LANES:LANES + 1]
                outs += [pv[p][0:hd, 0:BS] / denom[:, 0:BS], pv[p][hd:LANES, BS:] / denom[:, BS:]]
                dmin = jnp.min(denom, axis=1, keepdims=True)
                lmin = dmin if lmin is None else jnp.minimum(lmin, dmin)
            o_ref[0] = jnp.transpose(jnp.concatenate(outs, axis=0))
            lmin_ref[0] = jnp.broadcast_to(lmin, lmin_ref.shape[1:])


def _moba_call(q, k, v, kmean, bounded):
    B, S, W = q.shape
    BS = MOBA_BLOCK
    NP = MOBA_PAIRS
    nb = S // BS
    ngroups = W // (NP * LANES)
    assert nb < LANES - 1 and W % (NP * LANES) == 0
    qspec = pl.BlockSpec((1, BS, NP * LANES), lambda b, g, i: (b, i, g))
    kvspec = pl.BlockSpec((1, S, NP * LANES), lambda b, g, i: (b, 0, g))
    return pl.pallas_call(
        functools.partial(_moba_kernel, bounded=bounded),
        out_shape=(jax.ShapeDtypeStruct((B, S, W), F32),
                   jax.ShapeDtypeStruct((B * ngroups * nb, SUBLANES, LANES), F32)),
        grid=(B, ngroups, nb),
        in_specs=[qspec, kvspec, kvspec, pl.BlockSpec((1, nb, NP * LANES), lambda b, g, i: (b, 0, g))],
        out_specs=(qspec, pl.BlockSpec((1, SUBLANES, LANES), lambda b, g, i: ((b * ngroups + g) * nb + i, 0, 0))),
        scratch_shapes=[pltpu.VMEM((NP, S, 2 * LANES), BF16),
                        pltpu.VMEM((NP, VT_ROWS, S), BF16),
                        pltpu.VMEM((NP, 2 * LANES, 2 * BS), BF16),
                        pltpu.VMEM((NP, SUBLANES, LANES), F32)],
        compiler_params=pltpu.CompilerParams(dimension_semantics=("arbitrary",) * 3,
                                             vmem_limit_bytes=VMEM_LIMIT_BYTES),
        name="moba_bounded" if bounded else "moba_exact",
    )(q, k, v, kmean)


def _moba(q, k, v, kmean):
    o_fast, lmin = _moba_call(q, k, v, kmean, True)
    ok = jnp.all(lmin[:, 0, 0] > MOBA_MIN_DENOM)
    return lax.cond(ok, lambda: o_fast, lambda: _moba_call(q, k, v, kmean, False)[0])


def _out_ffn2_kernel(x1_ref, oa_ref, om_ref, g2_ref, sh_ref, sc_ref, g3_ref, an_ref, wo_ref,
                     nw_ref, wg_ref, wu_ref, wd_ref, fn_ref, o_ref, a_ref):
    oa = _rmsnorm(oa_ref[...], an_ref[...]).astype(BF16)
    mix = (jnp.dot(oa, wo_ref[0:ATTN_WIDTH, :], preferred_element_type=F32)
           + jnp.dot(om_ref[...], wo_ref[ATTN_WIDTH:, :], preferred_element_type=F32))
    x2 = x1_ref[...] + (1.0 + g2_ref[0, 0]) * mix
    h = _rms_mod(x2, nw_ref[...], sh_ref[0, 0], sc_ref[0, 0])
    y = _swiglu(h.astype(BF16), wg_ref, wu_ref, wd_ref, a_ref)
    x3 = x2 + (0.5 * (1.0 + g3_ref[0, 0])) * y
    o_ref[...] = _rmsnorm(x3, fn_ref[...])


def _out_ffn2(x1, oa, om, mod4, attn_norm, w_out, norm_w, wg, wu, wd, final_norm, seq):
    T = x1.shape[0]
    tm = FFN_TOKENS
    tps = seq // tm
    tok = lambda w: pl.BlockSpec((tm, w), lambda i: (i, 0))
    return pl.pallas_call(
        _out_ffn2_kernel,
        out_shape=jax.ShapeDtypeStruct((T, D_MODEL), F32),
        grid=(T // tm,),
        in_specs=[tok(D_MODEL), tok(ATTN_WIDTH), tok(MLSTM_WIDTH),
                  _mod_spec(5, tps), _mod_spec(6, tps), _mod_spec(7, tps), _mod_spec(8, tps),
                  _resident((1, ATTN_WIDTH)), _resident((D_MODEL, D_MODEL)), _resident((1, D_MODEL)),
                  _resident((D_MODEL, D_FF)), _resident((D_MODEL, D_FF)), _resident((D_FF, D_MODEL)),
                  _resident((1, D_MODEL))],
        out_specs=tok(D_MODEL),
        scratch_shapes=[pltpu.VMEM((tm, D_FF), BF16)],
        compiler_params=pltpu.CompilerParams(dimension_semantics=("arbitrary",),
                                             vmem_limit_bytes=VMEM_LIMIT_BYTES),
        name="out_ffn2",
    )(x1, oa, om, mod4, mod4, mod4, mod4, attn_norm, w_out, norm_w, wg, wu, wd, final_norm)


def kernel(x, c, w_ada, b_ada, ffn1_norm, ffn1_w_gate, ffn1_w_up, ffn1_w_down, mix_norm, w_in, conv_w, conv_b,
           w_q_m, w_k_m, w_v_m, w_if, b_if, mlstm_norm, mlstm_skip, attn_norm, w_out,
           ffn2_norm, ffn2_w_gate, ffn2_w_up, ffn2_w_down, final_norm):
    B, S, D = x.shape
    T = B * S
    assert w_ada.shape[0] == 1, "only depth 1 is supported"
    bf = lambda a: a.astype(BF16)
    xf = x.reshape(T, D)
    for l in range(1):
        mod4 = _adaln(c, w_ada[l], b_ada[l][None, :]).reshape(N_MOD, B, 1, D)
        x1 = _ffn1(xf, mod4, ffn1_norm[l][None, :], bf(ffn1_w_gate[l]), bf(ffn1_w_up[l]), bf(ffn1_w_down[l]), S)
        q, k, v, xm, z, kmean = _proj(x1, mod4, mix_norm[l][None, :], bf(w_in[l]), S)
        o_m = _mlstm(xm.reshape(B, S, MLSTM_WIDTH), z.reshape(B, S, MLSTM_WIDTH),
                     conv_w[l], conv_b[l][None, :],
                     bf(jnp.concatenate([w_q_m[l], w_k_m[l]], axis=-1)), bf(w_v_m[l]),
                     bf(w_if[l].T), b_if[l][:, None],
                     mlstm_norm[l][None, :], mlstm_skip[l][None, :])
        o_a = _moba(q.reshape(B, S, ATTN_WIDTH), k.reshape(B, S, ATTN_WIDTH), v.reshape(B, S, ATTN_WIDTH),
                    kmean.reshape(B, S // MOBA_BLOCK, ATTN_WIDTH))
        xf = _out_ffn2(x1, o_a.reshape(T, ATTN_WIDTH), o_m.reshape(T, MLSTM_WIDTH), mod4,
                       attn_norm[l][None, :], bf(w_out[l]), ffn2_norm[l][None, :],
                       bf(ffn2_w_gate[l]), bf(ffn2_w_up[l]), bf(ffn2_w_down[l]), final_norm[None, :], S)
    return xf.reshape(B, S, D)
```

```python
import functools

import jax
import jax.numpy as jnp
from jax import lax
from jax.experimental import pallas as pl
from jax.experimental.pallas import tpu as pltpu

F32 = jnp.float32
BF16 = jnp.bfloat16

D_MODEL = 1024
ATTN_HEADS = 8
ATTN_WIDTH = 512
ATTN_HEAD_DIM = 64
MLSTM_HEADS = 4
MLSTM_WIDTH = 512
MLSTM_HEAD_DIM = 128
IN_COLS = 3 * ATTN_WIDTH + 2 * MLSTM_WIDTH
CONV_WIDTH = 4
MOBA_BLOCK = 256
MOBA_TOPK = 3
D_FF = 2816
N_MOD = 9
EPS = 1e-6

LANES = 128
SUBLANES = 8
VMEM_LIMIT_BYTES = 56 * 1024 * 1024

FFN_TOKENS = 512
MXU_WIDTH = 256
FFN_CHUNKS = ((0, 6 * MXU_WIDTH), (6 * MXU_WIDTH, D_FF))
MLSTM_CHUNK = 256
MLSTM_STATE_ROWS = MLSTM_HEAD_DIM + 16
MLSTM_SEQS = 2
MOBA_PAIRS = 2
MOBA_LOOKAHEAD = 2
MOBA_BOUND_SLACK = 1.0 + 2.0 ** -6
MOBA_MIN_DENOM = 1e-25
VT_ROWS = ATTN_HEAD_DIM + 16
MASK_VALUE = -1e30

_NT = (((1,), (1,)), ((), ()))
_TN = (((0,), (0,)), ((), ()))


def _sigmoid(x):
    return 1.0 / (1.0 + jnp.exp(-x))


def _rmsnorm(x, w):
    return x * lax.rsqrt(jnp.mean(x * x, axis=-1, keepdims=True) + EPS) * w


def _rms_mod(x, w, shift, scale):
    return _rmsnorm(x, w) * (1.0 + scale) + shift


def _swiglu(h_bf, wg_ref, wu_ref, wd_ref, a_ref):
    for lo, hi in FFN_CHUNKS:
        sl = slice(lo, hi)
        g = jnp.dot(h_bf, wg_ref[:, sl], preferred_element_type=F32)
        u = jnp.dot(h_bf, wu_ref[:, sl], preferred_element_type=F32)
        a_ref[:, sl] = (g * _sigmoid(g) * u).astype(BF16)
    return jnp.dot(a_ref[...], wd_ref[...], preferred_element_type=F32)


def _adaln_kernel(c_ref, w_ref, b_ref, o_ref):
    o_ref[0] = jnp.dot(c_ref[...].astype(BF16), w_ref[...].astype(BF16),
                       preferred_element_type=F32) + b_ref[...]


def _adaln(c, w_ada, b_ada):
    B = c.shape[0]
    return pl.pallas_call(
        _adaln_kernel,
        out_shape=jax.ShapeDtypeStruct((N_MOD, B, D_MODEL), F32),
        grid=(N_MOD,),
        in_specs=[
            pl.BlockSpec((B, D_MODEL), lambda j: (0, 0)),
            pl.BlockSpec((D_MODEL, D_MODEL), lambda j: (0, j)),
            pl.BlockSpec((1, D_MODEL), lambda j: (0, j)),
        ],
        out_specs=pl.BlockSpec((1, B, D_MODEL), lambda j: (j, 0, 0)),
        compiler_params=pltpu.CompilerParams(dimension_semantics=("arbitrary",),
                                             vmem_limit_bytes=VMEM_LIMIT_BYTES),
        name="adaln",
    )(c, w_ada, b_ada)


def _resident(shape):
    nd = len(shape)
    return pl.BlockSpec(shape, lambda *_: (0,) * nd, pipeline_mode=pl.Buffered(1))


def _mod_spec(k, tiles_per_seq):
    return pl.BlockSpec((1, 1, 1, D_MODEL), lambda i: (k, i // tiles_per_seq, 0, 0))


def _ffn1_kernel(x_ref, sh_ref, sc_ref, g_ref, nw_ref, wg_ref, wu_ref, wd_ref, o_ref, a_ref):
    x = x_ref[...]
    h = _rms_mod(x, nw_ref[...], sh_ref[0, 0], sc_ref[0, 0])
    y = _swiglu(h.astype(BF16), wg_ref, wu_ref, wd_ref, a_ref)
    o_ref[...] = x + (0.5 * (1.0 + g_ref[0, 0])) * y


def _ffn1(x2d, mod4, norm_w, wg, wu, wd, seq):
    T = x2d.shape[0]
    tm = FFN_TOKENS
    tps = seq // tm
    tok = pl.BlockSpec((tm, D_MODEL), lambda i: (i, 0))
    return pl.pallas_call(
        _ffn1_kernel,
        out_shape=jax.ShapeDtypeStruct((T, D_MODEL), F32),
        grid=(T // tm,),
        in_specs=[tok, _mod_spec(0, tps), _mod_spec(1, tps), _mod_spec(2, tps),
                  _resident((1, D_MODEL)),
                  _resident((D_MODEL, D_FF)), _resident((D_MODEL, D_FF)), _resident((D_FF, D_MODEL))],
        out_specs=tok,
        scratch_shapes=[pltpu.VMEM((tm, D_FF), BF16)],
        compiler_params=pltpu.CompilerParams(dimension_semantics=("arbitrary",),
                                             vmem_limit_bytes=VMEM_LIMIT_BYTES),
        name="ffn1",
    )(x2d, mod4, mod4, mod4, norm_w, wg, wu, wd)


def _proj_kernel(x_ref, sh_ref, sc_ref, nw_ref, win_ref,
                 q_ref, k_ref, v_ref, xm_ref, z_ref, km_ref):
    h = _rms_mod(x_ref[...], nw_ref[...], sh_ref[0, 0], sc_ref[0, 0]).astype(BF16)
    proj = jnp.dot(h, win_ref[...], preferred_element_type=F32)
    aw = ATTN_WIDTH
    q_ref[...] = (proj[:, 0:aw] * (ATTN_HEAD_DIM ** -0.5)).astype(BF16)
    k = proj[:, aw:2 * aw]
    k_ref[...] = k.astype(BF16)
    v_ref[...] = proj[:, 2 * aw:3 * aw].astype(BF16)
    xm_ref[...] = proj[:, 3 * aw:3 * aw + MLSTM_WIDTH]
    z_ref[...] = proj[:, 3 * aw + MLSTM_WIDTH:]
    for r in range(FFN_TOKENS // MOBA_BLOCK):
        km_ref[r] = jnp.mean(k[r * MOBA_BLOCK:(r + 1) * MOBA_BLOCK], axis=0, keepdims=True)


def _proj(x1, mod4, norm_w, w_in, seq):
    T = x1.shape[0]
    tm = FFN_TOKENS
    tps = seq // tm
    nb = tm // MOBA_BLOCK
    tok = lambda w: pl.BlockSpec((tm, w), lambda i: (i, 0))
    return pl.pallas_call(
        _proj_kernel,
        out_shape=(jax.ShapeDtypeStruct((T, ATTN_WIDTH), BF16),
                   jax.ShapeDtypeStruct((T, ATTN_WIDTH), BF16),
                   jax.ShapeDtypeStruct((T, ATTN_WIDTH), BF16),
                   jax.ShapeDtypeStruct((T, MLSTM_WIDTH), F32),
                   jax.ShapeDtypeStruct((T, MLSTM_WIDTH), F32),
                   jax.ShapeDtypeStruct((T // MOBA_BLOCK, 1, ATTN_WIDTH), F32)),
        grid=(T // tm,),
        in_specs=[tok(D_MODEL), _mod_spec(3, tps), _mod_spec(4, tps),
                  _resident((1, D_MODEL)), _resident((D_MODEL, IN_COLS))],
        out_specs=(tok(ATTN_WIDTH), tok(ATTN_WIDTH), tok(ATTN_WIDTH),
                   tok(MLSTM_WIDTH), tok(MLSTM_WIDTH),
                   pl.BlockSpec((nb, 1, ATTN_WIDTH), lambda i: (i, 0, 0))),
        compiler_params=pltpu.CompilerParams(dimension_semantics=("arbitrary",),
                                             vmem_limit_bytes=VMEM_LIMIT_BYTES),
        name="proj",
    )(x1, mod4, mod4, norm_w, w_in)


def _gate_transform(g, gate_idx):
    log_sig = jnp.minimum(g, 0.0) - jnp.log1p(jnp.exp(-jnp.abs(g)))
    return jnp.where(gate_idx >= MLSTM_HEADS, log_sig, g)


def _mlstm_kernel(xm_ref, z_ref, cw_ref, cb_ref, wqk_ref, wv_ref, wift_ref,
                  bifr_ref, mn_ref, ms_ref, o_ref, ct_ref, m_ref, carry_ref):
    L = MLSTM_CHUNK
    dh = MLSTM_HEAD_DIM
    H = MLSTM_HEADS

    @pl.when(pl.program_id(1) == 0)
    def _():
        ct_ref[...] = jnp.zeros_like(ct_ref)
        m_ref[...] = jnp.zeros_like(m_ref)
        carry_ref[...] = jnp.zeros_like(carry_ref)

    cw = cw_ref[...]
    row8 = lax.broadcasted_iota(jnp.int32, (SUBLANES, MLSTM_WIDTH), 0)
    s_idx = lax.broadcasted_iota(jnp.int32, (L, L), 0)
    t_idx = lax.broadcasted_iota(jnp.int32, (L, L), 1)
    causal_t = s_idx <= t_idx
    tri_t = causal_t.astype(F32)
    ones_rows = (lax.broadcasted_iota(jnp.int32, (MLSTM_STATE_ROWS - dh, L), 0) == 0).astype(F32)

    def front(sq):
        xm = xm_ref[sq]
        carry = carry_ref[sq]
        acc = xm * cw[CONV_WIDTH - 1:CONV_WIDTH] + cb_ref[...]
        for k in range(1, CONV_WIDTH):
            rolled = pltpu.roll(xm, k, 0)
            top = jnp.where(row8 < k, pltpu.roll(carry, k, 0), rolled[0:SUBLANES])
            shifted = jnp.concatenate([top, rolled[SUBLANES:]], axis=0)
            acc = acc + shifted * cw[CONV_WIDTH - 1 - k:CONV_WIDTH - k]
        carry_ref[sq] = xm[L - SUBLANES:L]
        xc = acc * _sigmoid(acc)

        xc_bf = xc.astype(BF16)
        xm_bf = xm.astype(BF16)
        qk, vv, cat = [], [], []
        for h in range(H):
            hs = slice(h * dh, (h + 1) * dh)
            qk_h = jnp.dot(xc_bf[:, hs], wqk_ref[h], preferred_element_type=F32)
            v_h = jnp.dot(xm_bf[:, hs], wv_ref[h], preferred_element_type=F32)
            qk.append(qk_h)
            vv.append(v_h)
            cat += [qk_h.astype(BF16), v_h.astype(BF16)]
        cat = jnp.concatenate(cat, axis=1)

        g_row = lax.dot_general(wift_ref[...], cat, _NT, preferred_element_type=F32) + bifr_ref[...]
        a_row = _gate_transform(g_row, lax.broadcasted_iota(jnp.int32, (2 * H, L), 0))
        b_row = jnp.dot(a_row, tri_t, preferred_element_type=F32, precision=lax.Precision.HIGHEST)
        c_row = jnp.concatenate([a_row[0:H] - b_row[H:2 * H], jnp.zeros((SUBLANES - H, L), F32)], axis=0)
        return dict(xc=xc, qk=qk, vv=vv, a_row=a_row, b_row=b_row, c_col=jnp.transpose(c_row))

    def head(sq, h, f):
        hs = slice(h * dh, (h + 1) * dh)
        qt_bf = jnp.transpose(f["qk"][h][:, :dh]).astype(BF16)
        k_bf = (f["qk"][h][:, dh:] * (dh ** -0.5)).astype(BF16)
        v_aug_t = jnp.concatenate([jnp.transpose(f["vv"][h]), ones_rows], axis=0)
        c_c = f["c_col"][:, h:h + 1]
        b_r = f["b_row"][H + h:H + h + 1, :]
        i_r = f["a_row"][h:h + 1, :]
        m_prev = m_ref[sq, h, 0:1, 0:1]

        log_d = jnp.where(causal_t, c_c + b_r, -jnp.inf)
        log_inter = b_r + m_prev
        m_t = jnp.maximum(log_inter, jnp.max(log_d, axis=0, keepdims=True))
        w_intra = jnp.exp(log_d - m_t)
        w_inter = jnp.exp(log_inter - m_t)
        s_t = jnp.dot(k_bf, qt_bf, preferred_element_type=F32) * w_intra
        ct = ct_ref[sq, h]
        comb = (jnp.dot(v_aug_t.astype(BF16), s_t.astype(BF16), preferred_element_type=F32)
                + w_inter * jnp.dot(ct.astype(BF16), qt_bf, preferred_element_type=F32))
        hh_t = comb[:dh] / jnp.maximum(jnp.abs(comb[dh:dh + 1]), jnp.exp(-m_t))

        b_last = b_r[:, L - 1:L]
        log_g = b_last - b_r + i_r
        m_new = jnp.maximum(b_last + m_prev, jnp.max(log_g, axis=1, keepdims=True))
        w_g = jnp.exp(log_g - m_new)
        decay = jnp.exp(b_last + m_prev - m_new)
        upd = jnp.dot((v_aug_t * w_g).astype(BF16), k_bf, preferred_element_type=F32)
        ct_ref[sq, h] = decay * ct + upd
        m_ref[sq, h] = jnp.broadcast_to(m_new, (SUBLANES, LANES))

        mu = jnp.mean(hh_t, axis=0, keepdims=True)
        dlt = hh_t - mu
        var = jnp.mean(dlt * dlt, axis=0, keepdims=True)
        hn = jnp.transpose(dlt * lax.rsqrt(var + EPS)) * mn_ref[:, hs]
        zh = z_ref[sq, :, hs]
        return ((hn + ms_ref[:, hs] * f["xc"][:, hs]) * (zh * _sigmoid(zh))).astype(BF16)

    fronts = [front(sq) for sq in range(MLSTM_SEQS)]
    outs = [[] for _ in range(MLSTM_SEQS)]
    for h in range(H):
        for sq in range(MLSTM_SEQS):
            outs[sq].append(head(sq, h, fronts[sq]))
    for sq in range(MLSTM_SEQS):
        o_ref[sq] = jnp.concatenate(outs[sq], axis=1)


def _mlstm(xm, z, conv_w, conv_b, wqk, wv, wif_t, bif_r, mnorm, mskip):
    B, S, W = xm.shape
    L = MLSTM_CHUNK
    NS = MLSTM_SEQS
    assert B % NS == 0 and S % L == 0
    tok = pl.BlockSpec((NS, L, W), lambda b, j: (b, j, 0))
    return pl.pallas_call(
        _mlstm_kernel,
        out_shape=jax.ShapeDtypeStruct((B, S, W), BF16),
        grid=(B // NS, S // L),
        in_specs=[tok, tok,
                  _resident(conv_w.shape), _resident(conv_b.shape), _resident(wqk.shape), _resident(wv.shape),
                  _resident(wif_t.shape), _resident(bif_r.shape),
                  _resident(mnorm.shape), _resident(mskip.shape)],
        out_specs=tok,
        scratch_shapes=[pltpu.VMEM((NS, MLSTM_HEADS, MLSTM_STATE_ROWS, MLSTM_HEAD_DIM), F32),
                        pltpu.VMEM((NS, MLSTM_HEADS, SUBLANES, LANES), F32),
                        pltpu.VMEM((NS, SUBLANES, W), F32)],
        compiler_params=pltpu.CompilerParams(dimension_semantics=("arbitrary", "arbitrary"),
                                             vmem_limit_bytes=VMEM_LIMIT_BYTES),
        name="mlstm",
    )(xm, z, conv_w, conv_b, wqk, wv, wif_t, bif_r, mnorm, mskip)


def _moba_kernel(q_ref, k_ref, v_ref, km_ref, o_ref, lmin_ref, kaug_ref, vt_ref, qaug_ref, kn_ref, *, bounded):
    BS = MOBA_BLOCK
    NP = MOBA_PAIRS
    S = k_ref.shape[1]
    nb = km_ref.shape[1]
    hd = ATTN_HEAD_DIM
    i = pl.program_id(2)

    @pl.when(i == 0)
    def _():
        lane_s = lax.broadcasted_iota(jnp.int32, (S, LANES), 1)
        blk_s = lax.shift_right_logical(lax.broadcasted_iota(jnp.int32, (S, LANES), 0), BS.bit_length() - 1)
        k_tail = ((lane_s == blk_s) | (lane_s == nb)).astype(BF16)
        v_tail = (lax.broadcasted_iota(jnp.int32, (VT_ROWS - hd, S), 0) == 0).astype(BF16)
        for p in range(NP):
            ps = slice(p * LANES, (p + 1) * LANES)
            kaug_ref[p, :, 0:LANES] = k_ref[0, :, ps]
            kaug_ref[p, :, LANES:] = k_tail
            v_t = jnp.transpose(v_ref[0, :, ps].astype(F32)).astype(BF16)
            for hh in range(2):
                vt_ref[p, hh * VT_ROWS:hh * VT_ROWS + hd, :] = v_t[hh * hd:(hh + 1) * hd]
                vt_ref[p, hh * VT_ROWS + hd:(hh + 1) * VT_ROWS, :] = v_tail
            if bounded:
                kf = k_ref[0, :, ps].astype(F32)
                head_of_dim = lax.shift_right_logical(lax.broadcasted_iota(jnp.int32, (LANES, LANES), 0),
                                                      hd.bit_length() - 1)
                ind = (head_of_dim == lax.broadcasted_iota(jnp.int32, (LANES, LANES), 1)).astype(BF16)
                ksq = jnp.dot((kf * kf).astype(BF16), ind, preferred_element_type=F32)
                kn_ref[p] = jnp.broadcast_to(jnp.max(ksq, axis=0, keepdims=True), kn_ref.shape[1:])

            q_t = jnp.transpose(q_ref[0, :, ps].astype(F32))
            q_rep = jnp.concatenate([q_t[:, (c // 2) * BS:(c // 2 + 1) * BS] for c in range(2 * nb)], axis=1)
            col = lax.broadcasted_iota(jnp.int32, (LANES, 2 * S), 1)
            dim = lax.broadcasted_iota(jnp.int32, (LANES, 2 * S), 0)
            head_of_col = lax.shift_right_logical(col, BS.bit_length() - 1) & 1
            qq = jnp.where(lax.shift_right_logical(dim, hd.bit_length() - 1) == head_of_col, q_rep, 0.0)
            qq_bf = qq.astype(BF16)

            km = km_ref[0, :, ps]
            km_hi = km.astype(BF16)
            km_lo = (km - km_hi.astype(F32)).astype(BF16)
            gate = (jnp.dot(km_hi, qq_bf, preferred_element_type=F32)
                    + jnp.dot(km_lo, qq_bf, preferred_element_type=F32))
            blk_t = lax.broadcasted_iota(jnp.int32, (nb, 2 * S), 0)
            q_blk = lax.shift_right_logical(lax.broadcasted_iota(jnp.int32, (nb, 2 * S), 1), BS.bit_length())
            cur = jnp.where(blk_t < q_blk, gate, -jnp.inf)
            bias_t = jnp.full((nb, 2 * S), MASK_VALUE, F32)
            for _ in range(MOBA_TOPK):
                mx = jnp.max(cur, axis=0, keepdims=True)
                first = jnp.min(jnp.where(cur == mx, blk_t, nb), axis=0, keepdims=True)
                pick = (blk_t == first) & (mx > -jnp.inf)
                bias_t = jnp.where(pick, 0.0, bias_t)
                cur = jnp.where(pick, -jnp.inf, cur)
            bias_t = jnp.where(blk_t == q_blk, 0.0, bias_t)

            pad_rows = lax.broadcasted_iota(jnp.int32, (LANES - nb, 2 * S), 0)
            if bounded:
                qsq = jnp.sum(qq * qq, axis=0, keepdims=True)
                ksq_max = jnp.where(head_of_col[0:1] == 0, kn_ref[p, 0:1, 0:1], kn_ref[p, 0:1, 1:2])
                bound = jnp.sqrt(qsq * ksq_max) * MOBA_BOUND_SLACK
                tail = jnp.where(pad_rows == 0, -bound, 0.0)
            else:
                tail = jnp.zeros((LANES - nb, 2 * S), F32)
            qaug_ref[p] = jnp.concatenate([qq_bf, bias_t.astype(BF16), tail.astype(BF16)], axis=0)

    future_key = (lax.broadcasted_iota(jnp.int32, (BS, 2 * BS), 0)
                  > (lax.broadcasted_iota(jnp.int32, (BS, 2 * BS), 1) & (BS - 1)))

    for c in range(nb):
        @pl.when(i == c)
        def _():
            def scores(j):
                return [jnp.dot(kaug_ref[p, j * BS:(j + 1) * BS, :], qaug_ref[p, :, 2 * c * BS:2 * (c + 1) * BS],
                                preferred_element_type=F32) for p in range(NP)]

            m = [None] * NP
            pv = [[None, None] for _ in range(NP)]
            order = list(range(c, -1, -1))
            pending = [scores(j) for j in order[:MOBA_LOOKAHEAD]]
            for idx, j in enumerate(order):
                s_all = pending.pop(0)
                if idx + MOBA_LOOKAHEAD < len(order):
                    pending.append(scores(order[idx + MOBA_LOOKAHEAD]))
                for p in range(NP):
                    s = s_all[p]
                    if j == c:
                        s = jnp.where(future_key, MASK_VALUE, s)
                    alpha = None
                    if not bounded:
                        cm = jnp.max(s, axis=0, keepdims=True)
                        if m[p] is not None:
                            cm = jnp.maximum(m[p], cm)
                            alpha = jnp.exp(m[p] - cm)
                        m[p] = cm
                        s = s - cm
                    p_bf = jnp.exp(s).astype(BF16)
                    for hh in range(2):
                        d = jnp.dot(vt_ref[p, hh * VT_ROWS:(hh + 1) * VT_ROWS, j * BS:(j + 1) * BS],
                                    p_bf[:, hh * BS:(hh + 1) * BS], preferred_element_type=F32)
                        if pv[p][hh] is None:
                            pv[p][hh] = d
                        elif alpha is None:
                            pv[p][hh] = pv[p][hh] + d
                        else:
                            pv[p][hh] = alpha[:, hh * BS:(hh + 1) * BS] * pv[p][hh] + d
            outs, lmin = [], None
            for p in range(NP):
                for hh in range(2):
                    denom = pv[p][hh][hd:hd + 1]
                    outs.append(pv[p][hh][0:hd] / denom)
                    dmin = jnp.min(denom, axis=1, keepdims=True)
                    lmin = dmin if lmin is None else jnp.minimum(lmin, dmin)
            o_ref[0] = jnp.transpose(jnp.concatenate(outs, axis=0))
            lmin_ref[0] = jnp.broadcast_to(lmin, lmin_ref.shape[1:])


def _moba_call(q, k, v, kmean, bounded):
    B, S, W = q.shape
    BS = MOBA_BLOCK
    NP = MOBA_PAIRS
    nb = S // BS
    ngroups = W // (NP * LANES)
    assert nb < LANES - 1 and W % (NP * LANES) == 0
    qspec = pl.BlockSpec((1, BS, NP * LANES), lambda b, g, i: (b, i, g))
    kvspec = pl.BlockSpec((1, S, NP * LANES), lambda b, g, i: (b, 0, g))
    return pl.pallas_call(
        functools.partial(_moba_kernel, bounded=bounded),
        out_shape=(jax.ShapeDtypeStruct((B, S, W), F32),
                   jax.ShapeDtypeStruct((B * ngroups * nb, SUBLANES, LANES), F32)),
        grid=(B, ngroups, nb),
        in_specs=[kvspec, kvspec, kvspec, pl.BlockSpec((1, nb, NP * LANES), lambda b, g, i: (b, 0, g))],
        out_specs=(qspec, pl.BlockSpec((1, SUBLANES, LANES), lambda b, g, i: ((b * ngroups + g) * nb + i, 0, 0))),
        scratch_shapes=[pltpu.VMEM((NP, S, 2 * LANES), BF16),
                        pltpu.VMEM((NP, 2 * VT_ROWS, S), BF16),
                        pltpu.VMEM((NP, 2 * LANES, 2 * S), BF16),
                        pltpu.VMEM((NP, SUBLANES, LANES), F32)],
        compiler_params=pltpu.CompilerParams(dimension_semantics=("arbitrary",) * 3,
                                             vmem_limit_bytes=VMEM_LIMIT_BYTES),
        name="moba_bounded" if bounded else "moba_exact",
    )(q, k, v, kmean)


def _moba(q, k, v, kmean):
    o_fast, lmin = _moba_call(q, k, v, kmean, True)
    ok = jnp.all(lmin[:, 0, 0] > MOBA_MIN_DENOM)
    return lax.cond(ok, lambda: o_fast, lambda: _moba_call(q, k, v, kmean, False)[0])


def _out_ffn2_kernel(x1_ref, oa_ref, om_ref, g2_ref, sh_ref, sc_ref, g3_ref, an_ref, wo_ref,
                     nw_ref, wg_ref, wu_ref, wd_ref, fn_ref, o_ref, a_ref):
    oa = _rmsnorm(oa_ref[...], an_ref[...]).astype(BF16)
    mix = (jnp.dot(oa, wo_ref[0:ATTN_WIDTH, :], preferred_element_type=F32)
           + jnp.dot(om_ref[...], wo_ref[ATTN_WIDTH:, :], preferred_element_type=F32))
    x2 = x1_ref[...] + (1.0 + g2_ref[0, 0]) * mix
    h = _rms_mod(x2, nw_ref[...], sh_ref[0, 0], sc_ref[0, 0])
    y = _swiglu(h.astype(BF16), wg_ref, wu_ref, wd_ref, a_ref)
    x3 = x2 + (0.5 * (1.0 + g3_ref[0, 0])) * y
    o_ref[...] = _rmsnorm(x3, fn_ref[...])


def _out_ffn2(x1, oa, om, mod4, attn_norm, w_out, norm_w, wg, wu, wd, final_norm, seq):
    T = x1.shape[0]
    tm = FFN_TOKENS
    tps = seq // tm
    tok = lambda w: pl.BlockSpec((tm, w), lambda i: (i, 0))
    return pl.pallas_call(
        _out_ffn2_kernel,
        out_shape=jax.ShapeDtypeStruct((T, D_MODEL), F32),
        grid=(T // tm,),
        in_specs=[tok(D_MODEL), tok(ATTN_WIDTH), tok(MLSTM_WIDTH),
                  _mod_spec(5, tps), _mod_spec(6, tps), _mod_spec(7, tps), _mod_spec(8, tps),
                  _resident((1, ATTN_WIDTH)), _resident((D_MODEL, D_MODEL)), _resident((1, D_MODEL)),
                  _resident((D_MODEL, D_FF)), _resident((D_MODEL, D_FF)), _resident((D_FF, D_MODEL)),
                  _resident((1, D_MODEL))],
        out_specs=tok(D_MODEL),
        scratch_shapes=[pltpu.VMEM((tm, D_FF), BF16)],
        compiler_params=pltpu.CompilerParams(dimension_semantics=("arbitrary",),
                                             vmem_limit_bytes=VMEM_LIMIT_BYTES),
        name="out_ffn2",
    )(x1, oa, om, mod4, mod4, mod4, mod4, attn_norm, w_out, norm_w, wg, wu, wd, final_norm)


def kernel(x, c, w_ada, b_ada, ffn1_norm, ffn1_w_gate, ffn1_w_up, ffn1_w_down, mix_norm, w_in, conv_w, conv_b,
           w_q_m, w_k_m, w_v_m, w_if, b_if, mlstm_norm, mlstm_skip, attn_norm, w_out,
           ffn2_norm, ffn2_w_gate, ffn2_w_up, ffn2_w_down, final_norm):
    B, S, D = x.shape
    T = B * S
    assert w_ada.shape[0] == 1, "only depth 1 is supported"
    bf = lambda a: a.astype(BF16)
    xf = x.reshape(T, D)
    for l in range(1):
        mod4 = _adaln(c, w_ada[l], b_ada[l][None, :]).reshape(N_MOD, B, 1, D)
        x1 = _ffn1(xf, mod4, ffn1_norm[l][None, :], bf(ffn1_w_gate[l]), bf(ffn1_w_up[l]), bf(ffn1_w_down[l]), S)
        q, k, v, xm, z, kmean = _proj(x1, mod4, mix_norm[l][None, :], bf(w_in[l]), S)
        o_m = _mlstm(xm.reshape(B, S, MLSTM_WIDTH), z.reshape(B, S, MLSTM_WIDTH),
                     conv_w[l], conv_b[l][None, :],
                     bf(jnp.concatenate([w_q_m[l], w_k_m[l]], axis=-1)), bf(w_v_m[l]),
                     bf(w_if[l].T), b_if[l][:, None],
                     mlstm_norm[l][None, :], mlstm_skip[l][None, :])
        o_a = _moba(q.reshape(B, S, ATTN_WIDTH), k.reshape(B, S, ATTN_WIDTH), v.reshape(B, S, ATTN_WIDTH),
                    kmean.reshape(B, S // MOBA_BLOCK, ATTN_WIDTH))
        xf = _out_ffn2(x1, o_a.reshape(T, ATTN_WIDTH), o_m.reshape(T, MLSTM_WIDTH), mod4,
                       attn_norm[l][None, :], bf(w_out[l]), ffn2_norm[l][None, :],
                       bf(ffn2_w_gate[l]), bf(ffn2_w_up[l]), bf(ffn2_w_down[l]), final_norm[None, :], S)
    return xf.reshape(B, S, D)
```

```python
import functools

import jax
import jax.numpy as jnp
from jax import lax
from jax.experimental import pallas as pl
from jax.experimental.pallas import tpu as pltpu

F32 = jnp.float32
BF16 = jnp.bfloat16

D_MODEL = 1024
ATTN_HEADS = 8
ATTN_WIDTH = 512
ATTN_HEAD_DIM = 64
MLSTM_HEADS = 4
MLSTM_WIDTH = 512
MLSTM_HEAD_DIM = 128
IN_COLS = 3 * ATTN_WIDTH + 2 * MLSTM_WIDTH
CONV_WIDTH = 4
MOBA_BLOCK = 256
MOBA_TOPK = 3
D_FF = 2816
N_MOD = 9
EPS = 1e-6

LANES = 128
SUBLANES = 8
VMEM_LIMIT_BYTES = 56 * 1024 * 1024

FFN_TOKENS = 512
MXU_WIDTH = 256
FFN_CHUNKS = ((0, 6 * MXU_WIDTH), (6 * MXU_WIDTH, D_FF))
MLSTM_CHUNK = 256
MLSTM_STATE_ROWS = MLSTM_HEAD_DIM + 16
MLSTM_SEQS = 2
MOBA_PAIRS = 2
MOBA_LOOKAHEAD = 2
MOBA_BOUND_SLACK = 1.0 + 2.0 ** -6
MOBA_MIN_DENOM = 1e-25
VT_ROWS = ATTN_HEAD_DIM + 16
MASK_VALUE = -1e30

_NT = (((1,), (1,)), ((), ()))
_TN = (((0,), (0,)), ((), ()))


def _sigmoid(x):
    return 1.0 / (1.0 + jnp.exp(-x))


def _rmsnorm(x, w):
    return x * lax.rsqrt(jnp.mean(x * x, axis=-1, keepdims=True) + EPS) * w


def _rms_mod(x, w, shift, scale):
    return _rmsnorm(x, w) * (1.0 + scale) + shift


def _swiglu(h_bf, wg_ref, wu_ref, wd_ref, a_ref):
    for lo, hi in FFN_CHUNKS:
        sl = slice(lo, hi)
        g = jnp.dot(h_bf, wg_ref[:, sl], preferred_element_type=F32)
        u = jnp.dot(h_bf, wu_ref[:, sl], preferred_element_type=F32)
        a_ref[:, sl] = (g * _sigmoid(g) * u).astype(BF16)
    return jnp.dot(a_ref[...], wd_ref[...], preferred_element_type=F32)


def _adaln_kernel(c_ref, w_ref, b_ref, o_ref):
    o_ref[0] = jnp.dot(c_ref[...].astype(BF16), w_ref[...].astype(BF16),
                       preferred_element_type=F32) + b_ref[...]


def _adaln(c, w_ada, b_ada):
    B = c.shape[0]
    return pl.pallas_call(
        _adaln_kernel,
        out_shape=jax.ShapeDtypeStruct((N_MOD, B, D_MODEL), F32),
        grid=(N_MOD,),
        in_specs=[
            pl.BlockSpec((B, D_MODEL), lambda j: (0, 0)),
            pl.BlockSpec((D_MODEL, D_MODEL), lambda j: (0, j)),
            pl.BlockSpec((1, D_MODEL), lambda j: (0, j)),
        ],
        out_specs=pl.BlockSpec((1, B, D_MODEL), lambda j: (j, 0, 0)),
        compiler_params=pltpu.CompilerParams(dimension_semantics=("arbitrary",),
                                             vmem_limit_bytes=VMEM_LIMIT_BYTES),
        name="adaln",
    )(c, w_ada, b_ada)


def _resident(shape):
    nd = len(shape)
    return pl.BlockSpec(shape, lambda *_: (0,) * nd, pipeline_mode=pl.Buffered(1))


def _mod_spec(k, tiles_per_seq):
    return pl.BlockSpec((1, 1, 1, D_MODEL), lambda i: (k, i // tiles_per_seq, 0, 0))


def _ffn1_kernel(x_ref, sh_ref, sc_ref, g_ref, nw_ref, wg_ref, wu_ref, wd_ref, o_ref, a_ref):
    x = x_ref[...]
    h = _rms_mod(x, nw_ref[...], sh_ref[0, 0], sc_ref[0, 0])
    y = _swiglu(h.astype(BF16), wg_ref, wu_ref, wd_ref, a_ref)
    o_ref[...] = x + (0.5 * (1.0 + g_ref[0, 0])) * y


def _ffn1(x2d, mod4, norm_w, wg, wu, wd, seq):
    T = x2d.shape[0]
    tm = FFN_TOKENS
    tps = seq // tm
    tok = pl.BlockSpec((tm, D_MODEL), lambda i: (i, 0))
    return pl.pallas_call(
        _ffn1_kernel,
        out_shape=jax.ShapeDtypeStruct((T, D_MODEL), F32),
        grid=(T // tm,),
        in_specs=[tok, _mod_spec(0, tps), _mod_spec(1, tps), _mod_spec(2, tps),
                  _resident((1, D_MODEL)),
                  _resident((D_MODEL, D_FF)), _resident((D_MODEL, D_FF)), _resident((D_FF, D_MODEL))],
        out_specs=tok,
        scratch_shapes=[pltpu.VMEM((tm, D_FF), BF16)],
        compiler_params=pltpu.CompilerParams(dimension_semantics=("arbitrary",),
                                             vmem_limit_bytes=VMEM_LIMIT_BYTES),
        name="ffn1",
    )(x2d, mod4, mod4, mod4, norm_w, wg, wu, wd)


def _proj_kernel(x_ref, sh_ref, sc_ref, nw_ref, win_ref,
                 q_ref, k_ref, v_ref, xm_ref, z_ref, km_ref):
    h = _rms_mod(x_ref[...], nw_ref[...], sh_ref[0, 0], sc_ref[0, 0]).astype(BF16)
    proj = jnp.dot(h, win_ref[...], preferred_element_type=F32)
    aw = ATTN_WIDTH
    q_ref[...] = (proj[:, 0:aw] * (ATTN_HEAD_DIM ** -0.5)).astype(BF16)
    k = proj[:, aw:2 * aw]
    k_ref[...] = k.astype(BF16)
    v_ref[...] = proj[:, 2 * aw:3 * aw].astype(BF16)
    xm_ref[...] = proj[:, 3 * aw:3 * aw + MLSTM_WIDTH]
    z_ref[...] = proj[:, 3 * aw + MLSTM_WIDTH:]
    for r in range(FFN_TOKENS // MOBA_BLOCK):
        km_ref[r] = jnp.mean(k[r * MOBA_BLOCK:(r + 1) * MOBA_BLOCK], axis=0, keepdims=True)


def _proj(x1, mod4, norm_w, w_in, seq):
    T = x1.shape[0]
    tm = FFN_TOKENS
    tps = seq // tm
    nb = tm // MOBA_BLOCK
    tok = lambda w: pl.BlockSpec((tm, w), lambda i: (i, 0))
    return pl.pallas_call(
        _proj_kernel,
        out_shape=(jax.ShapeDtypeStruct((T, ATTN_WIDTH), BF16),
                   jax.ShapeDtypeStruct((T, ATTN_WIDTH), BF16),
                   jax.ShapeDtypeStruct((T, ATTN_WIDTH), BF16),
                   jax.ShapeDtypeStruct((T, MLSTM_WIDTH), F32),
                   jax.ShapeDtypeStruct((T, MLSTM_WIDTH), F32),
                   jax.ShapeDtypeStruct((T // MOBA_BLOCK, 1, ATTN_WIDTH), F32)),
        grid=(T // tm,),
        in_specs=[tok(D_MODEL), _mod_spec(3, tps), _mod_spec(4, tps),
                  _resident((1, D_MODEL)), _resident((D_MODEL, IN_COLS))],
        out_specs=(tok(ATTN_WIDTH), tok(ATTN_WIDTH), tok(ATTN_WIDTH),
                   tok(MLSTM_WIDTH), tok(MLSTM_WIDTH),
                   pl.BlockSpec((nb, 1, ATTN_WIDTH), lambda i: (i, 0, 0))),
        compiler_params=pltpu.CompilerParams(dimension_semantics=("arbitrary",),
                                             vmem_limit_bytes=VMEM_LIMIT_BYTES),
        name="proj",
    )(x1, mod4, mod4, norm_w, w_in)


def _gate_transform(g, gate_idx):
    log_sig = jnp.minimum(g, 0.0) - jnp.log1p(jnp.exp(-jnp.abs(g)))
    return jnp.where(gate_idx >= MLSTM_HEADS, log_sig, g)


def _mlstm_kernel(xm_ref, z_ref, cw_ref, cb_ref, wqk_ref, wv_ref, wift_ref,
                  bifr_ref, mn_ref, ms_ref, o_ref, ct_ref, m_ref, carry_ref):
    L = MLSTM_CHUNK
    dh = MLSTM_HEAD_DIM
    H = MLSTM_HEADS

    @pl.when(pl.program_id(1) == 0)
    def _():
        ct_ref[...] = jnp.zeros_like(ct_ref)
        m_ref[...] = jnp.zeros_like(m_ref)
        carry_ref[...] = jnp.zeros_like(carry_ref)

    cw = cw_ref[...]
    row8 = lax.broadcasted_iota(jnp.int32, (SUBLANES, MLSTM_WIDTH), 0)
    s_idx = lax.broadcasted_iota(jnp.int32, (L, L), 0)
    t_idx = lax.broadcasted_iota(jnp.int32, (L, L), 1)
    causal_t = s_idx <= t_idx
    tri_t = causal_t.astype(F32)
    ones_rows = (lax.broadcasted_iota(jnp.int32, (MLSTM_STATE_ROWS - dh, L), 0) == 0).astype(F32)

    def front(sq):
        xm = xm_ref[sq]
        carry = carry_ref[sq]
        acc = xm * cw[CONV_WIDTH - 1:CONV_WIDTH] + cb_ref[...]
        for k in range(1, CONV_WIDTH):
            rolled = pltpu.roll(xm, k, 0)
            top = jnp.where(row8 < k, pltpu.roll(carry, k, 0), rolled[0:SUBLANES])
            shifted = jnp.concatenate([top, rolled[SUBLANES:]], axis=0)
            acc = acc + shifted * cw[CONV_WIDTH - 1 - k:CONV_WIDTH - k]
        carry_ref[sq] = xm[L - SUBLANES:L]
        xc = acc * _sigmoid(acc)

        xc_bf = xc.astype(BF16)
        xm_bf = xm.astype(BF16)
        qk, vv, cat = [], [], []
        for h in range(H):
            hs = slice(h * dh, (h + 1) * dh)
            qk_h = jnp.dot(xc_bf[:, hs], wqk_ref[h], preferred_element_type=F32)
            v_h = jnp.dot(xm_bf[:, hs], wv_ref[h], preferred_element_type=F32)
            qk.append(qk_h)
            vv.append(v_h)
            cat += [qk_h.astype(BF16), v_h.astype(BF16)]
        cat = jnp.concatenate(cat, axis=1)

        g_row = lax.dot_general(wift_ref[...], cat, _NT, preferred_element_type=F32) + bifr_ref[...]
        a_row = _gate_transform(g_row, lax.broadcasted_iota(jnp.int32, (2 * H, L), 0))
        b_row = jnp.dot(a_row, tri_t, preferred_element_type=F32, precision=lax.Precision.HIGHEST)
        c_row = jnp.concatenate([a_row[0:H] - b_row[H:2 * H], jnp.zeros((SUBLANES - H, L), F32)], axis=0)
        return dict(xc=xc, qk=qk, vv=vv, a_row=a_row, b_row=b_row, c_col=jnp.transpose(c_row))

    def head(sq, h, f):
        hs = slice(h * dh, (h + 1) * dh)
        qt_bf = jnp.transpose(f["qk"][h][:, :dh]).astype(BF16)
        k_bf = (f["qk"][h][:, dh:] * (dh ** -0.5)).astype(BF16)
        v_aug_t = jnp.concatenate([jnp.transpose(f["vv"][h]), ones_rows], axis=0)
        c_c = f["c_col"][:, h:h + 1]
        b_r = f["b_row"][H + h:H + h + 1, :]
        i_r = f["a_row"][h:h + 1, :]
        m_prev = m_ref[sq, h, 0:1, 0:1]

        log_d = jnp.where(causal_t, c_c + b_r, -jnp.inf)
        log_inter = b_r + m_prev
        m_t = jnp.maximum(log_inter, jnp.max(log_d, axis=0, keepdims=True))
        w_intra = jnp.exp(log_d - m_t)
        w_inter = jnp.exp(log_inter - m_t)
        s_t = jnp.dot(k_bf, qt_bf, preferred_element_type=F32) * w_intra
        ct = ct_ref[sq, h]
        comb = (jnp.dot(v_aug_t.astype(BF16), s_t.astype(BF16), preferred_element_type=F32)
                + w_inter * jnp.dot(ct.astype(BF16), qt_bf, preferred_element_type=F32))
        hh_t = comb[:dh] / jnp.maximum(jnp.abs(comb[dh:dh + 1]), jnp.exp(-m_t))

        b_last = b_r[:, L - 1:L]
        log_g = b_last - b_r + i_r
        m_new = jnp.maximum(b_last + m_prev, jnp.max(log_g, axis=1, keepdims=True))
        w_g = jnp.exp(log_g - m_new)
        decay = jnp.exp(b_last + m_prev - m_new)
        upd = jnp.dot((v_aug_t * w_g).astype(BF16), k_bf, preferred_element_type=F32)
        ct_ref[sq, h] = decay * ct + upd
        m_ref[sq, h] = jnp.broadcast_to(m_new, (SUBLANES, LANES))

        mu = jnp.mean(hh_t, axis=0, keepdims=True)
        dlt = hh_t - mu
        var = jnp.mean(dlt * dlt, axis=0, keepdims=True)
        hn = jnp.transpose(dlt * lax.rsqrt(var + EPS)) * mn_ref[:, hs]
        zh = z_ref[sq, :, hs]
        return ((hn + ms_ref[:, hs] * f["xc"][:, hs]) * (zh * _sigmoid(zh))).astype(BF16)

    fronts = [front(sq) for sq in range(MLSTM_SEQS)]
    outs = [[] for _ in range(MLSTM_SEQS)]
    for h in range(H):
        for sq in range(MLSTM_SEQS):
            outs[sq].append(head(sq, h, fronts[sq]))
    for sq in range(MLSTM_SEQS):
        o_ref[sq] = jnp.concatenate(outs[sq], axis=1)


def _mlstm(xm, z, conv_w, conv_b, wqk, wv, wif_t, bif_r, mnorm, mskip):
    B, S, W = xm.shape
    L = MLSTM_CHUNK
    NS = MLSTM_SEQS
    assert B % NS == 0 and S % L == 0
    tok = pl.BlockSpec((NS, L, W), lambda b, j: (b, j, 0))
    return pl.pallas_call(
        _mlstm_kernel,
        out_shape=jax.ShapeDtypeStruct((B, S, W), BF16),
        grid=(B // NS, S // L),
        in_specs=[tok, tok,
                  _resident(conv_w.shape), _resident(conv_b.shape), _resident(wqk.shape), _resident(wv.shape),
                  _resident(wif_t.shape), _resident(bif_r.shape),
                  _resident(mnorm.shape), _resident(mskip.shape)],
        out_specs=tok,
        scratch_shapes=[pltpu.VMEM((NS, MLSTM_HEADS, MLSTM_STATE_ROWS, MLSTM_HEAD_DIM), F32),
                        pltpu.VMEM((NS, MLSTM_HEADS, SUBLANES, LANES), F32),
                        pltpu.VMEM((NS, SUBLANES, W), F32)],
        compiler_params=pltpu.CompilerParams(dimension_semantics=("arbitrary", "arbitrary"),
                                             vmem_limit_bytes=VMEM_LIMIT_BYTES),
        name="mlstm",
    )(xm, z, conv_w, conv_b, wqk, wv, wif_t, bif_r, mnorm, mskip)


def _moba_kernel(q_ref, k_ref, v_ref, km_ref, o_ref, lmin_ref, kaug_ref, vt_ref, qaug_ref, kn_ref, *, bounded):
    BS = MOBA_BLOCK
    NP = MOBA_PAIRS
    S = k_ref.shape[1]
    nb = km_ref.shape[1]
    hd = ATTN_HEAD_DIM
    i = pl.program_id(2)

    @pl.when(i == 0)
    def _():
        lane_s = lax.broadcasted_iota(jnp.int32, (S, LANES), 1)
        blk_s = lax.shift_right_logical(lax.broadcasted_iota(jnp.int32, (S, LANES), 0), BS.bit_length() - 1)
        k_tail = ((lane_s == blk_s) | (lane_s == nb)).astype(BF16)
        v_tail = (lax.broadcasted_iota(jnp.int32, (VT_ROWS - hd, S), 0) == 0).astype(BF16)
        for p in range(NP):
            ps = slice(p * LANES, (p + 1) * LANES)
            kaug_ref[p, :, 0:LANES] = k_ref[0, :, ps]
            kaug_ref[p, :, LANES:] = k_tail
            v_t = jnp.transpose(v_ref[0, :, ps].astype(F32)).astype(BF16)
            for hh in range(2):
                vt_ref[p, hh * VT_ROWS:hh * VT_ROWS + hd, :] = v_t[hh * hd:(hh + 1) * hd]
                vt_ref[p, hh * VT_ROWS + hd:(hh + 1) * VT_ROWS, :] = v_tail
            if bounded:
                kf = k_ref[0, :, ps].astype(F32)
                head_of_dim = lax.shift_right_logical(lax.broadcasted_iota(jnp.int32, (LANES, LANES), 0),
                                                      hd.bit_length() - 1)
                ind = (head_of_dim == lax.broadcasted_iota(jnp.int32, (LANES, LANES), 1)).astype(BF16)
                ksq = jnp.dot((kf * kf).astype(BF16), ind, preferred_element_type=F32)
                kn_ref[p] = jnp.broadcast_to(jnp.max(ksq, axis=0, keepdims=True), kn_ref.shape[1:])

            km = km_ref[0, :, ps]
            km_hi = km.astype(BF16)
            km_lo = (km - km_hi.astype(F32)).astype(BF16)
            dim_t = lax.broadcasted_iota(jnp.int32, (LANES, 2 * BS), 0)
            col_t = lax.broadcasted_iota(jnp.int32, (LANES, 2 * BS), 1)
            in_head = (lax.shift_right_logical(dim_t, hd.bit_length() - 1)
                       == lax.shift_right_logical(col_t, BS.bit_length() - 1))
            blk_t = lax.broadcasted_iota(jnp.int32, (nb, 2 * BS), 0)
            pad_rows = lax.broadcasted_iota(jnp.int32, (LANES - nb, 2 * BS), 0)
            if bounded:
                ksq_max = jnp.where(col_t[0:1] < BS, kn_ref[p, 0:1, 0:1], kn_ref[p, 0:1, 1:2])
            for j in range(nb):
                q_j = jnp.transpose(q_ref[0, j * BS:(j + 1) * BS, ps].astype(F32))
                qq = jnp.where(in_head, jnp.concatenate([q_j, q_j], axis=1), 0.0)
                qq_bf = qq.astype(BF16)
                gate = (jnp.dot(km_hi, qq_bf, preferred_element_type=F32)
                        + jnp.dot(km_lo, qq_bf, preferred_element_type=F32))
                cur = jnp.where(blk_t < j, gate, -jnp.inf)
                bias_t = jnp.full((nb, 2 * BS), MASK_VALUE, F32)
                for _ in range(MOBA_TOPK):
                    mx = jnp.max(cur, axis=0, keepdims=True)
                    first = jnp.min(jnp.where(cur == mx, blk_t, nb), axis=0, keepdims=True)
                    pick = (blk_t == first) & (mx > -jnp.inf)
                    bias_t = jnp.where(pick, 0.0, bias_t)
                    cur = jnp.where(pick, -jnp.inf, cur)
                bias_t = jnp.where(blk_t == j, 0.0, bias_t)
                if bounded:
                    bound = jnp.sqrt(jnp.sum(qq * qq, axis=0, keepdims=True) * ksq_max) * MOBA_BOUND_SLACK
                    tail = jnp.where(pad_rows == 0, -bound, 0.0)
                else:
                    tail = jnp.zeros((LANES - nb, 2 * BS), F32)
                qaug_ref[p, :, 2 * j * BS:2 * (j + 1) * BS] = jnp.concatenate(
                    [qq_bf, bias_t.astype(BF16), tail.astype(BF16)], axis=0)

    future_key = (lax.broadcasted_iota(jnp.int32, (BS, 2 * BS), 0)
                  > (lax.broadcasted_iota(jnp.int32, (BS, 2 * BS), 1) & (BS - 1)))

    for c in range(nb):
        @pl.when(i == c)
        def _():
            def scores(j):
                return [jnp.dot(kaug_ref[p, j * BS:(j + 1) * BS, :], qaug_ref[p, :, 2 * c * BS:2 * (c + 1) * BS],
                                preferred_element_type=F32) for p in range(NP)]

            m = [None] * NP
            pv = [[None, None] for _ in range(NP)]
            order = list(range(c, -1, -1))
            pending = [scores(j) for j in order[:MOBA_LOOKAHEAD]]
            for idx, j in enumerate(order):
                s_all = pending.pop(0)
                if idx + MOBA_LOOKAHEAD < len(order):
                    pending.append(scores(order[idx + MOBA_LOOKAHEAD]))
                for p in range(NP):
                    s = s_all[p]
                    if j == c:
                        s = jnp.where(future_key, MASK_VALUE, s)
                    alpha = None
                    if not bounded:
                        cm = jnp.max(s, axis=0, keepdims=True)
                        if m[p] is not None:
                            cm = jnp.maximum(m[p], cm)
                            alpha = jnp.exp(m[p] - cm)
                        m[p] = cm
                        s = s - cm
                    p_bf = jnp.exp(s).astype(BF16)
                    for hh in range(2):
                        d = jnp.dot(vt_ref[p, hh * VT_ROWS:(hh + 1) * VT_ROWS, j * BS:(j + 1) * BS],
                                    p_bf[:, hh * BS:(hh + 1) * BS], preferred_element_type=F32)
                        if pv[p][hh] is None:
                            pv[p][hh] = d
                        elif alpha is None:
                            pv[p][hh] = pv[p][hh] + d
                        else:
                            pv[p][hh] = alpha[:, hh * BS:(hh + 1) * BS] * pv[p][hh] + d
            outs, lmin = [], None
            for p in range(NP):
                for hh in range(2):
                    denom = pv[p][hh][hd:hd + 1]
                    outs.append(pv[p][hh][0:hd] / denom)
                    dmin = jnp.min(denom, axis=1, keepdims=True)
                    lmin = dmin if lmin is None else jnp.minimum(lmin, dmin)
            o_ref[0] = jnp.transpose(jnp.concatenate(outs, axis=0))
            lmin_ref[0] = jnp.broadcast_to(lmin, lmin_ref.shape[1:])


def _moba_call(q, k, v, kmean, bounded):
    B, S, W = q.shape
    BS = MOBA_BLOCK
    NP = MOBA_PAIRS
    nb = S // BS
    ngroups = W // (NP * LANES)
    assert nb < LANES - 1 and W % (NP * LANES) == 0
    qspec = pl.BlockSpec((1, BS, NP * LANES), lambda b, g, i: (b, i, g))
    kvspec = pl.BlockSpec((1, S, NP * LANES), lambda b, g, i: (b, 0, g))
    return pl.pallas_call(
        functools.partial(_moba_kernel, bounded=bounded),
        out_shape=(jax.ShapeDtypeStruct((B, S, W), F32),
                   jax.ShapeDtypeStruct((B * ngroups * nb, SUBLANES, LANES), F32)),
        grid=(B, ngroups, nb),
        in_specs=[kvspec, kvspec, kvspec, pl.BlockSpec((1, nb, NP * LANES), lambda b, g, i: (b, 0, g))],
        out_specs=(qspec, pl.BlockSpec((1, SUBLANES, LANES), lambda b, g, i: ((b * ngroups + g) * nb + i, 0, 0))),
        scratch_shapes=[pltpu.VMEM((NP, S, 2 * LANES), BF16),
                        pltpu.VMEM((NP, 2 * VT_ROWS, S), BF16),
                        pltpu.VMEM((NP, 2 * LANES, 2 * S), BF16),
                        pltpu.VMEM((NP, SUBLANES, LANES), F32)],
        compiler_params=pltpu.CompilerParams(dimension_semantics=("arbitrary",) * 3,
                                             vmem_limit_bytes=VMEM_LIMIT_BYTES),
        name="moba_bounded" if bounded else "moba_exact",
    )(q, k, v, kmean)


def _moba(q, k, v, kmean):
    o_fast, lmin = _moba_call(q, k, v, kmean, True)
    ok = jnp.all(lmin[:, 0, 0] > MOBA_MIN_DENOM)
    return lax.cond(ok, lambda: o_fast, lambda: _moba_call(q, k, v, kmean, False)[0])


def _out_ffn2_kernel(x1_ref, oa_ref, om_ref, g2_ref, sh_ref, sc_ref, g3_ref, an_ref, wo_ref,
                     nw_ref, wg_ref, wu_ref, wd_ref, fn_ref, o_ref, a_ref):
    oa = _rmsnorm(oa_ref[...], an_ref[...]).astype(BF16)
    mix = (jnp.dot(oa, wo_ref[0:ATTN_WIDTH, :], preferred_element_type=F32)
           + jnp.dot(om_ref[...], wo_ref[ATTN_WIDTH:, :], preferred_element_type=F32))
    x2 = x1_ref[...] + (1.0 + g2_ref[0, 0]) * mix
    h = _rms_mod(x2, nw_ref[...], sh_ref[0, 0], sc_ref[0, 0])
    y = _swiglu(h.astype(BF16), wg_ref, wu_ref, wd_ref, a_ref)
    x3 = x2 + (0.5 * (1.0 + g3_ref[0, 0])) * y
    o_ref[...] = _rmsnorm(x3, fn_ref[...])


def _out_ffn2(x1, oa, om, mod4, attn_norm, w_out, norm_w, wg, wu, wd, final_norm, seq):
    T = x1.shape[0]
    tm = FFN_TOKENS
    tps = seq // tm
    tok = lambda w: pl.BlockSpec((tm, w), lambda i: (i, 0))
    return pl.pallas_call(
        _out_ffn2_kernel,
        out_shape=jax.ShapeDtypeStruct((T, D_MODEL), F32),
        grid=(T // tm,),
        in_specs=[tok(D_MODEL), tok(ATTN_WIDTH), tok(MLSTM_WIDTH),
                  _mod_spec(5, tps), _mod_spec(6, tps), _mod_spec(7, tps), _mod_spec(8, tps),
                  _resident((1, ATTN_WIDTH)), _resident((D_MODEL, D_MODEL)), _resident((1, D_MODEL)),
                  _resident((D_MODEL, D_FF)), _resident((D_MODEL, D_FF)), _resident((D_FF, D_MODEL)),
                  _resident((1, D_MODEL))],
        out_specs=tok(D_MODEL),
        scratch_shapes=[pltpu.VMEM((tm, D_FF), BF16)],
        compiler_params=pltpu.CompilerParams(dimension_semantics=("arbitrary",),
                                             vmem_limit_bytes=VMEM_LIMIT_BYTES),
        name="out_ffn2",
    )(x1, oa, om, mod4, mod4, mod4, mod4, attn_norm, w_out, norm_w, wg, wu, wd, final_norm)


def kernel(x, c, w_ada, b_ada, ffn1_norm, ffn1_w_gate, ffn1_w_up, ffn1_w_down, mix_norm, w_in, conv_w, conv_b,
           w_q_m, w_k_m, w_v_m, w_if, b_if, mlstm_norm, mlstm_skip, attn_norm, w_out,
           ffn2_norm, ffn2_w_gate, ffn2_w_up, ffn2_w_down, final_norm):
    B, S, D = x.shape
    T = B * S
    assert w_ada.shape[0] == 1, "only depth 1 is supported"
    bf = lambda a: a.astype(BF16)
    xf = x.reshape(T, D)
    for l in range(1):
        mod4 = _adaln(c, w_ada[l], b_ada[l][None, :]).reshape(N_MOD, B, 1, D)
        x1 = _ffn1(xf, mod4, ffn1_norm[l][None, :], bf(ffn1_w_gate[l]), bf(ffn1_w_up[l]), bf(ffn1_w_down[l]), S)
        q, k, v, xm, z, kmean = _proj(x1, mod4, mix_norm[l][None, :], bf(w_in[l]), S)
        o_m = _mlstm(xm.reshape(B, S, MLSTM_WIDTH), z.reshape(B, S, MLSTM_WIDTH),
                     conv_w[l], conv_b[l][None, :],
                     bf(jnp.concatenate([w_q_m[l], w_k_m[l]], axis=-1)), bf(w_v_m[l]),
                     bf(w_if[l].T), b_if[l][:, None],
                     mlstm_norm[l][None, :], mlstm_skip[l][None, :])
        o_a = _moba(q.reshape(B, S, ATTN_WIDTH), k.reshape(B, S, ATTN_WIDTH), v.reshape(B, S, ATTN_WIDTH),
                    kmean.reshape(B, S // MOBA_BLOCK, ATTN_WIDTH))
        xf = _out_ffn2(x1, o_a.reshape(T, ATTN_WIDTH), o_m.reshape(T, MLSTM_WIDTH), mod4,
                       attn_norm[l][None, :], bf(w_out[l]), ffn2_norm[l][None, :],
                       bf(ffn2_w_gate[l]), bf(ffn2_w_up[l]), bf(ffn2_w_down[l]), final_norm[None, :], S)
    return xf.reshape(B, S, D)
```

```python
import functools

import jax
import jax.numpy as jnp
from jax import lax
from jax.experimental import pallas as pl
from jax.experimental.pallas import tpu as pltpu

F32 = jnp.float32
BF16 = jnp.bfloat16

D_MODEL = 1024
ATTN_HEADS = 8
ATTN_WIDTH = 512
ATTN_HEAD_DIM = 64
MLSTM_HEADS = 4
MLSTM_WIDTH = 512
MLSTM_HEAD_DIM = 128
IN_COLS = 3 * ATTN_WIDTH + 2 * MLSTM_WIDTH
CONV_WIDTH = 4
MOBA_BLOCK = 256
MOBA_TOPK = 3
D_FF = 2816
N_MOD = 9
EPS = 1e-6

LANES = 128
SUBLANES = 8
VMEM_LIMIT_BYTES = 56 * 1024 * 1024

FFN_TOKENS = 512
MXU_WIDTH = 256
FFN_CHUNKS = ((0, 6 * MXU_WIDTH), (6 * MXU_WIDTH, D_FF))
MLSTM_CHUNK = 256
MLSTM_STATE_ROWS = MLSTM_HEAD_DIM + 16
MLSTM_SEQS = 2
MOBA_PAIRS = 2
MOBA_LOOKAHEAD = 2
MOBA_BOUND_SLACK = 1.0 + 2.0 ** -6
MOBA_MIN_DENOM = 1e-25
VT_ROWS = ATTN_HEAD_DIM + 16
MASK_VALUE = -1e30

_NT = (((1,), (1,)), ((), ()))
_TN = (((0,), (0,)), ((), ()))


def _sigmoid(x):
    return 1.0 / (1.0 + jnp.exp(-x))


def _rmsnorm(x, w):
    return x * lax.rsqrt(jnp.mean(x * x, axis=-1, keepdims=True) + EPS) * w


def _rms_mod(x, w, shift, scale):
    return _rmsnorm(x, w) * (1.0 + scale) + shift


def _swiglu(h_bf, wg_ref, wu_ref, wd_ref, a_ref):
    for lo, hi in FFN_CHUNKS:
        sl = slice(lo, hi)
        g = jnp.dot(h_bf, wg_ref[:, sl], preferred_element_type=F32)
        u = jnp.dot(h_bf, wu_ref[:, sl], preferred_element_type=F32)
        a_ref[:, sl] = (g * _sigmoid(g) * u).astype(BF16)
    return jnp.dot(a_ref[...], wd_ref[...], preferred_element_type=F32)


def _adaln_kernel(c_ref, w_ref, b_ref, o_ref):
    o_ref[0] = jnp.dot(c_ref[...].astype(BF16), w_ref[...].astype(BF16),
                       preferred_element_type=F32) + b_ref[...]


def _adaln(c, w_ada, b_ada):
    B = c.shape[0]
    return pl.pallas_call(
        _adaln_kernel,
        out_shape=jax.ShapeDtypeStruct((N_MOD, B, D_MODEL), F32),
        grid=(N_MOD,),
        in_specs=[
            pl.BlockSpec((B, D_MODEL), lambda j: (0, 0)),
            pl.BlockSpec((D_MODEL, D_MODEL), lambda j: (0, j)),
            pl.BlockSpec((1, D_MODEL), lambda j: (0, j)),
        ],
        out_specs=pl.BlockSpec((1, B, D_MODEL), lambda j: (j, 0, 0)),
        compiler_params=pltpu.CompilerParams(dimension_semantics=("arbitrary",),
                                             vmem_limit_bytes=VMEM_LIMIT_BYTES),
        name="adaln",
    )(c, w_ada, b_ada)


def _resident(shape):
    nd = len(shape)
    return pl.BlockSpec(shape, lambda *_: (0,) * nd, pipeline_mode=pl.Buffered(1))


def _mod_spec(k, tiles_per_seq):
    return pl.BlockSpec((1, 1, 1, D_MODEL), lambda i: (k, i // tiles_per_seq, 0, 0))


def _ffn1_kernel(x_ref, sh_ref, sc_ref, g_ref, nw_ref, wg_ref, wu_ref, wd_ref, o_ref, a_ref):
    x = x_ref[...]
    h = _rms_mod(x, nw_ref[...], sh_ref[0, 0], sc_ref[0, 0])
    y = _swiglu(h.astype(BF16), wg_ref, wu_ref, wd_ref, a_ref)
    o_ref[...] = x + (0.5 * (1.0 + g_ref[0, 0])) * y


def _ffn1(x2d, mod4, norm_w, wg, wu, wd, seq):
    T = x2d.shape[0]
    tm = FFN_TOKENS
    tps = seq // tm
    tok = pl.BlockSpec((tm, D_MODEL), lambda i: (i, 0))
    return pl.pallas_call(
        _ffn1_kernel,
        out_shape=jax.ShapeDtypeStruct((T, D_MODEL), F32),
        grid=(T // tm,),
        in_specs=[tok, _mod_spec(0, tps), _mod_spec(1, tps), _mod_spec(2, tps),
                  _resident((1, D_MODEL)),
                  _resident((D_MODEL, D_FF)), _resident((D_MODEL, D_FF)), _resident((D_FF, D_MODEL))],
        out_specs=tok,
        scratch_shapes=[pltpu.VMEM((tm, D_FF), BF16)],
        compiler_params=pltpu.CompilerParams(dimension_semantics=("arbitrary",),
                                             vmem_limit_bytes=VMEM_LIMIT_BYTES),
        name="ffn1",
    )(x2d, mod4, mod4, mod4, norm_w, wg, wu, wd)


def _proj_kernel(x_ref, sh_ref, sc_ref, nw_ref, win_ref,
                 q_ref, k_ref, v_ref, xm_ref, z_ref, km_ref):
    h = _rms_mod(x_ref[...], nw_ref[...], sh_ref[0, 0], sc_ref[0, 0]).astype(BF16)
    proj = jnp.dot(h, win_ref[...], preferred_element_type=F32)
    aw = ATTN_WIDTH
    q_ref[...] = (proj[:, 0:aw] * (ATTN_HEAD_DIM ** -0.5)).astype(BF16)
    k = proj[:, aw:2 * aw]
    k_ref[...] = k.astype(BF16)
    v_ref[...] = proj[:, 2 * aw:3 * aw].astype(BF16)
    xm_ref[...] = proj[:, 3 * aw:3 * aw + MLSTM_WIDTH]
    z_ref[...] = proj[:, 3 * aw + MLSTM_WIDTH:]
    for r in range(FFN_TOKENS // MOBA_BLOCK):
        km_ref[r] = jnp.mean(k[r * MOBA_BLOCK:(r + 1) * MOBA_BLOCK], axis=0, keepdims=True)


def _proj(x1, mod4, norm_w, w_in, seq):
    T = x1.shape[0]
    tm = FFN_TOKENS
    tps = seq // tm
    nb = tm // MOBA_BLOCK
    tok = lambda w: pl.BlockSpec((tm, w), lambda i: (i, 0))
    return pl.pallas_call(
        _proj_kernel,
        out_shape=(jax.ShapeDtypeStruct((T, ATTN_WIDTH), BF16),
                   jax.ShapeDtypeStruct((T, ATTN_WIDTH), BF16),
                   jax.ShapeDtypeStruct((T, ATTN_WIDTH), BF16),
                   jax.ShapeDtypeStruct((T, MLSTM_WIDTH), F32),
                   jax.ShapeDtypeStruct((T, MLSTM_WIDTH), F32),
                   jax.ShapeDtypeStruct((T // MOBA_BLOCK, 1, ATTN_WIDTH), F32)),
        grid=(T // tm,),
        in_specs=[tok(D_MODEL), _mod_spec(3, tps), _mod_spec(4, tps),
                  _resident((1, D_MODEL)), _resident((D_MODEL, IN_COLS))],
        out_specs=(tok(ATTN_WIDTH), tok(ATTN_WIDTH), tok(ATTN_WIDTH),
                   tok(MLSTM_WIDTH), tok(MLSTM_WIDTH),
                   pl.BlockSpec((nb, 1, ATTN_WIDTH), lambda i: (i, 0, 0))),
        compiler_params=pltpu.CompilerParams(dimension_semantics=("arbitrary",),
                                             vmem_limit_bytes=VMEM_LIMIT_BYTES),
        name="proj",
    )(x1, mod4, mod4, norm_w, w_in)


def _gate_transform(g, gate_idx):
    log_sig = jnp.minimum(g, 0.0) - jnp.log1p(jnp.exp(-jnp.abs(g)))
    return jnp.where(gate_idx >= MLSTM_HEADS, log_sig, g)


def _mlstm_kernel(xm_ref, z_ref, cw_ref, cb_ref, wqk_ref, wv_ref, wift_ref,
                  bifr_ref, mn_ref, ms_ref, o_ref, ct_ref, m_ref, carry_ref):
    L = MLSTM_CHUNK
    dh = MLSTM_HEAD_DIM
    H = MLSTM_HEADS

    @pl.when(pl.program_id(1) == 0)
    def _():
        ct_ref[...] = jnp.zeros_like(ct_ref)
        m_ref[...] = jnp.zeros_like(m_ref)
        carry_ref[...] = jnp.zeros_like(carry_ref)

    cw = cw_ref[...]
    row8 = lax.broadcasted_iota(jnp.int32, (SUBLANES, MLSTM_WIDTH), 0)
    s_idx = lax.broadcasted_iota(jnp.int32, (L, L), 0)
    t_idx = lax.broadcasted_iota(jnp.int32, (L, L), 1)
    causal_t = s_idx <= t_idx
    tri_t = causal_t.astype(F32)
    ones_rows = (lax.broadcasted_iota(jnp.int32, (MLSTM_STATE_ROWS - dh, L), 0) == 0).astype(F32)

    def front(sq):
        xm = xm_ref[sq]
        carry = carry_ref[sq]
        acc = xm * cw[CONV_WIDTH - 1:CONV_WIDTH] + cb_ref[...]
        for k in range(1, CONV_WIDTH):
            rolled = pltpu.roll(xm, k, 0)
            top = jnp.where(row8 < k, pltpu.roll(carry, k, 0), rolled[0:SUBLANES])
            shifted = jnp.concatenate([top, rolled[SUBLANES:]], axis=0)
            acc = acc + shifted * cw[CONV_WIDTH - 1 - k:CONV_WIDTH - k]
        carry_ref[sq] = xm[L - SUBLANES:L]
        xc = acc * _sigmoid(acc)

        xc_bf = xc.astype(BF16)
        xm_bf = xm.astype(BF16)
        qk, vv, cat = [], [], []
        for h in range(H):
            hs = slice(h * dh, (h + 1) * dh)
            qk_h = jnp.dot(xc_bf[:, hs], wqk_ref[h], preferred_element_type=F32)
            v_h = jnp.dot(xm_bf[:, hs], wv_ref[h], preferred_element_type=F32)
            qk.append(qk_h)
            vv.append(v_h)
            cat += [qk_h.astype(BF16), v_h.astype(BF16)]
        cat = jnp.concatenate(cat, axis=1)

        g_row = lax.dot_general(wift_ref[...], cat, _NT, preferred_element_type=F32) + bifr_ref[...]
        a_row = _gate_transform(g_row, lax.broadcasted_iota(jnp.int32, (2 * H, L), 0))
        b_row = jnp.dot(a_row, tri_t, preferred_element_type=F32, precision=lax.Precision.HIGHEST)
        c_row = jnp.concatenate([a_row[0:H] - b_row[H:2 * H], jnp.zeros((SUBLANES - H, L), F32)], axis=0)
        return dict(xc=xc, qk=qk, vv=vv, a_row=a_row, b_row=b_row, c_col=jnp.transpose(c_row))

    def head(sq, h, f):
        hs = slice(h * dh, (h + 1) * dh)
        qt_bf = jnp.transpose(f["qk"][h][:, :dh]).astype(BF16)
        k_bf = (f["qk"][h][:, dh:] * (dh ** -0.5)).astype(BF16)
        v_aug_t = jnp.concatenate([jnp.transpose(f["vv"][h]), ones_rows], axis=0)
        c_c = f["c_col"][:, h:h + 1]
        b_r = f["b_row"][H + h:H + h + 1, :]
        i_r = f["a_row"][h:h + 1, :]
        m_prev = m_ref[sq, h, 0:1, 0:1]

        log_d = jnp.where(causal_t, c_c + b_r, -jnp.inf)
        log_inter = b_r + m_prev
        m_t = jnp.maximum(log_inter, jnp.max(log_d, axis=0, keepdims=True))
        w_intra = jnp.exp(log_d - m_t)
        w_inter = jnp.exp(log_inter - m_t)
        s_t = jnp.dot(k_bf, qt_bf, preferred_element_type=F32) * w_intra
        ct = ct_ref[sq, h]
        comb = (jnp.dot(v_aug_t.astype(BF16), s_t.astype(BF16), preferred_element_type=F32)
                + w_inter * jnp.dot(ct.astype(BF16), qt_bf, preferred_element_type=F32))
        hh_t = comb[:dh] / jnp.maximum(jnp.abs(comb[dh:dh + 1]), jnp.exp(-m_t))

        b_last = b_r[:, L - 1:L]
        log_g = b_last - b_r + i_r
        m_new = jnp.maximum(b_last + m_prev, jnp.max(log_g, axis=1, keepdims=True))
        w_g = jnp.exp(log_g - m_new)
        decay = jnp.exp(b_last + m_prev - m_new)
        upd = jnp.dot((v_aug_t * w_g).astype(BF16), k_bf, preferred_element_type=F32)
        ct_ref[sq, h] = decay * ct + upd
        m_ref[sq, h] = jnp.broadcast_to(m_new, (SUBLANES, LANES))

        mu = jnp.mean(hh_t, axis=0, keepdims=True)
        dlt = hh_t - mu
        var = jnp.mean(dlt * dlt, axis=0, keepdims=True)
        hn = jnp.transpose(dlt * lax.rsqrt(var + EPS)) * mn_ref[:, hs]
        zh = z_ref[sq, :, hs]
        return ((hn + ms_ref[:, hs] * f["xc"][:, hs]) * (zh * _sigmoid(zh))).astype(BF16)

    fronts = [front(sq) for sq in range(MLSTM_SEQS)]
    outs = [[] for _ in range(MLSTM_SEQS)]
    for h in range(H):
        for sq in range(MLSTM_SEQS):
            outs[sq].append(head(sq, h, fronts[sq]))
    for sq in range(MLSTM_SEQS):
        o_ref[sq] = jnp.concatenate(outs[sq], axis=1)


def _mlstm(xm, z, conv_w, conv_b, wqk, wv, wif_t, bif_r, mnorm, mskip):
    B, S, W = xm.shape
    L = MLSTM_CHUNK
    NS = MLSTM_SEQS
    assert B % NS == 0 and S % L == 0
    tok = pl.BlockSpec((NS, L, W), lambda b, j: (b, j, 0))
    return pl.pallas_call(
        _mlstm_kernel,
        out_shape=jax.ShapeDtypeStruct((B, S, W), BF16),
        grid=(B // NS, S // L),
        in_specs=[tok, tok,
                  _resident(conv_w.shape), _resident(conv_b.shape), _resident(wqk.shape), _resident(wv.shape),
                  _resident(wif_t.shape), _resident(bif_r.shape),
                  _resident(mnorm.shape), _resident(mskip.shape)],
        out_specs=tok,
        scratch_shapes=[pltpu.VMEM((NS, MLSTM_HEADS, MLSTM_STATE_ROWS, MLSTM_HEAD_DIM), F32),
                        pltpu.VMEM((NS, MLSTM_HEADS, SUBLANES, LANES), F32),
                        pltpu.VMEM((NS, SUBLANES, W), F32)],
        compiler_params=pltpu.CompilerParams(dimension_semantics=("arbitrary", "arbitrary"),
                                             vmem_limit_bytes=VMEM_LIMIT_BYTES),
        name="mlstm",
    )(xm, z, conv_w, conv_b, wqk, wv, wif_t, bif_r, mnorm, mskip)


def _moba_kernel(q_ref, k_ref, v_ref, km_ref, o_ref, lmin_ref, kaug_ref, vt_ref, qaug_ref, kn_ref, *, bounded):
    BS = MOBA_BLOCK
    NP = MOBA_PAIRS
    S = k_ref.shape[1]
    nb = km_ref.shape[1]
    hd = ATTN_HEAD_DIM
    i = pl.program_id(2)

    @pl.when(i == 0)
    def _():
        lane_s = lax.broadcasted_iota(jnp.int32, (S, LANES), 1)
        blk_s = lax.shift_right_logical(lax.broadcasted_iota(jnp.int32, (S, LANES), 0), BS.bit_length() - 1)
        k_tail = ((lane_s == blk_s) | (lane_s == nb)).astype(BF16)
        v_tail = (lax.broadcasted_iota(jnp.int32, (VT_ROWS - hd, S), 0) == 0).astype(BF16)
        for p in range(NP):
            ps = slice(p * LANES, (p + 1) * LANES)
            kaug_ref[p, :, 0:LANES] = k_ref[0, :, ps]
            kaug_ref[p, :, LANES:] = k_tail
            v_t = jnp.transpose(v_ref[0, :, ps].astype(F32)).astype(BF16)
            for hh in range(2):
                vt_ref[p, hh * VT_ROWS:hh * VT_ROWS + hd, :] = v_t[hh * hd:(hh + 1) * hd]
                vt_ref[p, hh * VT_ROWS + hd:(hh + 1) * VT_ROWS, :] = v_tail
            if bounded:
                kf = k_ref[0, :, ps].astype(F32)
                head_of_dim = lax.shift_right_logical(lax.broadcasted_iota(jnp.int32, (LANES, LANES), 0),
                                                      hd.bit_length() - 1)
                ind = (head_of_dim == lax.broadcasted_iota(jnp.int32, (LANES, LANES), 1)).astype(BF16)
                ksq = jnp.dot((kf * kf).astype(BF16), ind, preferred_element_type=F32)
                kn_ref[p] = jnp.broadcast_to(jnp.max(ksq, axis=0, keepdims=True), kn_ref.shape[1:])

            km = km_ref[0, :, ps]
            km_hi = km.astype(BF16)
            km_lo = (km - km_hi.astype(F32)).astype(BF16)
            dim_t = lax.broadcasted_iota(jnp.int32, (LANES, 2 * BS), 0)
            col_t = lax.broadcasted_iota(jnp.int32, (LANES, 2 * BS), 1)
            in_head = (lax.shift_right_logical(dim_t, hd.bit_length() - 1)
                       == lax.shift_right_logical(col_t, BS.bit_length() - 1))
            blk_t = lax.broadcasted_iota(jnp.int32, (nb, 2 * BS), 0)
            pad_rows = lax.broadcasted_iota(jnp.int32, (LANES - nb, 2 * BS), 0)
            if bounded:
                ksq_max = jnp.where(col_t[0:1] < BS, kn_ref[p, 0:1, 0:1], kn_ref[p, 0:1, 1:2])
            for j in range(nb):
                q_j = jnp.transpose(q_ref[0, j * BS:(j + 1) * BS, ps].astype(F32))
                qq = jnp.where(in_head, jnp.concatenate([q_j, q_j], axis=1), 0.0)
                qq_bf = qq.astype(BF16)
                gate = (jnp.dot(km_hi, qq_bf, preferred_element_type=F32)
                        + jnp.dot(km_lo, qq_bf, preferred_element_type=F32))
                cur = jnp.where(blk_t < j, gate, -jnp.inf)
                bias_t = jnp.full((nb, 2 * BS), MASK_VALUE, F32)
                for _ in range(MOBA_TOPK):
                    mx = jnp.max(cur, axis=0, keepdims=True)
                    first = jnp.min(jnp.where(cur == mx, blk_t, nb), axis=0, keepdims=True)
                    pick = (blk_t == first) & (mx > -jnp.inf)
                    bias_t = jnp.where(pick, 0.0, bias_t)
                    cur = jnp.where(pick, -jnp.inf, cur)
                bias_t = jnp.where(blk_t == j, 0.0, bias_t)
                if bounded:
                    bound = jnp.sqrt(jnp.sum(qq * qq, axis=0, keepdims=True) * ksq_max) * MOBA_BOUND_SLACK
                    tail = jnp.where(pad_rows == 0, -bound, 0.0)
                else:
                    tail = jnp.zeros((LANES - nb, 2 * BS), F32)
                qaug_ref[p, :, 2 * j * BS:2 * (j + 1) * BS] = jnp.concatenate(
                    [qq_bf, bias_t.astype(BF16), tail.astype(BF16)], axis=0)

    future_key = (lax.broadcasted_iota(jnp.int32, (BS, 2 * BS), 0)
                  > (lax.broadcasted_iota(jnp.int32, (BS, 2 * BS), 1) & (BS - 1)))

    for c in range(nb):
        @pl.when(i == c)
        def _():
            def scores(j):
                return [jnp.dot(kaug_ref[p, j * BS:(j + 1) * BS, :], qaug_ref[p, :, 2 * c * BS:2 * (c + 1) * BS],
                                preferred_element_type=F32) for p in range(NP)]

            m = [None] * NP
            pv = [None] * NP
            order = list(range(c, -1, -1))
            pending = [scores(j) for j in order[:MOBA_LOOKAHEAD]]
            for idx, j in enumerate(order):
                s_all = pending.pop(0)
                if idx + MOBA_LOOKAHEAD < len(order):
                    pending.append(scores(order[idx + MOBA_LOOKAHEAD]))
                for p in range(NP):
                    s = s_all[p]
                    if j == c:
                        s = jnp.where(future_key, MASK_VALUE, s)
                    alpha = None
                    if not bounded:
                        cm = jnp.max(s, axis=0, keepdims=True)
                        if m[p] is not None:
                            cm = jnp.maximum(m[p], cm)
                            alpha = jnp.exp(m[p] - cm)
                        m[p] = cm
                        s = s - cm
                    d = jnp.dot(vt_ref[p, :, j * BS:(j + 1) * BS], jnp.exp(s).astype(BF16),
                                preferred_element_type=F32)
                    if pv[p] is None:
                        pv[p] = d
                    elif alpha is None:
                        pv[p] = pv[p] + d
                    else:
                        pv[p] = alpha * pv[p] + d
            outs, lmin = [], None
            for p in range(NP):
                for hh in range(2):
                    pv_h = pv[p][hh * VT_ROWS:(hh + 1) * VT_ROWS, hh * BS:(hh + 1) * BS]
                    denom = pv_h[hd:hd + 1]
                    outs.append(pv_h[0:hd] / denom)
                    dmin = jnp.min(denom, axis=1, keepdims=True)
                    lmin = dmin if lmin is None else jnp.minimum(lmin, dmin)
            o_ref[0] = jnp.transpose(jnp.concatenate(outs, axis=0))
            lmin_ref[0] = jnp.broadcast_to(lmin, lmin_ref.shape[1:])


def _moba_call(q, k, v, kmean, bounded):
    B, S, W = q.shape
    BS = MOBA_BLOCK
    NP = MOBA_PAIRS
    nb = S // BS
    ngroups = W // (NP * LANES)
    assert nb < LANES - 1 and W % (NP * LANES) == 0
    qspec = pl.BlockSpec((1, BS, NP * LANES), lambda b, g, i: (b, i, g))
    kvspec = pl.BlockSpec((1, S, NP * LANES), lambda b, g, i: (b, 0, g))
    return pl.pallas_call(
        functools.partial(_moba_kernel, bounded=bounded),
        out_shape=(jax.ShapeDtypeStruct((B, S, W), F32),
                   jax.ShapeDtypeStruct((B * ngroups * nb, SUBLANES, LANES), F32)),
        grid=(B, ngroups, nb),
        in_specs=[kvspec, kvspec, kvspec, pl.BlockSpec((1, nb, NP * LANES), lambda b, g, i: (b, 0, g))],
        out_specs=(qspec, pl.BlockSpec((1, SUBLANES, LANES), lambda b, g, i: ((b * ngroups + g) * nb + i, 0, 0))),
        scratch_shapes=[pltpu.VMEM((NP, S, 2 * LANES), BF16),
                        pltpu.VMEM((NP, 2 * VT_ROWS, S), BF16),
                        pltpu.VMEM((NP, 2 * LANES, 2 * S), BF16),
                        pltpu.VMEM((NP, SUBLANES, LANES), F32)],
        compiler_params=pltpu.CompilerParams(dimension_semantics=("arbitrary",) * 3,
                                             vmem_limit_bytes=VMEM_LIMIT_BYTES),
        name="moba_bounded" if bounded else "moba_exact",
    )(q, k, v, kmean)


def _moba(q, k, v, kmean):
    o_fast, lmin = _moba_call(q, k, v, kmean, True)
    ok = jnp.all(lmin[:, 0, 0] > MOBA_MIN_DENOM)
    return lax.cond(ok, lambda: o_fast, lambda: _moba_call(q, k, v, kmean, False)[0])


def _out_ffn2_kernel(x1_ref, oa_ref, om_ref, g2_ref, sh_ref, sc_ref, g3_ref, an_ref, wo_ref,
                     nw_ref, wg_ref, wu_ref, wd_ref, fn_ref, o_ref, a_ref):
    oa = _rmsnorm(oa_ref[...], an_ref[...]).astype(BF16)
    mix = (jnp.dot(oa, wo_ref[0:ATTN_WIDTH, :], preferred_element_type=F32)
           + jnp.dot(om_ref[...], wo_ref[ATTN_WIDTH:, :], preferred_element_type=F32))
    x2 = x1_ref[...] + (1.0 + g2_ref[0, 0]) * mix
    h = _rms_mod(x2, nw_ref[...], sh_ref[0, 0], sc_ref[0, 0])
    y = _swiglu(h.astype(BF16), wg_ref, wu_ref, wd_ref, a_ref)
    x3 = x2 + (0.5 * (1.0 + g3_ref[0, 0])) * y
    o_ref[...] = _rmsnorm(x3, fn_ref[...])


def _out_ffn2(x1, oa, om, mod4, attn_norm, w_out, norm_w, wg, wu, wd, final_norm, seq):
    T = x1.shape[0]
    tm = FFN_TOKENS
    tps = seq // tm
    tok = lambda w: pl.BlockSpec((tm, w), lambda i: (i, 0))
    return pl.pallas_call(
        _out_ffn2_kernel,
        out_shape=jax.ShapeDtypeStruct((T, D_MODEL), F32),
        grid=(T // tm,),
        in_specs=[tok(D_MODEL), tok(ATTN_WIDTH), tok(MLSTM_WIDTH),
                  _mod_spec(5, tps), _mod_spec(6, tps), _mod_spec(7, tps), _mod_spec(8, tps),
                  _resident((1, ATTN_WIDTH)), _resident((D_MODEL, D_MODEL)), _resident((1, D_MODEL)),
                  _resident((D_MODEL, D_FF)), _resident((D_MODEL, D_FF)), _resident((D_FF, D_MODEL)),
                  _resident((1, D_MODEL))],
        out_specs=tok(D_MODEL),
        scratch_shapes=[pltpu.VMEM((tm, D_FF), BF16)],
        compiler_params=pltpu.CompilerParams(dimension_semantics=("arbitrary",),
                                             vmem_limit_bytes=VMEM_LIMIT_BYTES),
        name="out_ffn2",
    )(x1, oa, om, mod4, mod4, mod4, mod4, attn_norm, w_out, norm_w, wg, wu, wd, final_norm)


def kernel(x, c, w_ada, b_ada, ffn1_norm, ffn1_w_gate, ffn1_w_up, ffn1_w_down, mix_norm, w_in, conv_w, conv_b,
           w_q_m, w_k_m, w_v_m, w_if, b_if, mlstm_norm, mlstm_skip, attn_norm, w_out,
           ffn2_norm, ffn2_w_gate, ffn2_w_up, ffn2_w_down, final_norm):
    B, S, D = x.shape
    T = B * S
    assert w_ada.shape[0] == 1, "only depth 1 is supported"
    bf = lambda a: a.astype(BF16)
    xf = x.reshape(T, D)
    for l in range(1):
        mod4 = _adaln(c, w_ada[l], b_ada[l][None, :]).reshape(N_MOD, B, 1, D)
        x1 = _ffn1(xf, mod4, ffn1_norm[l][None, :], bf(ffn1_w_gate[l]), bf(ffn1_w_up[l]), bf(ffn1_w_down[l]), S)
        q, k, v, xm, z, kmean = _proj(x1, mod4, mix_norm[l][None, :], bf(w_in[l]), S)
        o_m = _mlstm(xm.reshape(B, S, MLSTM_WIDTH), z.reshape(B, S, MLSTM_WIDTH),
                     conv_w[l], conv_b[l][None, :],
                     bf(jnp.concatenate([w_q_m[l], w_k_m[l]], axis=-1)), bf(w_v_m[l]),
                     bf(w_if[l].T), b_if[l][:, None],
                     mlstm_norm[l][None, :], mlstm_skip[l][None, :])
        o_a = _moba(q.reshape(B, S, ATTN_WIDTH), k.reshape(B, S, ATTN_WIDTH), v.reshape(B, S, ATTN_WIDTH),
                    kmean.reshape(B, S // MOBA_BLOCK, ATTN_WIDTH))
        xf = _out_ffn2(x1, o_a.reshape(T, ATTN_WIDTH), o_m.reshape(T, MLSTM_WIDTH), mod4,
                       attn_norm[l][None, :], bf(w_out[l]), ffn2_norm[l][None, :],
                       bf(ffn2_w_gate[l]), bf(ffn2_w_up[l]), bf(ffn2_w_down[l]), final_norm[None, :], S)
    return xf.reshape(B, S, D)
```

```python
import functools

import jax
import jax.numpy as jnp
from jax import lax
from jax.experimental import pallas as pl
from jax.experimental.pallas import tpu as pltpu

F32 = jnp.float32
BF16 = jnp.bfloat16

D_MODEL = 1024
ATTN_HEADS = 8
ATTN_WIDTH = 512
ATTN_HEAD_DIM = 64
MLSTM_HEADS = 4
MLSTM_WIDTH = 512
MLSTM_HEAD_DIM = 128
IN_COLS = 3 * ATTN_WIDTH + 2 * MLSTM_WIDTH
CONV_WIDTH = 4
MOBA_BLOCK = 256
MOBA_TOPK = 3
D_FF = 2816
N_MOD = 9
EPS = 1e-6

LANES = 128
SUBLANES = 8
VMEM_LIMIT_BYTES = 56 * 1024 * 1024

FFN_TOKENS = 512
MXU_WIDTH = 256
FFN_CHUNKS = ((0, 6 * MXU_WIDTH), (6 * MXU_WIDTH, D_FF))
MLSTM_CHUNK = 256
MLSTM_STATE_ROWS = MLSTM_HEAD_DIM + 16
MLSTM_SEQS = 2
MOBA_PAIRS = 2
MOBA_LOOKAHEAD = 2
MOBA_CASE_BLOCKS = 2
MOBA_BOUND_SLACK = 1.0 + 2.0 ** -6
MOBA_MIN_DENOM = 1e-25
VT_ROWS = ATTN_HEAD_DIM + 16
MASK_VALUE = -1e30

_NT = (((1,), (1,)), ((), ()))
_TN = (((0,), (0,)), ((), ()))


def _sigmoid(x):
    return 1.0 / (1.0 + jnp.exp(-x))


def _rmsnorm(x, w):
    return x * lax.rsqrt(jnp.mean(x * x, axis=-1, keepdims=True) + EPS) * w


def _rms_mod(x, w, shift, scale):
    return _rmsnorm(x, w) * (1.0 + scale) + shift


def _swiglu(h_bf, wg_ref, wu_ref, wd_ref, a_ref):
    for lo, hi in FFN_CHUNKS:
        sl = slice(lo, hi)
        g = jnp.dot(h_bf, wg_ref[:, sl], preferred_element_type=F32)
        u = jnp.dot(h_bf, wu_ref[:, sl], preferred_element_type=F32)
        a_ref[:, sl] = (g * _sigmoid(g) * u).astype(BF16)
    return jnp.dot(a_ref[...], wd_ref[...], preferred_element_type=F32)


def _adaln_kernel(c_ref, w_ref, b_ref, o_ref):
    o_ref[0] = jnp.dot(c_ref[...].astype(BF16), w_ref[...].astype(BF16),
                       preferred_element_type=F32) + b_ref[...]


def _adaln(c, w_ada, b_ada):
    B = c.shape[0]
    return pl.pallas_call(
        _adaln_kernel,
        out_shape=jax.ShapeDtypeStruct((N_MOD, B, D_MODEL), F32),
        grid=(N_MOD,),
        in_specs=[
            pl.BlockSpec((B, D_MODEL), lambda j: (0, 0)),
            pl.BlockSpec((D_MODEL, D_MODEL), lambda j: (0, j)),
            pl.BlockSpec((1, D_MODEL), lambda j: (0, j)),
        ],
        out_specs=pl.BlockSpec((1, B, D_MODEL), lambda j: (j, 0, 0)),
        compiler_params=pltpu.CompilerParams(dimension_semantics=("arbitrary",),
                                             vmem_limit_bytes=VMEM_LIMIT_BYTES),
        name="adaln",
    )(c, w_ada, b_ada)


def _resident(shape):
    nd = len(shape)
    return pl.BlockSpec(shape, lambda *_: (0,) * nd, pipeline_mode=pl.Buffered(1))


def _mod_spec(k, tiles_per_seq):
    return pl.BlockSpec((1, 1, 1, D_MODEL), lambda i: (k, i // tiles_per_seq, 0, 0))


def _ffn1_kernel(x_ref, sh_ref, sc_ref, g_ref, nw_ref, wg_ref, wu_ref, wd_ref, o_ref, a_ref):
    x = x_ref[...]
    h = _rms_mod(x, nw_ref[...], sh_ref[0, 0], sc_ref[0, 0])
    y = _swiglu(h.astype(BF16), wg_ref, wu_ref, wd_ref, a_ref)
    o_ref[...] = x + (0.5 * (1.0 + g_ref[0, 0])) * y


def _ffn1(x2d, mod4, norm_w, wg, wu, wd, seq):
    T = x2d.shape[0]
    tm = FFN_TOKENS
    tps = seq // tm
    tok = pl.BlockSpec((tm, D_MODEL), lambda i: (i, 0))
    return pl.pallas_call(
        _ffn1_kernel,
        out_shape=jax.ShapeDtypeStruct((T, D_MODEL), F32),
        grid=(T // tm,),
        in_specs=[tok, _mod_spec(0, tps), _mod_spec(1, tps), _mod_spec(2, tps),
                  _resident((1, D_MODEL)),
                  _resident((D_MODEL, D_FF)), _resident((D_MODEL, D_FF)), _resident((D_FF, D_MODEL))],
        out_specs=tok,
        scratch_shapes=[pltpu.VMEM((tm, D_FF), BF16)],
        compiler_params=pltpu.CompilerParams(dimension_semantics=("arbitrary",),
                                             vmem_limit_bytes=VMEM_LIMIT_BYTES),
        name="ffn1",
    )(x2d, mod4, mod4, mod4, norm_w, wg, wu, wd)


def _proj_kernel(x_ref, sh_ref, sc_ref, nw_ref, win_ref,
                 q_ref, k_ref, v_ref, xm_ref, z_ref, km_ref):
    h = _rms_mod(x_ref[...], nw_ref[...], sh_ref[0, 0], sc_ref[0, 0]).astype(BF16)
    proj = jnp.dot(h, win_ref[...], preferred_element_type=F32)
    aw = ATTN_WIDTH
    q_ref[...] = (proj[:, 0:aw] * (ATTN_HEAD_DIM ** -0.5)).astype(BF16)
    k = proj[:, aw:2 * aw]
    k_ref[...] = k.astype(BF16)
    v_ref[...] = proj[:, 2 * aw:3 * aw].astype(BF16)
    xm_ref[...] = proj[:, 3 * aw:3 * aw + MLSTM_WIDTH]
    z_ref[...] = proj[:, 3 * aw + MLSTM_WIDTH:]
    for r in range(FFN_TOKENS // MOBA_BLOCK):
        km_ref[r] = jnp.mean(k[r * MOBA_BLOCK:(r + 1) * MOBA_BLOCK], axis=0, keepdims=True)


def _proj(x1, mod4, norm_w, w_in, seq):
    T = x1.shape[0]
    tm = FFN_TOKENS
    tps = seq // tm
    nb = tm // MOBA_BLOCK
    tok = lambda w: pl.BlockSpec((tm, w), lambda i: (i, 0))
    return pl.pallas_call(
        _proj_kernel,
        out_shape=(jax.ShapeDtypeStruct((T, ATTN_WIDTH), BF16),
                   jax.ShapeDtypeStruct((T, ATTN_WIDTH), BF16),
                   jax.ShapeDtypeStruct((T, ATTN_WIDTH), BF16),
                   jax.ShapeDtypeStruct((T, MLSTM_WIDTH), F32),
                   jax.ShapeDtypeStruct((T, MLSTM_WIDTH), F32),
                   jax.ShapeDtypeStruct((T // MOBA_BLOCK, 1, ATTN_WIDTH), F32)),
        grid=(T // tm,),
        in_specs=[tok(D_MODEL), _mod_spec(3, tps), _mod_spec(4, tps),
                  _resident((1, D_MODEL)), _resident((D_MODEL, IN_COLS))],
        out_specs=(tok(ATTN_WIDTH), tok(ATTN_WIDTH), tok(ATTN_WIDTH),
                   tok(MLSTM_WIDTH), tok(MLSTM_WIDTH),
                   pl.BlockSpec((nb, 1, ATTN_WIDTH), lambda i: (i, 0, 0))),
        compiler_params=pltpu.CompilerParams(dimension_semantics=("arbitrary",),
                                             vmem_limit_bytes=VMEM_LIMIT_BYTES),
        name="proj",
    )(x1, mod4, mod4, norm_w, w_in)


def _gate_transform(g, gate_idx):
    log_sig = jnp.minimum(g, 0.0) - jnp.log1p(jnp.exp(-jnp.abs(g)))
    return jnp.where(gate_idx >= MLSTM_HEADS, log_sig, g)


def _mlstm_kernel(xm_ref, z_ref, cw_ref, cb_ref, wqk_ref, wv_ref, wift_ref,
                  bifr_ref, mn_ref, ms_ref, o_ref, ct_ref, m_ref, carry_ref):
    L = MLSTM_CHUNK
    dh = MLSTM_HEAD_DIM
    H = MLSTM_HEADS

    @pl.when(pl.program_id(1) == 0)
    def _():
        ct_ref[...] = jnp.zeros_like(ct_ref)
        m_ref[...] = jnp.zeros_like(m_ref)
        carry_ref[...] = jnp.zeros_like(carry_ref)

    cw = cw_ref[...]
    row8 = lax.broadcasted_iota(jnp.int32, (SUBLANES, MLSTM_WIDTH), 0)
    s_idx = lax.broadcasted_iota(jnp.int32, (L, L), 0)
    t_idx = lax.broadcasted_iota(jnp.int32, (L, L), 1)
    causal_t = s_idx <= t_idx
    tri_t = causal_t.astype(F32)
    ones_rows = (lax.broadcasted_iota(jnp.int32, (MLSTM_STATE_ROWS - dh, L), 0) == 0).astype(F32)

    def front(sq):
        xm = xm_ref[sq]
        carry = carry_ref[sq]
        acc = xm * cw[CONV_WIDTH - 1:CONV_WIDTH] + cb_ref[...]
        for k in range(1, CONV_WIDTH):
            rolled = pltpu.roll(xm, k, 0)
            top = jnp.where(row8 < k, pltpu.roll(carry, k, 0), rolled[0:SUBLANES])
            shifted = jnp.concatenate([top, rolled[SUBLANES:]], axis=0)
            acc = acc + shifted * cw[CONV_WIDTH - 1 - k:CONV_WIDTH - k]
        carry_ref[sq] = xm[L - SUBLANES:L]
        xc = acc * _sigmoid(acc)

        xc_bf = xc.astype(BF16)
        xm_bf = xm.astype(BF16)
        qk, vv, cat = [], [], []
        for h in range(H):
            hs = slice(h * dh, (h + 1) * dh)
            qk_h = jnp.dot(xc_bf[:, hs], wqk_ref[h], preferred_element_type=F32)
            v_h = jnp.dot(xm_bf[:, hs], wv_ref[h], preferred_element_type=F32)
            qk.append(qk_h)
            vv.append(v_h)
            cat += [qk_h.astype(BF16), v_h.astype(BF16)]
        cat = jnp.concatenate(cat, axis=1)

        g_row = lax.dot_general(wift_ref[...], cat, _NT, preferred_element_type=F32) + bifr_ref[...]
        a_row = _gate_transform(g_row, lax.broadcasted_iota(jnp.int32, (2 * H, L), 0))
        b_row = jnp.dot(a_row, tri_t, preferred_element_type=F32, precision=lax.Precision.HIGHEST)
        c_row = jnp.concatenate([a_row[0:H] - b_row[H:2 * H], jnp.zeros((SUBLANES - H, L), F32)], axis=0)
        return dict(xc=xc, qk=qk, vv=vv, a_row=a_row, b_row=b_row, c_col=jnp.transpose(c_row))

    def head(sq, h, f):
        hs = slice(h * dh, (h + 1) * dh)
        qt_bf = jnp.transpose(f["qk"][h][:, :dh]).astype(BF16)
        k_bf = (f["qk"][h][:, dh:] * (dh ** -0.5)).astype(BF16)
        v_aug_t = jnp.concatenate([jnp.transpose(f["vv"][h]), ones_rows], axis=0)
        c_c = f["c_col"][:, h:h + 1]
        b_r = f["b_row"][H + h:H + h + 1, :]
        i_r = f["a_row"][h:h + 1, :]
        m_prev = m_ref[sq, h, 0:1, 0:1]

        log_d = jnp.where(causal_t, c_c + b_r, -jnp.inf)
        log_inter = b_r + m_prev
        m_t = jnp.maximum(log_inter, jnp.max(log_d, axis=0, keepdims=True))
        w_intra = jnp.exp(log_d - m_t)
        w_inter = jnp.exp(log_inter - m_t)
        s_t = jnp.dot(k_bf, qt_bf, preferred_element_type=F32) * w_intra
        ct = ct_ref[sq, h]
        comb = (jnp.dot(v_aug_t.astype(BF16), s_t.astype(BF16), preferred_element_type=F32)
                + w_inter * jnp.dot(ct.astype(BF16), qt_bf, preferred_element_type=F32))
        hh_t = comb[:dh] / jnp.maximum(jnp.abs(comb[dh:dh + 1]), jnp.exp(-m_t))

        b_last = b_r[:, L - 1:L]
        log_g = b_last - b_r + i_r
        m_new = jnp.maximum(b_last + m_prev, jnp.max(log_g, axis=1, keepdims=True))
        w_g = jnp.exp(log_g - m_new)
        decay = jnp.exp(b_last + m_prev - m_new)
        upd = jnp.dot((v_aug_t * w_g).astype(BF16), k_bf, preferred_element_type=F32)
        ct_ref[sq, h] = decay * ct + upd
        m_ref[sq, h] = jnp.broadcast_to(m_new, (SUBLANES, LANES))

        mu = jnp.mean(hh_t, axis=0, keepdims=True)
        dlt = hh_t - mu
        var = jnp.mean(dlt * dlt, axis=0, keepdims=True)
        hn = jnp.transpose(dlt * lax.rsqrt(var + EPS)) * mn_ref[:, hs]
        zh = z_ref[sq, :, hs]
        return ((hn + ms_ref[:, hs] * f["xc"][:, hs]) * (zh * _sigmoid(zh))).astype(BF16)

    fronts = [front(sq) for sq in range(MLSTM_SEQS)]
    outs = [[] for _ in range(MLSTM_SEQS)]
    for h in range(H):
        for sq in range(MLSTM_SEQS):
            outs[sq].append(head(sq, h, fronts[sq]))
    for sq in range(MLSTM_SEQS):
        o_ref[sq] = jnp.concatenate(outs[sq], axis=1)


def _mlstm(xm, z, conv_w, conv_b, wqk, wv, wif_t, bif_r, mnorm, mskip):
    B, S, W = xm.shape
    L = MLSTM_CHUNK
    NS = MLSTM_SEQS
    assert B % NS == 0 and S % L == 0
    tok = pl.BlockSpec((NS, L, W), lambda b, j: (b, j, 0))
    return pl.pallas_call(
        _mlstm_kernel,
        out_shape=jax.ShapeDtypeStruct((B, S, W), BF16),
        grid=(B // NS, S // L),
        in_specs=[tok, tok,
                  _resident(conv_w.shape), _resident(conv_b.shape), _resident(wqk.shape), _resident(wv.shape),
                  _resident(wif_t.shape), _resident(bif_r.shape),
                  _resident(mnorm.shape), _resident(mskip.shape)],
        out_specs=tok,
        scratch_shapes=[pltpu.VMEM((NS, MLSTM_HEADS, MLSTM_STATE_ROWS, MLSTM_HEAD_DIM), F32),
                        pltpu.VMEM((NS, MLSTM_HEADS, SUBLANES, LANES), F32),
                        pltpu.VMEM((NS, SUBLANES, W), F32)],
        compiler_params=pltpu.CompilerParams(dimension_semantics=("arbitrary", "arbitrary"),
                                             vmem_limit_bytes=VMEM_LIMIT_BYTES),
        name="mlstm",
    )(xm, z, conv_w, conv_b, wqk, wv, wif_t, bif_r, mnorm, mskip)


def _moba_kernel(q_ref, k_ref, v_ref, km_ref, o_ref, lmin_ref, kaug_ref, vt_ref, qaug_ref, kn_ref, *, bounded):
    BS = MOBA_BLOCK
    NP = MOBA_PAIRS
    S = k_ref.shape[1]
    nb = km_ref.shape[1]
    hd = ATTN_HEAD_DIM
    i = pl.program_id(2)

    @pl.when(i == 0)
    def _():
        lane_s = lax.broadcasted_iota(jnp.int32, (S, LANES), 1)
        blk_s = lax.shift_right_logical(lax.broadcasted_iota(jnp.int32, (S, LANES), 0), BS.bit_length() - 1)
        k_tail = ((lane_s == blk_s) | (lane_s == nb)).astype(BF16)
        v_tail = (lax.broadcasted_iota(jnp.int32, (VT_ROWS - hd, S), 0) == 0).astype(BF16)
        for p in range(NP):
            ps = slice(p * LANES, (p + 1) * LANES)
            kaug_ref[p, :, 0:LANES] = k_ref[0, :, ps]
            kaug_ref[p, :, LANES:] = k_tail
            v_t = jnp.transpose(v_ref[0, :, ps].astype(F32)).astype(BF16)
            for hh in range(2):
                vt_ref[p, hh * VT_ROWS:hh * VT_ROWS + hd, :] = v_t[hh * hd:(hh + 1) * hd]
                vt_ref[p, hh * VT_ROWS + hd:(hh + 1) * VT_ROWS, :] = v_tail
            if bounded:
                kf = k_ref[0, :, ps].astype(F32)
                head_of_dim = lax.shift_right_logical(lax.broadcasted_iota(jnp.int32, (LANES, LANES), 0),
                                                      hd.bit_length() - 1)
                ind = (head_of_dim == lax.broadcasted_iota(jnp.int32, (LANES, LANES), 1)).astype(BF16)
                ksq = jnp.dot((kf * kf).astype(BF16), ind, preferred_element_type=F32)
                kn_ref[p] = jnp.broadcast_to(jnp.max(ksq, axis=0, keepdims=True), kn_ref.shape[1:])

            km = km_ref[0, :, ps]
            km_hi = km.astype(BF16)
            km_lo = (km - km_hi.astype(F32)).astype(BF16)
            dim_t = lax.broadcasted_iota(jnp.int32, (LANES, 2 * BS), 0)
            col_t = lax.broadcasted_iota(jnp.int32, (LANES, 2 * BS), 1)
            in_head = (lax.shift_right_logical(dim_t, hd.bit_length() - 1)
                       == lax.shift_right_logical(col_t, BS.bit_length() - 1))
            blk_t = lax.broadcasted_iota(jnp.int32, (nb, 2 * BS), 0)
            pad_rows = lax.broadcasted_iota(jnp.int32, (LANES - nb, 2 * BS), 0)
            if bounded:
                ksq_max = jnp.where(col_t[0:1] < BS, kn_ref[p, 0:1, 0:1], kn_ref[p, 0:1, 1:2])
            for j in range(nb):
                q_j = jnp.transpose(q_ref[0, j * BS:(j + 1) * BS, ps].astype(F32))
                qq = jnp.where(in_head, jnp.concatenate([q_j, q_j], axis=1), 0.0)
                qq_bf = qq.astype(BF16)
                gate = (jnp.dot(km_hi, qq_bf, preferred_element_type=F32)
                        + jnp.dot(km_lo, qq_bf, preferred_element_type=F32))
                cur = jnp.where(blk_t < j, gate, -jnp.inf)
                bias_t = jnp.full((nb, 2 * BS), MASK_VALUE, F32)
                for _ in range(MOBA_TOPK):
                    mx = jnp.max(cur, axis=0, keepdims=True)
                    first = jnp.min(jnp.where(cur == mx, blk_t, nb), axis=0, keepdims=True)
                    pick = (blk_t == first) & (mx > -jnp.inf)
                    bias_t = jnp.where(pick, 0.0, bias_t)
                    cur = jnp.where(pick, -jnp.inf, cur)
                bias_t = jnp.where(blk_t == j, 0.0, bias_t)
                if bounded:
                    bound = jnp.sqrt(jnp.sum(qq * qq, axis=0, keepdims=True) * ksq_max) * MOBA_BOUND_SLACK
                    tail = jnp.where(pad_rows == 0, -bound, 0.0)
                else:
                    tail = jnp.zeros((LANES - nb, 2 * BS), F32)
                qaug_ref[p, j] = jnp.concatenate(
                    [qq_bf, bias_t.astype(BF16), tail.astype(BF16)], axis=0)

    future_key = (lax.broadcasted_iota(jnp.int32, (BS, 2 * BS), 0)
                  > (lax.broadcasted_iota(jnp.int32, (BS, 2 * BS), 1) & (BS - 1)))

    def scores(j):
        return [jnp.dot(kaug_ref[p, j * BS:(j + 1) * BS, :], qaug_ref[p, i], preferred_element_type=F32)
                for p in range(NP)]

    CB = MOBA_CASE_BLOCKS
    for c in range(nb // CB):
        @pl.when(lax.shift_right_logical(i, CB.bit_length() - 1) == c)
        def _():
            m = [None] * NP
            pv = [[None, None] for _ in range(NP)]
            order = list(range((c + 1) * CB - 1, -1, -1))
            pending = [scores(j) for j in order[:MOBA_LOOKAHEAD]]
            for idx, j in enumerate(order):
                s_all = pending.pop(0)
                if idx + MOBA_LOOKAHEAD < len(order):
                    pending.append(scores(order[idx + MOBA_LOOKAHEAD]))
                for p in range(NP):
                    s = s_all[p]
                    if j >= c * CB:
                        s = jnp.where(future_key, jnp.where(i == j, MASK_VALUE, s), s)
                    alpha = None
                    if not bounded:
                        cm = jnp.max(s, axis=0, keepdims=True)
                        if m[p] is not None:
                            cm = jnp.maximum(m[p], cm)
                            alpha = jnp.exp(m[p] - cm)
                        m[p] = cm
                        s = s - cm
                    p_bf = jnp.exp(s).astype(BF16)
                    for hh in range(2):
                        d = jnp.dot(vt_ref[p, hh * VT_ROWS:(hh + 1) * VT_ROWS, j * BS:(j + 1) * BS],
                                    p_bf[:, hh * BS:(hh + 1) * BS], preferred_element_type=F32)
                        if pv[p][hh] is None:
                            pv[p][hh] = d
                        elif alpha is None:
                            pv[p][hh] = pv[p][hh] + d
                        else:
                            pv[p][hh] = alpha[:, hh * BS:(hh + 1) * BS] * pv[p][hh] + d
            outs, lmin = [], None
            for p in range(NP):
                for hh in range(2):
                    denom = pv[p][hh][hd:hd + 1]
                    outs.append(pv[p][hh][0:hd] / denom)
                    dmin = jnp.min(denom, axis=1, keepdims=True)
                    lmin = dmin if lmin is None else jnp.minimum(lmin, dmin)
            o_ref[0] = jnp.transpose(jnp.concatenate(outs, axis=0))
            lmin_ref[0] = jnp.broadcast_to(lmin, lmin_ref.shape[1:])


def _moba_call(q, k, v, kmean, bounded):
    B, S, W = q.shape
    BS = MOBA_BLOCK
    NP = MOBA_PAIRS
    nb = S // BS
    ngroups = W // (NP * LANES)
    assert nb < LANES - 1 and W % (NP * LANES) == 0
    qspec = pl.BlockSpec((1, BS, NP * LANES), lambda b, g, i: (b, i, g))
    kvspec = pl.BlockSpec((1, S, NP * LANES), lambda b, g, i: (b, 0, g))
    return pl.pallas_call(
        functools.partial(_moba_kernel, bounded=bounded),
        out_shape=(jax.ShapeDtypeStruct((B, S, W), F32),
                   jax.ShapeDtypeStruct((B * ngroups * nb, SUBLANES, LANES), F32)),
        grid=(B, ngroups, nb),
        in_specs=[kvspec, kvspec, kvspec, pl.BlockSpec((1, nb, NP * LANES), lambda b, g, i: (b, 0, g))],
        out_specs=(qspec, pl.BlockSpec((1, SUBLANES, LANES), lambda b, g, i: ((b * ngroups + g) * nb + i, 0, 0))),
        scratch_shapes=[pltpu.VMEM((NP, S, 2 * LANES), BF16),
                        pltpu.VMEM((NP, 2 * VT_ROWS, S), BF16),
                        pltpu.VMEM((NP, nb, 2 * LANES, 2 * BS), BF16),
                        pltpu.VMEM((NP, SUBLANES, LANES), F32)],
        compiler_params=pltpu.CompilerParams(dimension_semantics=("arbitrary",) * 3,
                                             vmem_limit_bytes=VMEM_LIMIT_BYTES),
        name="moba_bounded" if bounded else "moba_exact",
    )(q, k, v, kmean)


def _moba(q, k, v, kmean):
    o_fast, lmin = _moba_call(q, k, v, kmean, True)
    ok = jnp.all(lmin[:, 0, 0] > MOBA_MIN_DENOM)
    return lax.cond(ok, lambda: o_fast, lambda: _moba_call(q, k, v, kmean, False)[0])


def _out_ffn2_kernel(x1_ref, oa_ref, om_ref, g2_ref, sh_ref, sc_ref, g3_ref, an_ref, wo_ref,
                     nw_ref, wg_ref, wu_ref, wd_ref, fn_ref, o_ref, a_ref):
    oa = _rmsnorm(oa_ref[...], an_ref[...]).astype(BF16)
    mix = (jnp.dot(oa, wo_ref[0:ATTN_WIDTH, :], preferred_element_type=F32)
           + jnp.dot(om_ref[...], wo_ref[ATTN_WIDTH:, :], preferred_element_type=F32))
    x2 = x1_ref[...] + (1.0 + g2_ref[0, 0]) * mix
    h = _rms_mod(x2, nw_ref[...], sh_ref[0, 0], sc_ref[0, 0])
    y = _swiglu(h.astype(BF16), wg_ref, wu_ref, wd_ref, a_ref)
    x3 = x2 + (0.5 * (1.0 + g3_ref[0, 0])) * y
    o_ref[...] = _rmsnorm(x3, fn_ref[...])


def _out_ffn2(x1, oa, om, mod4, attn_norm, w_out, norm_w, wg, wu, wd, final_norm, seq):
    T = x1.shape[0]
    tm = FFN_TOKENS
    tps = seq // tm
    tok = lambda w: pl.BlockSpec((tm, w), lambda i: (i, 0))
    return pl.pallas_call(
        _out_ffn2_kernel,
        out_shape=jax.ShapeDtypeStruct((T, D_MODEL), F32),
        grid=(T // tm,),
        in_specs=[tok(D_MODEL), tok(ATTN_WIDTH), tok(MLSTM_WIDTH),
                  _mod_spec(5, tps), _mod_spec(6, tps), _mod_spec(7, tps), _mod_spec(8, tps),
                  _resident((1, ATTN_WIDTH)), _resident((D_MODEL, D_MODEL)), _resident((1, D_MODEL)),
                  _resident((D_MODEL, D_FF)), _resident((D_MODEL, D_FF)), _resident((D_FF, D_MODEL)),
                  _resident((1, D_MODEL))],
        out_specs=tok(D_MODEL),
        scratch_shapes=[pltpu.VMEM((tm, D_FF), BF16)],
        compiler_params=pltpu.CompilerParams(dimension_semantics=("arbitrary",),
                                             vmem_limit_bytes=VMEM_LIMIT_BYTES),
        name="out_ffn2",
    )(x1, oa, om, mod4, mod4, mod4, mod4, attn_norm, w_out, norm_w, wg, wu, wd, final_norm)


def kernel(x, c, w_ada, b_ada, ffn1_norm, ffn1_w_gate, ffn1_w_up, ffn1_w_down, mix_norm, w_in, conv_w, conv_b,
           w_q_m, w_k_m, w_v_m, w_if, b_if, mlstm_norm, mlstm_skip, attn_norm, w_out,
           ffn2_norm, ffn2_w_gate, ffn2_w_up, ffn2_w_down, final_norm):
    B, S, D = x.shape
    T = B * S
    assert w_ada.shape[0] == 1, "only depth 1 is supported"
    bf = lambda a: a.astype(BF16)
    xf = x.reshape(T, D)
    for l in range(1):
        mod4 = _adaln(c, w_ada[l], b_ada[l][None, :]).reshape(N_MOD, B, 1, D)
        x1 = _ffn1(xf, mod4, ffn1_norm[l][None, :], bf(ffn1_w_gate[l]), bf(ffn1_w_up[l]), bf(ffn1_w_down[l]), S)
        q, k, v, xm, z, kmean = _proj(x1, mod4, mix_norm[l][None, :], bf(w_in[l]), S)
        o_m = _mlstm(xm.reshape(B, S, MLSTM_WIDTH), z.reshape(B, S, MLSTM_WIDTH),
                     conv_w[l], conv_b[l][None, :],
                     bf(jnp.concatenate([w_q_m[l], w_k_m[l]], axis=-1)), bf(w_v_m[l]),
                     bf(w_if[l].T), b_if[l][:, None],
                     mlstm_norm[l][None, :], mlstm_skip[l][None, :])
        o_a = _moba(q.reshape(B, S, ATTN_WIDTH), k.reshape(B, S, ATTN_WIDTH), v.reshape(B, S, ATTN_WIDTH),
                    kmean.reshape(B, S // MOBA_BLOCK, ATTN_WIDTH))
        xf = _out_ffn2(x1, o_a.reshape(T, ATTN_WIDTH), o_m.reshape(T, MLSTM_WIDTH), mod4,
                       attn_norm[l][None, :], bf(w_out[l]), ffn2_norm[l][None, :],
                       bf(ffn2_w_gate[l]), bf(ffn2_w_up[l]), bf(ffn2_w_down[l]), final_norm[None, :], S)
    return xf.reshape(B, S, D)
```

```python
import functools

import jax
import jax.numpy as jnp
from jax import lax
from jax.experimental import pallas as pl
from jax.experimental.pallas import tpu as pltpu

F32 = jnp.float32
BF16 = jnp.bfloat16

D_MODEL = 1024
ATTN_HEADS = 8
ATTN_WIDTH = 512
ATTN_HEAD_DIM = 64
MLSTM_HEADS = 4
MLSTM_WIDTH = 512
MLSTM_HEAD_DIM = 128
IN_COLS = 3 * ATTN_WIDTH + 2 * MLSTM_WIDTH
CONV_WIDTH = 4
MOBA_BLOCK = 256
MOBA_TOPK = 3
D_FF = 2816
N_MOD = 9
EPS = 1e-6

LANES = 128
SUBLANES = 8
VMEM_LIMIT_BYTES = 56 * 1024 * 1024

FFN_TOKENS = 512
MXU_WIDTH = 256
FFN_CHUNKS = ((0, 6 * MXU_WIDTH), (6 * MXU_WIDTH, D_FF))
MLSTM_CHUNK = 256
MLSTM_STATE_ROWS = MLSTM_HEAD_DIM + 16
MLSTM_SEQS = 2
MOBA_PAIRS = 2
MOBA_LOOKAHEAD = 2
MOBA_CASE_BLOCKS = 1
MOBA_BOUND_SLACK = 1.0 + 2.0 ** -6
MOBA_MIN_DENOM = 1e-25
VT_ROWS = LANES + 16
MASK_VALUE = -1e30

_NT = (((1,), (1,)), ((), ()))
_TN = (((0,), (0,)), ((), ()))


def _sigmoid(x):
    return 1.0 / (1.0 + jnp.exp(-x))


def _rmsnorm(x, w):
    return x * lax.rsqrt(jnp.mean(x * x, axis=-1, keepdims=True) + EPS) * w


def _rms_mod(x, w, shift, scale):
    return _rmsnorm(x, w) * (1.0 + scale) + shift


def _swiglu(h_bf, wg_ref, wu_ref, wd_ref, a_ref):
    for lo, hi in FFN_CHUNKS:
        sl = slice(lo, hi)
        g = jnp.dot(h_bf, wg_ref[:, sl], preferred_element_type=F32)
        u = jnp.dot(h_bf, wu_ref[:, sl], preferred_element_type=F32)
        a_ref[:, sl] = (g * _sigmoid(g) * u).astype(BF16)
    return jnp.dot(a_ref[...], wd_ref[...], preferred_element_type=F32)


def _adaln_kernel(c_ref, w_ref, b_ref, o_ref):
    o_ref[0] = jnp.dot(c_ref[...].astype(BF16), w_ref[...].astype(BF16),
                       preferred_element_type=F32) + b_ref[...]


def _adaln(c, w_ada, b_ada):
    B = c.shape[0]
    return pl.pallas_call(
        _adaln_kernel,
        out_shape=jax.ShapeDtypeStruct((N_MOD, B, D_MODEL), F32),
        grid=(N_MOD,),
        in_specs=[
            pl.BlockSpec((B, D_MODEL), lambda j: (0, 0)),
            pl.BlockSpec((D_MODEL, D_MODEL), lambda j: (0, j)),
            pl.BlockSpec((1, D_MODEL), lambda j: (0, j)),
        ],
        out_specs=pl.BlockSpec((1, B, D_MODEL), lambda j: (j, 0, 0)),
        compiler_params=pltpu.CompilerParams(dimension_semantics=("arbitrary",),
                                             vmem_limit_bytes=VMEM_LIMIT_BYTES),
        name="adaln",
    )(c, w_ada, b_ada)


def _resident(shape):
    nd = len(shape)
    return pl.BlockSpec(shape, lambda *_: (0,) * nd, pipeline_mode=pl.Buffered(1))


def _mod_spec(k, tiles_per_seq):
    return pl.BlockSpec((1, 1, 1, D_MODEL), lambda i: (k, i // tiles_per_seq, 0, 0))


def _ffn1_kernel(x_ref, sh_ref, sc_ref, g_ref, nw_ref, wg_ref, wu_ref, wd_ref, o_ref, a_ref):
    x = x_ref[...]
    h = _rms_mod(x, nw_ref[...], sh_ref[0, 0], sc_ref[0, 0])
    y = _swiglu(h.astype(BF16), wg_ref, wu_ref, wd_ref, a_ref)
    o_ref[...] = x + (0.5 * (1.0 + g_ref[0, 0])) * y


def _ffn1(x2d, mod4, norm_w, wg, wu, wd, seq):
    T = x2d.shape[0]
    tm = FFN_TOKENS
    tps = seq // tm
    tok = pl.BlockSpec((tm, D_MODEL), lambda i: (i, 0))
    return pl.pallas_call(
        _ffn1_kernel,
        out_shape=jax.ShapeDtypeStruct((T, D_MODEL), F32),
        grid=(T // tm,),
        in_specs=[tok, _mod_spec(0, tps), _mod_spec(1, tps), _mod_spec(2, tps),
                  _resident((1, D_MODEL)),
                  _resident((D_MODEL, D_FF)), _resident((D_MODEL, D_FF)), _resident((D_FF, D_MODEL))],
        out_specs=tok,
        scratch_shapes=[pltpu.VMEM((tm, D_FF), BF16)],
        compiler_params=pltpu.CompilerParams(dimension_semantics=("arbitrary",),
                                             vmem_limit_bytes=VMEM_LIMIT_BYTES),
        name="ffn1",
    )(x2d, mod4, mod4, mod4, norm_w, wg, wu, wd)


def _proj_kernel(x_ref, sh_ref, sc_ref, nw_ref, win_ref,
                 q_ref, k_ref, v_ref, xm_ref, z_ref, km_ref):
    h = _rms_mod(x_ref[...], nw_ref[...], sh_ref[0, 0], sc_ref[0, 0]).astype(BF16)
    proj = jnp.dot(h, win_ref[...], preferred_element_type=F32)
    aw = ATTN_WIDTH
    q_ref[...] = (proj[:, 0:aw] * (ATTN_HEAD_DIM ** -0.5)).astype(BF16)
    k = proj[:, aw:2 * aw]
    k_ref[...] = k.astype(BF16)
    v_ref[...] = proj[:, 2 * aw:3 * aw].astype(BF16)
    xm_ref[...] = proj[:, 3 * aw:3 * aw + MLSTM_WIDTH]
    z_ref[...] = proj[:, 3 * aw + MLSTM_WIDTH:]
    for r in range(FFN_TOKENS // MOBA_BLOCK):
        km_ref[r] = jnp.mean(k[r * MOBA_BLOCK:(r + 1) * MOBA_BLOCK], axis=0, keepdims=True)


def _proj(x1, mod4, norm_w, w_in, seq):
    T = x1.shape[0]
    tm = FFN_TOKENS
    tps = seq // tm
    nb = tm // MOBA_BLOCK
    tok = lambda w: pl.BlockSpec((tm, w), lambda i: (i, 0))
    return pl.pallas_call(
        _proj_kernel,
        out_shape=(jax.ShapeDtypeStruct((T, ATTN_WIDTH), BF16),
                   jax.ShapeDtypeStruct((T, ATTN_WIDTH), BF16),
                   jax.ShapeDtypeStruct((T, ATTN_WIDTH), BF16),
                   jax.ShapeDtypeStruct((T, MLSTM_WIDTH), F32),
                   jax.ShapeDtypeStruct((T, MLSTM_WIDTH), F32),
                   jax.ShapeDtypeStruct((T // MOBA_BLOCK, 1, ATTN_WIDTH), F32)),
        grid=(T // tm,),
        in_specs=[tok(D_MODEL), _mod_spec(3, tps), _mod_spec(4, tps),
                  _resident((1, D_MODEL)), _resident((D_MODEL, IN_COLS))],
        out_specs=(tok(ATTN_WIDTH), tok(ATTN_WIDTH), tok(ATTN_WIDTH),
                   tok(MLSTM_WIDTH), tok(MLSTM_WIDTH),
                   pl.BlockSpec((nb, 1, ATTN_WIDTH), lambda i: (i, 0, 0))),
        compiler_params=pltpu.CompilerParams(dimension_semantics=("arbitrary",),
                                             vmem_limit_bytes=VMEM_LIMIT_BYTES),
        name="proj",
    )(x1, mod4, mod4, norm_w, w_in)


def _gate_transform(g, gate_idx):
    log_sig = jnp.minimum(g, 0.0) - jnp.log1p(jnp.exp(-jnp.abs(g)))
    return jnp.where(gate_idx >= MLSTM_HEADS, log_sig, g)


def _mlstm_kernel(xm_ref, z_ref, cw_ref, cb_ref, wqk_ref, wv_ref, wift_ref,
                  bifr_ref, mn_ref, ms_ref, o_ref, ct_ref, m_ref, carry_ref):
    L = MLSTM_CHUNK
    dh = MLSTM_HEAD_DIM
    H = MLSTM_HEADS

    @pl.when(pl.program_id(1) == 0)
    def _():
        ct_ref[...] = jnp.zeros_like(ct_ref)
        m_ref[...] = jnp.zeros_like(m_ref)
        carry_ref[...] = jnp.zeros_like(carry_ref)

    cw = cw_ref[...]
    row8 = lax.broadcasted_iota(jnp.int32, (SUBLANES, MLSTM_WIDTH), 0)
    s_idx = lax.broadcasted_iota(jnp.int32, (L, L), 0)
    t_idx = lax.broadcasted_iota(jnp.int32, (L, L), 1)
    causal_t = s_idx <= t_idx
    tri_t = causal_t.astype(F32)
    ones_rows = (lax.broadcasted_iota(jnp.int32, (MLSTM_STATE_ROWS - dh, L), 0) == 0).astype(F32)

    def front(sq):
        xm = xm_ref[sq]
        carry = carry_ref[sq]
        acc = xm * cw[CONV_WIDTH - 1:CONV_WIDTH] + cb_ref[...]
        for k in range(1, CONV_WIDTH):
            rolled = pltpu.roll(xm, k, 0)
            top = jnp.where(row8 < k, pltpu.roll(carry, k, 0), rolled[0:SUBLANES])
            shifted = jnp.concatenate([top, rolled[SUBLANES:]], axis=0)
            acc = acc + shifted * cw[CONV_WIDTH - 1 - k:CONV_WIDTH - k]
        carry_ref[sq] = xm[L - SUBLANES:L]
        xc = acc * _sigmoid(acc)

        xc_bf = xc.astype(BF16)
        xm_bf = xm.astype(BF16)
        qk, vv, cat = [], [], []
        for h in range(H):
            hs = slice(h * dh, (h + 1) * dh)
            qk_h = jnp.dot(xc_bf[:, hs], wqk_ref[h], preferred_element_type=F32)
            v_h = jnp.dot(xm_bf[:, hs], wv_ref[h], preferred_element_type=F32)
            qk.append(qk_h)
            vv.append(v_h)
            cat += [qk_h.astype(BF16), v_h.astype(BF16)]
        cat = jnp.concatenate(cat, axis=1)

        g_row = lax.dot_general(wift_ref[...], cat, _NT, preferred_element_type=F32) + bifr_ref[...]
        a_row = _gate_transform(g_row, lax.broadcasted_iota(jnp.int32, (2 * H, L), 0))
        b_row = jnp.dot(a_row, tri_t, preferred_element_type=F32, precision=lax.Precision.HIGHEST)
        c_row = jnp.concatenate([a_row[0:H] - b_row[H:2 * H], jnp.zeros((SUBLANES - H, L), F32)], axis=0)
        return dict(xc=xc, qk=qk, vv=vv, a_row=a_row, b_row=b_row, c_col=jnp.transpose(c_row))

    def head(sq, h, f):
        hs = slice(h * dh, (h + 1) * dh)
        qt_bf = jnp.transpose(f["qk"][h][:, :dh]).astype(BF16)
        k_bf = (f["qk"][h][:, dh:] * (dh ** -0.5)).astype(BF16)
        v_aug_t = jnp.concatenate([jnp.transpose(f["vv"][h]), ones_rows], axis=0)
        c_c = f["c_col"][:, h:h + 1]
        b_r = f["b_row"][H + h:H + h + 1, :]
        i_r = f["a_row"][h:h + 1, :]
        m_prev = m_ref[sq, h, 0:1, 0:1]

        log_d = jnp.where(causal_t, c_c + b_r, -jnp.inf)
        log_inter = b_r + m_prev
        m_t = jnp.maximum(log_inter, jnp.max(log_d, axis=0, keepdims=True))
        w_intra = jnp.exp(log_d - m_t)
        w_inter = jnp.exp(log_inter - m_t)
        s_t = jnp.dot(k_bf, qt_bf, preferred_element_type=F32) * w_intra
        ct = ct_ref[sq, h]
        comb = (jnp.dot(v_aug_t.astype(BF16), s_t.astype(BF16), preferred_element_type=F32)
                + w_inter * jnp.dot(ct.astype(BF16), qt_bf, preferred_element_type=F32))
        hh_t = comb[:dh] / jnp.maximum(jnp.abs(comb[dh:dh + 1]), jnp.exp(-m_t))

        b_last = b_r[:, L - 1:L]
        log_g = b_last - b_r + i_r
        m_new = jnp.maximum(b_last + m_prev, jnp.max(log_g, axis=1, keepdims=True))
        w_g = jnp.exp(log_g - m_new)
        decay = jnp.exp(b_last + m_prev - m_new)
        upd = jnp.dot((v_aug_t * w_g).astype(BF16), k_bf, preferred_element_type=F32)
        ct_ref[sq, h] = decay * ct + upd
        m_ref[sq, h] = jnp.broadcast_to(m_new, (SUBLANES, LANES))

        mu = jnp.mean(hh_t, axis=0, keepdims=True)
        dlt = hh_t - mu
        var = jnp.mean(dlt * dlt, axis=0, keepdims=True)
        hn = jnp.transpose(dlt * lax.rsqrt(var + EPS)) * mn_ref[:, hs]
        zh = z_ref[sq, :, hs]
        return ((hn + ms_ref[:, hs] * f["xc"][:, hs]) * (zh * _sigmoid(zh))).astype(BF16)

    fronts = [front(sq) for sq in range(MLSTM_SEQS)]
    outs = [[] for _ in range(MLSTM_SEQS)]
    for h in range(H):
        for sq in range(MLSTM_SEQS):
            outs[sq].append(head(sq, h, fronts[sq]))
    for sq in range(MLSTM_SEQS):
        o_ref[sq] = jnp.concatenate(outs[sq], axis=1)


def _mlstm(xm, z, conv_w, conv_b, wqk, wv, wif_t, bif_r, mnorm, mskip):
    B, S, W = xm.shape
    L = MLSTM_CHUNK
    NS = MLSTM_SEQS
    assert B % NS == 0 and S % L == 0
    tok = pl.BlockSpec((NS, L, W), lambda b, j: (b, j, 0))
    return pl.pallas_call(
        _mlstm_kernel,
        out_shape=jax.ShapeDtypeStruct((B, S, W), BF16),
        grid=(B // NS, S // L),
        in_specs=[tok, tok,
                  _resident(conv_w.shape), _resident(conv_b.shape), _resident(wqk.shape), _resident(wv.shape),
                  _resident(wif_t.shape), _resident(bif_r.shape),
                  _resident(mnorm.shape), _resident(mskip.shape)],
        out_specs=tok,
        scratch_shapes=[pltpu.VMEM((NS, MLSTM_HEADS, MLSTM_STATE_ROWS, MLSTM_HEAD_DIM), F32),
                        pltpu.VMEM((NS, MLSTM_HEADS, SUBLANES, LANES), F32),
                        pltpu.VMEM((NS, SUBLANES, W), F32)],
        compiler_params=pltpu.CompilerParams(dimension_semantics=("arbitrary", "arbitrary"),
                                             vmem_limit_bytes=VMEM_LIMIT_BYTES),
        name="mlstm",
    )(xm, z, conv_w, conv_b, wqk, wv, wif_t, bif_r, mnorm, mskip)


def _moba_kernel(q_ref, k_ref, v_ref, km_ref, o_ref, lmin_ref, kaug_ref, vt_ref, qaug_ref, kn_ref, *, bounded):
    BS = MOBA_BLOCK
    NP = MOBA_PAIRS
    S = k_ref.shape[1]
    nb = km_ref.shape[1]
    hd = ATTN_HEAD_DIM
    i = pl.program_id(2)

    @pl.when(i == 0)
    def _():
        lane_s = lax.broadcasted_iota(jnp.int32, (S, LANES), 1)
        blk_s = lax.shift_right_logical(lax.broadcasted_iota(jnp.int32, (S, LANES), 0), BS.bit_length() - 1)
        k_tail = ((lane_s == blk_s) | (lane_s == nb)).astype(BF16)
        v_tail = (lax.broadcasted_iota(jnp.int32, (VT_ROWS - LANES, S), 0) == 0).astype(BF16)
        for p in range(NP):
            ps = slice(p * LANES, (p + 1) * LANES)
            kaug_ref[p, :, 0:LANES] = k_ref[0, :, ps]
            kaug_ref[p, :, LANES:] = k_tail
            vt_ref[p, 0:LANES, :] = jnp.transpose(v_ref[0, :, ps].astype(F32)).astype(BF16)
            vt_ref[p, LANES:, :] = v_tail
            if bounded:
                kf = k_ref[0, :, ps].astype(F32)
                head_of_dim = lax.shift_right_logical(lax.broadcasted_iota(jnp.int32, (LANES, LANES), 0),
                                                      hd.bit_length() - 1)
                ind = (head_of_dim == lax.broadcasted_iota(jnp.int32, (LANES, LANES), 1)).astype(BF16)
                ksq = jnp.dot((kf * kf).astype(BF16), ind, preferred_element_type=F32)
                kn_ref[p] = jnp.broadcast_to(jnp.max(ksq, axis=0, keepdims=True), kn_ref.shape[1:])

            km = km_ref[0, :, ps]
            km_hi = km.astype(BF16)
            km_lo = (km - km_hi.astype(F32)).astype(BF16)
            dim_t = lax.broadcasted_iota(jnp.int32, (LANES, 2 * BS), 0)
            col_t = lax.broadcasted_iota(jnp.int32, (LANES, 2 * BS), 1)
            in_head = (lax.shift_right_logical(dim_t, hd.bit_length() - 1)
                       == lax.shift_right_logical(col_t, BS.bit_length() - 1))
            blk_t = lax.broadcasted_iota(jnp.int32, (nb, 2 * BS), 0)
            pad_rows = lax.broadcasted_iota(jnp.int32, (LANES - nb, 2 * BS), 0)
            if bounded:
                ksq_max = jnp.where(col_t[0:1] < BS, kn_ref[p, 0:1, 0:1], kn_ref[p, 0:1, 1:2])
            for j in range(nb):
                q_j = jnp.transpose(q_ref[0, j * BS:(j + 1) * BS, ps].astype(F32))
                qq = jnp.where(in_head, jnp.concatenate([q_j, q_j], axis=1), 0.0)
                qq_bf = qq.astype(BF16)
                gate = (jnp.dot(km_hi, qq_bf, preferred_element_type=F32)
                        + jnp.dot(km_lo, qq_bf, preferred_element_type=F32))
                cur = jnp.where(blk_t < j, gate, -jnp.inf)
                bias_t = jnp.full((nb, 2 * BS), MASK_VALUE, F32)
                for _ in range(MOBA_TOPK):
                    mx = jnp.max(cur, axis=0, keepdims=True)
                    first = jnp.min(jnp.where(cur == mx, blk_t, nb), axis=0, keepdims=True)
                    pick = (blk_t == first) & (mx > -jnp.inf)
                    bias_t = jnp.where(pick, 0.0, bias_t)
                    cur = jnp.where(pick, -jnp.inf, cur)
                bias_t = jnp.where(blk_t == j, 0.0, bias_t)
                if bounded:
                    bound = jnp.sqrt(jnp.sum(qq * qq, axis=0, keepdims=True) * ksq_max) * MOBA_BOUND_SLACK
                    tail = jnp.where(pad_rows == 0, -bound, 0.0)
                else:
                    tail = jnp.zeros((LANES - nb, 2 * BS), F32)
                qaug_ref[p, j] = jnp.concatenate(
                    [qq_bf, bias_t.astype(BF16), tail.astype(BF16)], axis=0)

    future_key = (lax.broadcasted_iota(jnp.int32, (BS, 2 * BS), 0)
                  > (lax.broadcasted_iota(jnp.int32, (BS, 2 * BS), 1) & (BS - 1)))

    def scores(j):
        return [jnp.dot(kaug_ref[p, j * BS:(j + 1) * BS, :], qaug_ref[p, i], preferred_element_type=F32)
                for p in range(NP)]

    CB = MOBA_CASE_BLOCKS
    for c in range(nb // CB):
        @pl.when(lax.shift_right_logical(i, CB.bit_length() - 1) == c)
        def _():
            m = [None] * NP
            pv = [None] * NP
            order = list(range((c + 1) * CB - 1, -1, -1))
            pending = [scores(j) for j in order[:MOBA_LOOKAHEAD]]
            for idx, j in enumerate(order):
                s_all = pending.pop(0)
                if idx + MOBA_LOOKAHEAD < len(order):
                    pending.append(scores(order[idx + MOBA_LOOKAHEAD]))
                for p in range(NP):
                    s = s_all[p]
                    if j >= c * CB:
                        s = jnp.where(future_key, jnp.where(i == j, MASK_VALUE, s), s)
                    alpha = None
                    if not bounded:
                        cm = jnp.max(s, axis=0, keepdims=True)
                        if m[p] is not None:
                            cm = jnp.maximum(m[p], cm)
                            alpha = jnp.exp(m[p] - cm)
                        m[p] = cm
                        s = s - cm
                    d = jnp.dot(vt_ref[p, :, j * BS:(j + 1) * BS], jnp.exp(s).astype(BF16),
                                preferred_element_type=F32)
                    if pv[p] is None:
                        pv[p] = d
                    elif alpha is None:
                        pv[p] = pv[p] + d
                    else:
                        pv[p] = alpha * pv[p] + d
            outs, lmin = [], None
            for p in range(NP):
                denom = pv[p][LANES:LANES + 1]
                outs += [pv[p][0:hd, 0:BS] / denom[:, 0:BS], pv[p][hd:LANES, BS:] / denom[:, BS:]]
                dmin = jnp.min(denom, axis=1, keepdims=True)
                lmin = dmin if lmin is None else jnp.minimum(lmin, dmin)
            o_ref[0] = jnp.transpose(jnp.concatenate(outs, axis=0))
            lmin_ref[0] = jnp.broadcast_to(lmin, lmin_ref.shape[1:])


def _moba_call(q, k, v, kmean, bounded):
    B, S, W = q.shape
    BS = MOBA_BLOCK
    NP = MOBA_PAIRS
    nb = S // BS
    ngroups = W // (NP * LANES)
    assert nb < LANES - 1 and W % (NP * LANES) == 0
    qspec = pl.BlockSpec((1, BS, NP * LANES), lambda b, g, i: (b, i, g))
    kvspec = pl.BlockSpec((1, S, NP * LANES), lambda b, g, i: (b, 0, g))
    return pl.pallas_call(
        functools.partial(_moba_kernel, bounded=bounded),
        out_shape=(jax.ShapeDtypeStruct((B, S, W), F32),
                   jax.ShapeDtypeStruct((B * ngroups * nb, SUBLANES, LANES), F32)),
        grid=(B, ngroups, nb),
        in_specs=[kvspec, kvspec, kvspec, pl.BlockSpec((1, nb, NP * LANES), lambda b, g, i: (b, 0, g))],
        out_specs=(qspec, pl.BlockSpec((1, SUBLANES, LANES), lambda b, g, i: ((b * ngroups + g) * nb + i, 0, 0))),
        scratch_shapes=[pltpu.VMEM((NP, S, 2 * LANES), BF16),
                        pltpu.VMEM((NP, VT_ROWS, S), BF16),
                        pltpu.VMEM((NP, nb, 2 * LANES, 2 * BS), BF16),
                        pltpu.VMEM((NP, SUBLANES, LANES), F32)],
        compiler_params=pltpu.CompilerParams(dimension_semantics=("arbitrary",) * 3,
                                             vmem_limit_bytes=VMEM_LIMIT_BYTES),
        name="moba_bounded" if bounded else "moba_exact",
    )(q, k, v, kmean)


def _moba(q, k, v, kmean):
    o_fast, lmin = _moba_call(q, k, v, kmean, True)
    ok = jnp.all(lmin[:, 0, 0] > MOBA_MIN_DENOM)
    return lax.cond(ok, lambda: o_fast, lambda: _moba_call(q, k, v, kmean, False)[0])


def _out_ffn2_kernel(x1_ref, oa_ref, om_ref, g2_ref, sh_ref, sc_ref, g3_ref, an_ref, wo_ref,
                     nw_ref, wg_ref, wu_ref, wd_ref, fn_ref, o_ref, a_ref):
    oa = _rmsnorm(oa_ref[...], an_ref[...]).astype(BF16)
    mix = (jnp.dot(oa, wo_ref[0:ATTN_WIDTH, :], preferred_element_type=F32)
           + jnp.dot(om_ref[...], wo_ref[ATTN_WIDTH:, :], preferred_element_type=F32))
    x2 = x1_ref[...] + (1.0 + g2_ref[0, 0]) * mix
    h = _rms_mod(x2, nw_ref[...], sh_ref[0, 0], sc_ref[0, 0])
    y = _swiglu(h.astype(BF16), wg_ref, wu_ref, wd_ref, a_ref)
    x3 = x2 + (0.5 * (1.0 + g3_ref[0, 0])) * y
    o_ref[...] = _rmsnorm(x3, fn_ref[...])


def _out_ffn2(x1, oa, om, mod4, attn_norm, w_out, norm_w, wg, wu, wd, final_norm, seq):
    T = x1.shape[0]
    tm = FFN_TOKENS
    tps = seq // tm
    tok = lambda w: pl.BlockSpec((tm, w), lambda i: (i, 0))
    return pl.pallas_call(
        _out_ffn2_kernel,
        out_shape=jax.ShapeDtypeStruct((T, D_MODEL), F32),
        grid=(T // tm,),
        in_specs=[tok(D_MODEL), tok(ATTN_WIDTH), tok(MLSTM_WIDTH),
                  _mod_spec(5, tps), _mod_spec(6, tps), _mod_spec(7, tps), _mod_spec(8, tps),
                  _resident((1, ATTN_WIDTH)), _resident((D_MODEL, D_MODEL)), _resident((1, D_MODEL)),
                  _resident((D_MODEL, D_FF)), _resident((D_MODEL, D_FF)), _resident((D_FF, D_MODEL)),
                  _resident((1, D_MODEL))],
        out_specs=tok(D_MODEL),
        scratch_shapes=[pltpu.VMEM((tm, D_FF), BF16)],
        compiler_params=pltpu.CompilerParams(dimension_semantics=("arbitrary",),
                                             vmem_limit_bytes=VMEM_LIMIT_BYTES),
        name="out_ffn2",
    )(x1, oa, om, mod4, mod4, mod4, mod4, attn_norm, w_out, norm_w, wg, wu, wd, final_norm)


def kernel(x, c, w_ada, b_ada, ffn1_norm, ffn1_w_gate, ffn1_w_up, ffn1_w_down, mix_norm, w_in, conv_w, conv_b,
           w_q_m, w_k_m, w_v_m, w_if, b_if, mlstm_norm, mlstm_skip, attn_norm, w_out,
           ffn2_norm, ffn2_w_gate, ffn2_w_up, ffn2_w_down, final_norm):
    B, S, D = x.shape
    T = B * S
    assert w_ada.shape[0] == 1, "only depth 1 is supported"
    bf = lambda a: a.astype(BF16)
    xf = x.reshape(T, D)
    for l in range(1):
        mod4 = _adaln(c, w_ada[l], b_ada[l][None, :]).reshape(N_MOD, B, 1, D)
        x1 = _ffn1(xf, mod4, ffn1_norm[l][None, :], bf(ffn1_w_gate[l]), bf(ffn1_w_up[l]), bf(ffn1_w_down[l]), S)
        q, k, v, xm, z, kmean = _proj(x1, mod4, mix_norm[l][None, :], bf(w_in[l]), S)
        o_m = _mlstm(xm.reshape(B, S, MLSTM_WIDTH), z.reshape(B, S, MLSTM_WIDTH),
                     conv_w[l], conv_b[l][None, :],
                     bf(jnp.concatenate([w_q_m[l], w_k_m[l]], axis=-1)), bf(w_v_m[l]),
                     bf(w_if[l].T), b_if[l][:, None],
                     mlstm_norm[l][None, :], mlstm_skip[l][None, :])
        o_a = _moba(q.reshape(B, S, ATTN_WIDTH), k.reshape(B, S, ATTN_WIDTH), v.reshape(B, S, ATTN_WIDTH),
                    kmean.reshape(B, S // MOBA_BLOCK, ATTN_WIDTH))
        xf = _out_ffn2(x1, o_a.reshape(T, ATTN_WIDTH), o_m.reshape(T, MLSTM_WIDTH), mod4,
                       attn_norm[l][None, :], bf(w_out[l]), ffn2_norm[l][None, :],
                       bf(ffn2_w_gate[l]), bf(ffn2_w_up[l]), bf(ffn2_w_down[l]), final_norm[None, :], S)
    return xf.reshape(B, S, D)
```

```python
import functools

import jax
import jax.numpy as jnp
from jax import lax
from jax.experimental import pallas as pl
from jax.experimental.pallas import tpu as pltpu

F32 = jnp.float32
BF16 = jnp.bfloat16

D_MODEL = 1024
ATTN_HEADS = 8
ATTN_WIDTH = 512
ATTN_HEAD_DIM = 64
MLSTM_HEADS = 4
MLSTM_WIDTH = 512
MLSTM_HEAD_DIM = 128
IN_COLS = 3 * ATTN_WIDTH + 2 * MLSTM_WIDTH
CONV_WIDTH = 4
MOBA_BLOCK = 256
MOBA_TOPK = 3
D_FF = 2816
N_MOD = 9
EPS = 1e-6

LANES = 128
SUBLANES = 8
VMEM_LIMIT_BYTES = 56 * 1024 * 1024

FFN_TOKENS = 512
MXU_WIDTH = 256
FFN_CHUNKS = ((0, 6 * MXU_WIDTH), (6 * MXU_WIDTH, D_FF))
MLSTM_CHUNK = 256
MLSTM_STATE_ROWS = MLSTM_HEAD_DIM + 16
MLSTM_SEQS = 2
MOBA_PAIRS = 2
MOBA_LOOKAHEAD = 2
MOBA_CASE_BLOCKS = 2
MOBA_BOUND_SLACK = 1.0 + 2.0 ** -6
MOBA_MIN_DENOM = 1e-25
VT_ROWS = LANES + 16
MASK_VALUE = -1e30

_NT = (((1,), (1,)), ((), ()))
_TN = (((0,), (0,)), ((), ()))


def _sigmoid(x):
    return 1.0 / (1.0 + jnp.exp(-x))


def _rmsnorm(x, w):
    return x * lax.rsqrt(jnp.mean(x * x, axis=-1, keepdims=True) + EPS) * w


def _rms_mod(x, w, shift, scale):
    return _rmsnorm(x, w) * (1.0 + scale) + shift


def _swiglu(h_bf, wg_ref, wu_ref, wd_ref, a_ref):
    for lo, hi in FFN_CHUNKS:
        sl = slice(lo, hi)
        g = jnp.dot(h_bf, wg_ref[:, sl], preferred_element_type=F32)
        u = jnp.dot(h_bf, wu_ref[:, sl], preferred_element_type=F32)
        a_ref[:, sl] = (g * _sigmoid(g) * u).astype(BF16)
    return jnp.dot(a_ref[...], wd_ref[...], preferred_element_type=F32)


def _adaln_kernel(c_ref, w_ref, b_ref, o_ref):
    o_ref[0] = jnp.dot(c_ref[...].astype(BF16), w_ref[...].astype(BF16),
                       preferred_element_type=F32) + b_ref[...]


def _adaln(c, w_ada, b_ada):
    B = c.shape[0]
    return pl.pallas_call(
        _adaln_kernel,
        out_shape=jax.ShapeDtypeStruct((N_MOD, B, D_MODEL), F32),
        grid=(N_MOD,),
        in_specs=[
            pl.BlockSpec((B, D_MODEL), lambda j: (0, 0)),
            pl.BlockSpec((D_MODEL, D_MODEL), lambda j: (0, j)),
            pl.BlockSpec((1, D_MODEL), lambda j: (0, j)),
        ],
        out_specs=pl.BlockSpec((1, B, D_MODEL), lambda j: (j, 0, 0)),
        compiler_params=pltpu.CompilerParams(dimension_semantics=("arbitrary",),
                                             vmem_limit_bytes=VMEM_LIMIT_BYTES),
        name="adaln",
    )(c, w_ada, b_ada)


def _resident(shape):
    nd = len(shape)
    return pl.BlockSpec(shape, lambda *_: (0,) * nd, pipeline_mode=pl.Buffered(1))


def _mod_spec(k, tiles_per_seq):
    return pl.BlockSpec((1, 1, 1, D_MODEL), lambda i: (k, i // tiles_per_seq, 0, 0))


def _ffn1_kernel(x_ref, sh_ref, sc_ref, g_ref, nw_ref, wg_ref, wu_ref, wd_ref, o_ref, a_ref):
    x = x_ref[...]
    h = _rms_mod(x, nw_ref[...], sh_ref[0, 0], sc_ref[0, 0])
    y = _swiglu(h.astype(BF16), wg_ref, wu_ref, wd_ref, a_ref)
    o_ref[...] = x + (0.5 * (1.0 + g_ref[0, 0])) * y


def _ffn1(x2d, mod4, norm_w, wg, wu, wd, seq):
    T = x2d.shape[0]
    tm = FFN_TOKENS
    tps = seq // tm
    tok = pl.BlockSpec((tm, D_MODEL), lambda i: (i, 0))
    return pl.pallas_call(
        _ffn1_kernel,
        out_shape=jax.ShapeDtypeStruct((T, D_MODEL), F32),
        grid=(T // tm,),
        in_specs=[tok, _mod_spec(0, tps), _mod_spec(1, tps), _mod_spec(2, tps),
                  _resident((1, D_MODEL)),
                  _resident((D_MODEL, D_FF)), _resident((D_MODEL, D_FF)), _resident((D_FF, D_MODEL))],
        out_specs=tok,
        scratch_shapes=[pltpu.VMEM((tm, D_FF), BF16)],
        compiler_params=pltpu.CompilerParams(dimension_semantics=("arbitrary",),
                                             vmem_limit_bytes=VMEM_LIMIT_BYTES),
        name="ffn1",
    )(x2d, mod4, mod4, mod4, norm_w, wg, wu, wd)


def _proj_kernel(x_ref, sh_ref, sc_ref, nw_ref, win_ref,
                 q_ref, k_ref, v_ref, xm_ref, z_ref, km_ref):
    h = _rms_mod(x_ref[...], nw_ref[...], sh_ref[0, 0], sc_ref[0, 0]).astype(BF16)
    proj = jnp.dot(h, win_ref[...], preferred_element_type=F32)
    aw = ATTN_WIDTH
    q_ref[...] = (proj[:, 0:aw] * (ATTN_HEAD_DIM ** -0.5)).astype(BF16)
    k = proj[:, aw:2 * aw]
    k_ref[...] = k.astype(BF16)
    v_ref[...] = proj[:, 2 * aw:3 * aw].astype(BF16)
    xm_ref[...] = proj[:, 3 * aw:3 * aw + MLSTM_WIDTH]
    z_ref[...] = proj[:, 3 * aw + MLSTM_WIDTH:]
    for r in range(FFN_TOKENS // MOBA_BLOCK):
        km_ref[r] = jnp.mean(k[r * MOBA_BLOCK:(r + 1) * MOBA_BLOCK], axis=0, keepdims=True)


def _proj(x1, mod4, norm_w, w_in, seq):
    T = x1.shape[0]
    tm = FFN_TOKENS
    tps = seq // tm
    nb = tm // MOBA_BLOCK
    tok = lambda w: pl.BlockSpec((tm, w), lambda i: (i, 0))
    return pl.pallas_call(
        _proj_kernel,
        out_shape=(jax.ShapeDtypeStruct((T, ATTN_WIDTH), BF16),
                   jax.ShapeDtypeStruct((T, ATTN_WIDTH), BF16),
                   jax.ShapeDtypeStruct((T, ATTN_WIDTH), BF16),
                   jax.ShapeDtypeStruct((T, MLSTM_WIDTH), F32),
                   jax.ShapeDtypeStruct((T, MLSTM_WIDTH), F32),
                   jax.ShapeDtypeStruct((T // MOBA_BLOCK, 1, ATTN_WIDTH), F32)),
        grid=(T // tm,),
        in_specs=[tok(D_MODEL), _mod_spec(3, tps), _mod_spec(4, tps),
                  _resident((1, D_MODEL)), _resident((D_MODEL, IN_COLS))],
        out_specs=(tok(ATTN_WIDTH), tok(ATTN_WIDTH), tok(ATTN_WIDTH),
                   tok(MLSTM_WIDTH), tok(MLSTM_WIDTH),
                   pl.BlockSpec((nb, 1, ATTN_WIDTH), lambda i: (i, 0, 0))),
        compiler_params=pltpu.CompilerParams(dimension_semantics=("arbitrary",),
                                             vmem_limit_bytes=VMEM_LIMIT_BYTES),
        name="proj",
    )(x1, mod4, mod4, norm_w, w_in)


def _gate_transform(g, gate_idx):
    log_sig = jnp.minimum(g, 0.0) - jnp.log1p(jnp.exp(-jnp.abs(g)))
    return jnp.where(gate_idx >= MLSTM_HEADS, log_sig, g)


def _mlstm_kernel(xm_ref, z_ref, cw_ref, cb_ref, wqk_ref, wv_ref, wift_ref,
                  bifr_ref, mn_ref, ms_ref, o_ref, ct_ref, m_ref, carry_ref):
    L = MLSTM_CHUNK
    dh = MLSTM_HEAD_DIM
    H = MLSTM_HEADS

    @pl.when(pl.program_id(1) == 0)
    def _():
        ct_ref[...] = jnp.zeros_like(ct_ref)
        m_ref[...] = jnp.zeros_like(m_ref)
        carry_ref[...] = jnp.zeros_like(carry_ref)

    cw = cw_ref[...]
    row8 = lax.broadcasted_iota(jnp.int32, (SUBLANES, MLSTM_WIDTH), 0)
    s_idx = lax.broadcasted_iota(jnp.int32, (L, L), 0)
    t_idx = lax.broadcasted_iota(jnp.int32, (L, L), 1)
    causal_t = s_idx <= t_idx
    tri_t = causal_t.astype(F32)
    ones_rows = (lax.broadcasted_iota(jnp.int32, (MLSTM_STATE_ROWS - dh, L), 0) == 0).astype(F32)

    def front(sq):
        xm = xm_ref[sq]
        carry = carry_ref[sq]
        acc = xm * cw[CONV_WIDTH - 1:CONV_WIDTH] + cb_ref[...]
        for k in range(1, CONV_WIDTH):
            rolled = pltpu.roll(xm, k, 0)
            top = jnp.where(row8 < k, pltpu.roll(carry, k, 0), rolled[0:SUBLANES])
            shifted = jnp.concatenate([top, rolled[SUBLANES:]], axis=0)
            acc = acc + shifted * cw[CONV_WIDTH - 1 - k:CONV_WIDTH - k]
        carry_ref[sq] = xm[L - SUBLANES:L]
        xc = acc * _sigmoid(acc)

        xc_bf = xc.astype(BF16)
        xm_bf = xm.astype(BF16)
        qk, vv, cat = [], [], []
        for h in range(H):
            hs = slice(h * dh, (h + 1) * dh)
            qk_h = jnp.dot(xc_bf[:, hs], wqk_ref[h], preferred_element_type=F32)
            v_h = jnp.dot(xm_bf[:, hs], wv_ref[h], preferred_element_type=F32)
            qk.append(qk_h)
            vv.append(v_h)
            cat += [qk_h.astype(BF16), v_h.astype(BF16)]
        cat = jnp.concatenate(cat, axis=1)

        g_row = lax.dot_general(wift_ref[...], cat, _NT, preferred_element_type=F32) + bifr_ref[...]
        a_row = _gate_transform(g_row, lax.broadcasted_iota(jnp.int32, (2 * H, L), 0))
        b_row = jnp.dot(a_row, tri_t, preferred_element_type=F32, precision=lax.Precision.HIGHEST)
        c_row = jnp.concatenate([a_row[0:H] - b_row[H:2 * H], jnp.zeros((SUBLANES - H, L), F32)], axis=0)
        return dict(xc=xc, qk=qk, vv=vv, a_row=a_row, b_row=b_row, c_col=jnp.transpose(c_row))

    def head(sq, h, f):
        hs = slice(h * dh, (h + 1) * dh)
        qt_bf = jnp.transpose(f["qk"][h][:, :dh]).astype(BF16)
        k_bf = (f["qk"][h][:, dh:] * (dh ** -0.5)).astype(BF16)
        v_aug_t = jnp.concatenate([jnp.transpose(f["vv"][h]), ones_rows], axis=0)
        c_c = f["c_col"][:, h:h + 1]
        b_r = f["b_row"][H + h:H + h + 1, :]
        i_r = f["a_row"][h:h + 1, :]
        m_prev = m_ref[sq, h, 0:1, 0:1]

        log_d = jnp.where(causal_t, c_c + b_r, -jnp.inf)
        log_inter = b_r + m_prev
        m_t = jnp.maximum(log_inter, jnp.max(log_d, axis=0, keepdims=True))
        w_intra = jnp.exp(log_d - m_t)
        w_inter = jnp.exp(log_inter - m_t)
        s_t = jnp.dot(k_bf, qt_bf, preferred_element_type=F32) * w_intra
        ct = ct_ref[sq, h]
        comb = (jnp.dot(v_aug_t.astype(BF16), s_t.astype(BF16), preferred_element_type=F32)
                + w_inter * jnp.dot(ct.astype(BF16), qt_bf, preferred_element_type=F32))
        hh_t = comb[:dh] / jnp.maximum(jnp.abs(comb[dh:dh + 1]), jnp.exp(-m_t))

        b_last = b_r[:, L - 1:L]
        log_g = b_last - b_r + i_r
        m_new = jnp.maximum(b_last + m_prev, jnp.max(log_g, axis=1, keepdims=True))
        w_g = jnp.exp(log_g - m_new)
        decay = jnp.exp(b_last + m_prev - m_new)
        upd = jnp.dot((v_aug_t * w_g).astype(BF16), k_bf, preferred_element_type=F32)
        ct_ref[sq, h] = decay * ct + upd
        m_ref[sq, h] = jnp.broadcast_to(m_new, (SUBLANES, LANES))

        mu = jnp.mean(hh_t, axis=0, keepdims=True)
        dlt = hh_t - mu
        var = jnp.mean(dlt * dlt, axis=0, keepdims=True)
        hn = jnp.transpose(dlt * lax.rsqrt(var + EPS)) * mn_ref[:, hs]
        zh = z_ref[sq, :, hs]
        return ((hn + ms_ref[:, hs] * f["xc"][:, hs]) * (zh * _sigmoid(zh))).astype(BF16)

    fronts = [front(sq) for sq in range(MLSTM_SEQS)]
    outs = [[] for _ in range(MLSTM_SEQS)]
    for h in range(H):
        for sq in range(MLSTM_SEQS):
            outs[sq].append(head(sq, h, fronts[sq]))
    for sq in range(MLSTM_SEQS):
        o_ref[sq] = jnp.concatenate(outs[sq], axis=1)


def _mlstm(xm, z, conv_w, conv_b, wqk, wv, wif_t, bif_r, mnorm, mskip):
    B, S, W = xm.shape
    L = MLSTM_CHUNK
    NS = MLSTM_SEQS
    assert B % NS == 0 and S % L == 0
    tok = pl.BlockSpec((NS, L, W), lambda b, j: (b, j, 0))
    return pl.pallas_call(
        _mlstm_kernel,
        out_shape=jax.ShapeDtypeStruct((B, S, W), BF16),
        grid=(B // NS, S // L),
        in_specs=[tok, tok,
                  _resident(conv_w.shape), _resident(conv_b.shape), _resident(wqk.shape), _resident(wv.shape),
                  _resident(wif_t.shape), _resident(bif_r.shape),
                  _resident(mnorm.shape), _resident(mskip.shape)],
        out_specs=tok,
        scratch_shapes=[pltpu.VMEM((NS, MLSTM_HEADS, MLSTM_STATE_ROWS, MLSTM_HEAD_DIM), F32),
                        pltpu.VMEM((NS, MLSTM_HEADS, SUBLANES, LANES), F32),
                        pltpu.VMEM((NS, SUBLANES, W), F32)],
        compiler_params=pltpu.CompilerParams(dimension_semantics=("arbitrary", "arbitrary"),
                                             vmem_limit_bytes=VMEM_LIMIT_BYTES),
        name="mlstm",
    )(xm, z, conv_w, conv_b, wqk, wv, wif_t, bif_r, mnorm, mskip)


def _moba_kernel(q_ref, k_ref, v_ref, km_ref, o_ref, lmin_ref, kaug_ref, vt_ref, qaug_ref, kn_ref, *, bounded):
    BS = MOBA_BLOCK
    NP = MOBA_PAIRS
    S = k_ref.shape[1]
    nb = km_ref.shape[1]
    hd = ATTN_HEAD_DIM
    i = pl.program_id(2)

    @pl.when(i == 0)
    def _():
        lane_s = lax.broadcasted_iota(jnp.int32, (S, LANES), 1)
        blk_s = lax.shift_right_logical(lax.broadcasted_iota(jnp.int32, (S, LANES), 0), BS.bit_length() - 1)
        k_tail = ((lane_s == blk_s) | (lane_s == nb)).astype(BF16)
        v_tail = (lax.broadcasted_iota(jnp.int32, (VT_ROWS - LANES, S), 0) == 0).astype(BF16)
        for p in range(NP):
            ps = slice(p * LANES, (p + 1) * LANES)
            kaug_ref[p, :, 0:LANES] = k_ref[0, :, ps]
            kaug_ref[p, :, LANES:] = k_tail
            vt_ref[p, 0:LANES, :] = jnp.transpose(v_ref[0, :, ps].astype(F32)).astype(BF16)
            vt_ref[p, LANES:, :] = v_tail
            if bounded:
                kf = k_ref[0, :, ps].astype(F32)
                head_of_dim = lax.shift_right_logical(lax.broadcasted_iota(jnp.int32, (LANES, LANES), 0),
                                                      hd.bit_length() - 1)
                ind = (head_of_dim == lax.broadcasted_iota(jnp.int32, (LANES, LANES), 1)).astype(BF16)
                ksq = jnp.dot((kf * kf).astype(BF16), ind, preferred_element_type=F32)
                kn_ref[p] = jnp.broadcast_to(jnp.max(ksq, axis=0, keepdims=True), kn_ref.shape[1:])

            km = km_ref[0, :, ps]
            km_hi = km.astype(BF16)
            km_lo = (km - km_hi.astype(F32)).astype(BF16)
            dim_t = lax.broadcasted_iota(jnp.int32, (LANES, 2 * BS), 0)
            col_t = lax.broadcasted_iota(jnp.int32, (LANES, 2 * BS), 1)
            in_head = (lax.shift_right_logical(dim_t, hd.bit_length() - 1)
                       == lax.shift_right_logical(col_t, BS.bit_length() - 1))
            blk_t = lax.broadcasted_iota(jnp.int32, (nb, 2 * BS), 0)
            pad_rows = lax.broadcasted_iota(jnp.int32, (LANES - nb, 2 * BS), 0)
            if bounded:
                ksq_max = jnp.where(col_t[0:1] < BS, kn_ref[p, 0:1, 0:1], kn_ref[p, 0:1, 1:2])
            for j in range(nb):
                q_j = jnp.transpose(q_ref[0, j * BS:(j + 1) * BS, ps].astype(F32))
                qq = jnp.where(in_head, jnp.concatenate([q_j, q_j], axis=1), 0.0)
                qq_bf = qq.astype(BF16)
                gate = (jnp.dot(km_hi, qq_bf, preferred_element_type=F32)
                        + jnp.dot(km_lo, qq_bf, preferred_element_type=F32))
                cur = jnp.where(blk_t < j, gate, -jnp.inf)
                bias_t = jnp.full((nb, 2 * BS), MASK_VALUE, F32)
                for _ in range(MOBA_TOPK):
                    mx = jnp.max(cur, axis=0, keepdims=True)
                    first = jnp.min(jnp.where(cur == mx, blk_t, nb), axis=0, keepdims=True)
                    pick = (blk_t == first) & (mx > -jnp.inf)
                    bias_t = jnp.where(pick, 0.0, bias_t)
                    cur = jnp.where(pick, -jnp.inf, cur)
                bias_t = jnp.where(blk_t == j, 0.0, bias_t)
                if bounded:
                    bound = jnp.sqrt(jnp.sum(qq * qq, axis=0, keepdims=True) * ksq_max) * MOBA_BOUND_SLACK
                    tail = jnp.where(pad_rows == 0, -bound, 0.0)
                else:
                    tail = jnp.zeros((LANES - nb, 2 * BS), F32)
                qaug_ref[p, j] = jnp.concatenate(
                    [qq_bf, bias_t.astype(BF16), tail.astype(BF16)], axis=0)

    future_key = (lax.broadcasted_iota(jnp.int32, (BS, 2 * BS), 0)
                  > (lax.broadcasted_iota(jnp.int32, (BS, 2 * BS), 1) & (BS - 1)))

    def scores(j):
        return [jnp.dot(kaug_ref[p, j * BS:(j + 1) * BS, :], qaug_ref[p, i], preferred_element_type=F32)
                for p in range(NP)]

    CB = MOBA_CASE_BLOCKS
    for c in range(nb // CB):
        @pl.when(lax.shift_right_logical(i, CB.bit_length() - 1) == c)
        def _():
            m = [None] * NP
            pv = [None] * NP
            order = list(range((c + 1) * CB - 1, -1, -1))
            pending = [scores(j) for j in order[:MOBA_LOOKAHEAD]]
            for idx, j in enumerate(order):
                s_all = pending.pop(0)
                if idx + MOBA_LOOKAHEAD < len(order):
                    pending.append(scores(order[idx + MOBA_LOOKAHEAD]))
                for p in range(NP):
                    s = s_all[p]
                    if j >= c * CB:
                        s = jnp.where(future_key, jnp.where(i == j, MASK_VALUE, s), s)
                    alpha = None
                    if not bounded:
                        cm = jnp.max(s, axis=0, keepdims=True)
                        if m[p] is not None:
                            cm = jnp.maximum(m[p], cm)
                            alpha = jnp.exp(m[p] - cm)
                        m[p] = cm
                        s = s - cm
                    d = jnp.dot(vt_ref[p, :, j * BS:(j + 1) * BS], jnp.exp(s).astype(BF16),
                                preferred_element_type=F32)
                    if pv[p] is None:
                        pv[p] = d
                    elif alpha is None:
                        pv[p] = pv[p] + d
                    else:
                        pv[p] = alpha * pv[p] + d
            outs, lmin = [], None
            for p in range(NP):
                denom = pv[p][LANES:LANES + 1]
                outs += [pv[p][0:hd, 0:BS] / denom[:, 0:BS], pv[p][hd:LANES, BS:] / denom[:, BS:]]
                dmin = jnp.min(denom, axis=1, keepdims=True)
                lmin = dmin if lmin is None else jnp.minimum(lmin, dmin)
            o_ref[0] = jnp.transpose(jnp.concatenate(outs, axis=0))
            lmin_ref[0] = jnp.broadcast_to(lmin, lmin_ref.shape[1:])


def _moba_call(q, k, v, kmean, bounded):
    B, S, W = q.shape
    BS = MOBA_BLOCK
    NP = MOBA_PAIRS
    nb = S // BS
    ngroups = W // (NP * LANES)
    assert nb < LANES - 1 and W % (NP * LANES) == 0
    qspec = pl.BlockSpec((1, BS, NP * LANES), lambda b, g, i: (b, i, g))
    kvspec = pl.BlockSpec((1, S, NP * LANES), lambda b, g, i: (b, 0, g))
    return pl.pallas_call(
        functools.partial(_moba_kernel, bounded=bounded),
        out_shape=(jax.ShapeDtypeStruct((B, S, W), F32),
                   jax.ShapeDtypeStruct((B * ngroups * nb, SUBLANES, LANES), F32)),
        grid=(B, ngroups, nb),
        in_specs=[kvspec, kvspec, kvspec, pl.BlockSpec((1, nb, NP * LANES), lambda b, g, i: (b, 0, g))],
        out_specs=(qspec, pl.BlockSpec((1, SUBLANES, LANES), lambda b, g, i: ((b * ngroups + g) * nb + i, 0, 0))),
        scratch_shapes=[pltpu.VMEM((NP, S, 2 * LANES), BF16),
                        pltpu.VMEM((NP, VT_ROWS, S), BF16),
                        pltpu.VMEM((NP, nb, 2 * LANES, 2 * BS), BF16),
                        pltpu.VMEM((NP, SUBLANES, LANES), F32)],
        compiler_params=pltpu.CompilerParams(dimension_semantics=("arbitrary",) * 3,
                                             vmem_limit_bytes=VMEM_LIMIT_BYTES),
        name="moba_bounded" if bounded else "moba_exact",
    )(q, k, v, kmean)


def _moba(q, k, v, kmean):
    o_fast, lmin = _moba_call(q, k, v, kmean, True)
    ok = jnp.all(lmin[:, 0, 0] > MOBA_MIN_DENOM)
    return lax.cond(ok, lambda: o_fast, lambda: _moba_call(q, k, v, kmean, False)[0])


def _out_ffn2_kernel(x1_ref, oa_ref, om_ref, g2_ref, sh_ref, sc_ref, g3_ref, an_ref, wo_ref,
                     nw_ref, wg_ref, wu_ref, wd_ref, fn_ref, o_ref, a_ref):
    oa = _rmsnorm(oa_ref[...], an_ref[...]).astype(BF16)
    mix = (jnp.dot(oa, wo_ref[0:ATTN_WIDTH, :], preferred_element_type=F32)
           + jnp.dot(om_ref[...], wo_ref[ATTN_WIDTH:, :], preferred_element_type=F32))
    x2 = x1_ref[...] + (1.0 + g2_ref[0, 0]) * mix
    h = _rms_mod(x2, nw_ref[...], sh_ref[0, 0], sc_ref[0, 0])
    y = _swiglu(h.astype(BF16), wg_ref, wu_ref, wd_ref, a_ref)
    x3 = x2 + (0.5 * (1.0 + g3_ref[0, 0])) * y
    o_ref[...] = _rmsnorm(x3, fn_ref[...])


def _out_ffn2(x1, oa, om, mod4, attn_norm, w_out, norm_w, wg, wu, wd, final_norm, seq):
    T = x1.shape[0]
    tm = FFN_TOKENS
    tps = seq // tm
    tok = lambda w: pl.BlockSpec((tm, w), lambda i: (i, 0))
    return pl.pallas_call(
        _out_ffn2_kernel,
        out_shape=jax.ShapeDtypeStruct((T, D_MODEL), F32),
        grid=(T // tm,),
        in_specs=[tok(D_MODEL), tok(ATTN_WIDTH), tok(MLSTM_WIDTH),
                  _mod_spec(5, tps), _mod_spec(6, tps), _mod_spec(7, tps), _mod_spec(8, tps),
                  _resident((1, ATTN_WIDTH)), _resident((D_MODEL, D_MODEL)), _resident((1, D_MODEL)),
                  _resident((D_MODEL, D_FF)), _resident((D_MODEL, D_FF)), _resident((D_FF, D_MODEL)),
                  _resident((1, D_MODEL))],
        out_specs=tok(D_MODEL),
        scratch_shapes=[pltpu.VMEM((tm, D_FF), BF16)],
        compiler_params=pltpu.CompilerParams(dimension_semantics=("arbitrary",),
                                             vmem_limit_bytes=VMEM_LIMIT_BYTES),
        name="out_ffn2",
    )(x1, oa, om, mod4, mod4, mod4, mod4, attn_norm, w_out, norm_w, wg, wu, wd, final_norm)


def kernel(x, c, w_ada, b_ada, ffn1_norm, ffn1_w_gate, ffn1_w_up, ffn1_w_down, mix_norm, w_in, conv_w, conv_b,
           w_q_m, w_k_m, w_v_m, w_if, b_if, mlstm_norm, mlstm_skip, attn_norm, w_out,
           ffn2_norm, ffn2_w_gate, ffn2_w_up, ffn2_w_down, final_norm):
    B, S, D = x.shape
    T = B * S
    assert w_ada.shape[0] == 1, "only depth 1 is supported"
    bf = lambda a: a.astype(BF16)
    xf = x.reshape(T, D)
    for l in range(1):
        mod4 = _adaln(c, w_ada[l], b_ada[l][None, :]).reshape(N_MOD, B, 1, D)
        x1 = _ffn1(xf, mod4, ffn1_norm[l][None, :], bf(ffn1_w_gate[l]), bf(ffn1_w_up[l]), bf(ffn1_w_down[l]), S)
        q, k, v, xm, z, kmean = _proj(x1, mod4, mix_norm[l][None, :], bf(w_in[l]), S)
        o_m = _mlstm(xm.reshape(B, S, MLSTM_WIDTH), z.reshape(B, S, MLSTM_WIDTH),
                     conv_w[l], conv_b[l][None, :],
                     bf(jnp.concatenate([w_q_m[l], w_k_m[l]], axis=-1)), bf(w_v_m[l]),
                     bf(w_if[l].T), b_if[l][:, None],
                     mlstm_norm[l][None, :], mlstm_skip[l][None, :])
        o_a = _moba(q.reshape(B, S, ATTN_WIDTH), k.reshape(B, S, ATTN_WIDTH), v.reshape(B, S, ATTN_WIDTH),
                    kmean.reshape(B, S // MOBA_BLOCK, ATTN_WIDTH))
        xf = _out_ffn2(x1, o_a.reshape(T, ATTN_WIDTH), o_m.reshape(T, MLSTM_WIDTH), mod4,
                       attn_norm[l][None, :], bf(w_out[l]), ffn2_norm[l][None, :],
                       bf(ffn2_w_gate[l]), bf(ffn2_w_up[l]), bf(ffn2_w_down[l]), final_norm[None, :], S)
    return xf.reshape(B, S, D)
```

```python
import functools

import jax
import jax.numpy as jnp
from jax import lax
from jax.experimental import pallas as pl
from jax.experimental.pallas import tpu as pltpu

F32 = jnp.float32
BF16 = jnp.bfloat16

D_MODEL = 1024
ATTN_HEADS = 8
ATTN_WIDTH = 512
ATTN_HEAD_DIM = 64
MLSTM_HEADS = 4
MLSTM_WIDTH = 512
MLSTM_HEAD_DIM = 128
IN_COLS = 3 * ATTN_WIDTH + 2 * MLSTM_WIDTH
CONV_WIDTH = 4
MOBA_BLOCK = 256
MOBA_TOPK = 3
D_FF = 2816
N_MOD = 9
EPS = 1e-6

LANES = 128
SUBLANES = 8
VMEM_LIMIT_BYTES = 56 * 1024 * 1024

FFN_TOKENS = 512
MXU_WIDTH = 256
FFN_CHUNKS = ((0, 6 * MXU_WIDTH), (6 * MXU_WIDTH, D_FF))
MLSTM_CHUNK = 256
MLSTM_STATE_ROWS = MLSTM_HEAD_DIM + 16
MLSTM_SEQS = 4
MOBA_PAIRS = 2
MOBA_LOOKAHEAD = 2
MOBA_CASE_BLOCKS = 2
MOBA_BOUND_SLACK = 1.0 + 2.0 ** -6
MOBA_MIN_DENOM = 1e-25
VT_ROWS = LANES + 16
MASK_VALUE = -1e30

_NT = (((1,), (1,)), ((), ()))
_TN = (((0,), (0,)), ((), ()))


def _sigmoid(x):
    return 1.0 / (1.0 + jnp.exp(-x))


def _rmsnorm(x, w):
    return x * lax.rsqrt(jnp.mean(x * x, axis=-1, keepdims=True) + EPS) * w


def _rms_mod(x, w, shift, scale):
    return _rmsnorm(x, w) * (1.0 + scale) + shift


def _swiglu(h_bf, wg_ref, wu_ref, wd_ref, a_ref):
    for lo, hi in FFN_CHUNKS:
        sl = slice(lo, hi)
        g = jnp.dot(h_bf, wg_ref[:, sl], preferred_element_type=F32)
        u = jnp.dot(h_bf, wu_ref[:, sl], preferred_element_type=F32)
        a_ref[:, sl] = (g * _sigmoid(g) * u).astype(BF16)
    return jnp.dot(a_ref[...], wd_ref[...], preferred_element_type=F32)


def _adaln_kernel(c_ref, w_ref, b_ref, o_ref):
    o_ref[0] = jnp.dot(c_ref[...].astype(BF16), w_ref[...].astype(BF16),
                       preferred_element_type=F32) + b_ref[...]


def _adaln(c, w_ada, b_ada):
    B = c.shape[0]
    return pl.pallas_call(
        _adaln_kernel,
        out_shape=jax.ShapeDtypeStruct((N_MOD, B, D_MODEL), F32),
        grid=(N_MOD,),
        in_specs=[
            pl.BlockSpec((B, D_MODEL), lambda j: (0, 0)),
            pl.BlockSpec((D_MODEL, D_MODEL), lambda j: (0, j)),
            pl.BlockSpec((1, D_MODEL), lambda j: (0, j)),
        ],
        out_specs=pl.BlockSpec((1, B, D_MODEL), lambda j: (j, 0, 0)),
        compiler_params=pltpu.CompilerParams(dimension_semantics=("arbitrary",),
                                             vmem_limit_bytes=VMEM_LIMIT_BYTES),
        name="adaln",
    )(c, w_ada, b_ada)


def _resident(shape):
    nd = len(shape)
    return pl.BlockSpec(shape, lambda *_: (0,) * nd, pipeline_mode=pl.Buffered(1))


def _mod_spec(k, tiles_per_seq):
    return pl.BlockSpec((1, 1, 1, D_MODEL), lambda i: (k, i // tiles_per_seq, 0, 0))


def _ffn1_proj_kernel(x_ref, sh1_ref, sc1_ref, g1_ref, nw1_ref, wg_ref, wu_ref, wd_ref,
                      sh2_ref, sc2_ref, nw2_ref, win_ref,
                      x1_ref, q_ref, k_ref, v_ref, xm_ref, z_ref, km_ref, a_ref):
    x = x_ref[...]
    h = _rms_mod(x, nw1_ref[...], sh1_ref[0, 0], sc1_ref[0, 0])
    y = _swiglu(h.astype(BF16), wg_ref, wu_ref, wd_ref, a_ref)
    x1 = x + (0.5 * (1.0 + g1_ref[0, 0])) * y
    x1_ref[...] = x1

    h = _rms_mod(x1, nw2_ref[...], sh2_ref[0, 0], sc2_ref[0, 0]).astype(BF16)
    proj = jnp.dot(h, win_ref[...], preferred_element_type=F32)
    aw = ATTN_WIDTH
    q_ref[...] = (proj[:, 0:aw] * (ATTN_HEAD_DIM ** -0.5)).astype(BF16)
    k = proj[:, aw:2 * aw]
    k_ref[...] = k.astype(BF16)
    v_ref[...] = proj[:, 2 * aw:3 * aw].astype(BF16)
    xm_ref[...] = proj[:, 3 * aw:3 * aw + MLSTM_WIDTH]
    z_ref[...] = proj[:, 3 * aw + MLSTM_WIDTH:]
    for r in range(FFN_TOKENS // MOBA_BLOCK):
        km_ref[r] = jnp.mean(k[r * MOBA_BLOCK:(r + 1) * MOBA_BLOCK], axis=0, keepdims=True)


def _ffn1_proj(x2d, mod4, norm1_w, wg, wu, wd, norm2_w, w_in, seq):
    T = x2d.shape[0]
    tm = FFN_TOKENS
    tps = seq // tm
    nb = tm // MOBA_BLOCK
    tok = lambda w: pl.BlockSpec((tm, w), lambda i: (i, 0))
    return pl.pallas_call(
        _ffn1_proj_kernel,
        out_shape=(jax.ShapeDtypeStruct((T, D_MODEL), F32),
                   jax.ShapeDtypeStruct((T, ATTN_WIDTH), BF16),
                   jax.ShapeDtypeStruct((T, ATTN_WIDTH), BF16),
                   jax.ShapeDtypeStruct((T, ATTN_WIDTH), BF16),
                   jax.ShapeDtypeStruct((T, MLSTM_WIDTH), F32),
                   jax.ShapeDtypeStruct((T, MLSTM_WIDTH), F32),
                   jax.ShapeDtypeStruct((T // MOBA_BLOCK, 1, ATTN_WIDTH), F32)),
        grid=(T // tm,),
        in_specs=[tok(D_MODEL), _mod_spec(0, tps), _mod_spec(1, tps), _mod_spec(2, tps),
                  _resident((1, D_MODEL)),
                  _resident((D_MODEL, D_FF)), _resident((D_MODEL, D_FF)), _resident((D_FF, D_MODEL)),
                  _mod_spec(3, tps), _mod_spec(4, tps),
                  _resident((1, D_MODEL)), _resident((D_MODEL, IN_COLS))],
        out_specs=(tok(D_MODEL), tok(ATTN_WIDTH), tok(ATTN_WIDTH), tok(ATTN_WIDTH),
                   tok(MLSTM_WIDTH), tok(MLSTM_WIDTH),
                   pl.BlockSpec((nb, 1, ATTN_WIDTH), lambda i: (i, 0, 0))),
        scratch_shapes=[pltpu.VMEM((tm, D_FF), BF16)],
        compiler_params=pltpu.CompilerParams(dimension_semantics=("arbitrary",),
                                             vmem_limit_bytes=VMEM_LIMIT_BYTES),
        name="ffn1_proj",
    )(x2d, mod4, mod4, mod4, norm1_w, wg, wu, wd, mod4, mod4, norm2_w, w_in)


def _gate_transform(g, gate_idx):
    log_sig = jnp.minimum(g, 0.0) - jnp.log1p(jnp.exp(-jnp.abs(g)))
    return jnp.where(gate_idx >= MLSTM_HEADS, log_sig, g)


def _mlstm_kernel(xm_ref, z_ref, cw_ref, cb_ref, wqk_ref, wv_ref, wift_ref,
                  bifr_ref, mn_ref, ms_ref, o_ref, ct_ref, m_ref, carry_ref):
    L = MLSTM_CHUNK
    dh = MLSTM_HEAD_DIM
    H = MLSTM_HEADS

    @pl.when(pl.program_id(1) == 0)
    def _():
        ct_ref[...] = jnp.zeros_like(ct_ref)
        m_ref[...] = jnp.zeros_like(m_ref)
        carry_ref[...] = jnp.zeros_like(carry_ref)

    cw = cw_ref[...]
    row8 = lax.broadcasted_iota(jnp.int32, (SUBLANES, MLSTM_WIDTH), 0)
    s_idx = lax.broadcasted_iota(jnp.int32, (L, L), 0)
    t_idx = lax.broadcasted_iota(jnp.int32, (L, L), 1)
    causal_t = s_idx <= t_idx
    tri_t = causal_t.astype(F32)
    ones_rows = (lax.broadcasted_iota(jnp.int32, (MLSTM_STATE_ROWS - dh, L), 0) == 0).astype(F32)

    def front(sq):
        xm = xm_ref[sq]
        carry = carry_ref[sq]
        acc = xm * cw[CONV_WIDTH - 1:CONV_WIDTH] + cb_ref[...]
        for k in range(1, CONV_WIDTH):
            rolled = pltpu.roll(xm, k, 0)
            top = jnp.where(row8 < k, pltpu.roll(carry, k, 0), rolled[0:SUBLANES])
            shifted = jnp.concatenate([top, rolled[SUBLANES:]], axis=0)
            acc = acc + shifted * cw[CONV_WIDTH - 1 - k:CONV_WIDTH - k]
        carry_ref[sq] = xm[L - SUBLANES:L]
        xc = acc * _sigmoid(acc)

        xc_bf = xc.astype(BF16)
        xm_bf = xm.astype(BF16)
        qk, vv, cat = [], [], []
        for h in range(H):
            hs = slice(h * dh, (h + 1) * dh)
            qk_h = jnp.dot(xc_bf[:, hs], wqk_ref[h], preferred_element_type=F32)
            v_h = jnp.dot(xm_bf[:, hs], wv_ref[h], preferred_element_type=F32)
            qk.append(qk_h)
            vv.append(v_h)
            cat += [qk_h.astype(BF16), v_h.astype(BF16)]
        cat = jnp.concatenate(cat, axis=1)

        g_row = lax.dot_general(wift_ref[...], cat, _NT, preferred_element_type=F32) + bifr_ref[...]
        a_row = _gate_transform(g_row, lax.broadcasted_iota(jnp.int32, (2 * H, L), 0))
        b_row = jnp.dot(a_row, tri_t, preferred_element_type=F32, precision=lax.Precision.HIGHEST)
        c_row = jnp.concatenate([a_row[0:H] - b_row[H:2 * H], jnp.zeros((SUBLANES - H, L), F32)], axis=0)
        return dict(xc=xc, qk=qk, vv=vv, a_row=a_row, b_row=b_row, c_col=jnp.transpose(c_row))

    def head(sq, h, f):
        hs = slice(h * dh, (h + 1) * dh)
        qt_bf = jnp.transpose(f["qk"][h][:, :dh]).astype(BF16)
        k_bf = (f["qk"][h][:, dh:] * (dh ** -0.5)).astype(BF16)
        v_aug_t = jnp.concatenate([jnp.transpose(f["vv"][h]), ones_rows], axis=0)
        c_c = f["c_col"][:, h:h + 1]
        b_r = f["b_row"][H + h:H + h + 1, :]
        i_r = f["a_row"][h:h + 1, :]
        m_prev = m_ref[sq, h, 0:1, 0:1]

        log_d = jnp.where(causal_t, c_c + b_r, -jnp.inf)
        log_inter = b_r + m_prev
        m_t = jnp.maximum(log_inter, jnp.max(log_d, axis=0, keepdims=True))
        w_intra = jnp.exp(log_d - m_t)
        w_inter = jnp.exp(log_inter - m_t)
        s_t = jnp.dot(k_bf, qt_bf, preferred_element_type=F32) * w_intra
        ct = ct_ref[sq, h]
        comb = (jnp.dot(v_aug_t.astype(BF16), s_t.astype(BF16), preferred_element_type=F32)
                + w_inter * jnp.dot(ct.astype(BF16), qt_bf, preferred_element_type=F32))
        hh_t = comb[:dh] / jnp.maximum(jnp.abs(comb[dh:dh + 1]), jnp.exp(-m_t))

        b_last = b_r[:, L - 1:L]
        log_g = b_last - b_r + i_r
        m_new = jnp.maximum(b_last + m_prev, jnp.max(log_g, axis=1, keepdims=True))
        w_g = jnp.exp(log_g - m_new)
        decay = jnp.exp(b_last + m_prev - m_new)
        upd = jnp.dot((v_aug_t * w_g).astype(BF16), k_bf, preferred_element_type=F32)
        ct_ref[sq, h] = decay * ct + upd
        m_ref[sq, h] = jnp.broadcast_to(m_new, (SUBLANES, LANES))

        mu = jnp.mean(hh_t, axis=0, keepdims=True)
        dlt = hh_t - mu
        var = jnp.mean(dlt * dlt, axis=0, keepdims=True)
        hn = jnp.transpose(dlt * lax.rsqrt(var + EPS)) * mn_ref[:, hs]
        zh = z_ref[sq, :, hs]
        return ((hn + ms_ref[:, hs] * f["xc"][:, hs]) * (zh * _sigmoid(zh))).astype(BF16)

    fronts = [front(sq) for sq in range(MLSTM_SEQS)]
    outs = [[] for _ in range(MLSTM_SEQS)]
    for h in range(H):
        for sq in range(MLSTM_SEQS):
            outs[sq].append(head(sq, h, fronts[sq]))
    for sq in range(MLSTM_SEQS):
        o_ref[sq] = jnp.concatenate(outs[sq], axis=1)


def _mlstm(xm, z, conv_w, conv_b, wqk, wv, wif_t, bif_r, mnorm, mskip):
    B, S, W = xm.shape
    L = MLSTM_CHUNK
    NS = MLSTM_SEQS
    assert B % NS == 0 and S % L == 0
    tok = pl.BlockSpec((NS, L, W), lambda b, j: (b, j, 0))
    return pl.pallas_call(
        _mlstm_kernel,
        out_shape=jax.ShapeDtypeStruct((B, S, W), BF16),
        grid=(B // NS, S // L),
        in_specs=[tok, tok,
                  _resident(conv_w.shape), _resident(conv_b.shape), _resident(wqk.shape), _resident(wv.shape),
                  _resident(wif_t.shape), _resident(bif_r.shape),
                  _resident(mnorm.shape), _resident(mskip.shape)],
        out_specs=tok,
        scratch_shapes=[pltpu.VMEM((NS, MLSTM_HEADS, MLSTM_STATE_ROWS, MLSTM_HEAD_DIM), F32),
                        pltpu.VMEM((NS, MLSTM_HEADS, SUBLANES, LANES), F32),
                        pltpu.VMEM((NS, SUBLANES, W), F32)],
        compiler_params=pltpu.CompilerParams(dimension_semantics=("arbitrary", "arbitrary"),
                                             vmem_limit_bytes=VMEM_LIMIT_BYTES),
        name="mlstm",
    )(xm, z, conv_w, conv_b, wqk, wv, wif_t, bif_r, mnorm, mskip)


def _moba_kernel(q_ref, k_ref, v_ref, km_ref, o_ref, lmin_ref, kaug_ref, vt_ref, qaug_ref, kn_ref, *, bounded):
    BS = MOBA_BLOCK
    NP = MOBA_PAIRS
    S = k_ref.shape[1]
    nb = km_ref.shape[1]
    hd = ATTN_HEAD_DIM
    i = pl.program_id(2)

    @pl.when(i == 0)
    def _():
        lane_s = lax.broadcasted_iota(jnp.int32, (S, LANES), 1)
        blk_s = lax.shift_right_logical(lax.broadcasted_iota(jnp.int32, (S, LANES), 0), BS.bit_length() - 1)
        k_tail = ((lane_s == blk_s) | (lane_s == nb)).astype(BF16)
        v_tail = (lax.broadcasted_iota(jnp.int32, (VT_ROWS - LANES, S), 0) == 0).astype(BF16)
        for p in range(NP):
            ps = slice(p * LANES, (p + 1) * LANES)
            kaug_ref[p, :, 0:LANES] = k_ref[0, :, ps]
            kaug_ref[p, :, LANES:] = k_tail
            vt_ref[p, 0:LANES, :] = jnp.transpose(v_ref[0, :, ps].astype(F32)).astype(BF16)
            vt_ref[p, LANES:, :] = v_tail
            if bounded:
                kf = k_ref[0, :, ps].astype(F32)
                head_of_dim = lax.shift_right_logical(lax.broadcasted_iota(jnp.int32, (LANES, LANES), 0),
                                                      hd.bit_length() - 1)
                ind = (head_of_dim == lax.broadcasted_iota(jnp.int32, (LANES, LANES), 1)).astype(BF16)
                ksq = jnp.dot((kf * kf).astype(BF16), ind, preferred_element_type=F32)
                kn_ref[p] = jnp.broadcast_to(jnp.max(ksq, axis=0, keepdims=True), kn_ref.shape[1:])

            km = km_ref[0, :, ps]
            km_hi = km.astype(BF16)
            km_lo = (km - km_hi.astype(F32)).astype(BF16)
            dim_t = lax.broadcasted_iota(jnp.int32, (LANES, 2 * BS), 0)
            col_t = lax.broadcasted_iota(jnp.int32, (LANES, 2 * BS), 1)
            in_head = (lax.shift_right_logical(dim_t, hd.bit_length() - 1)
                       == lax.shift_right_logical(col_t, BS.bit_length() - 1))
            blk_t = lax.broadcasted_iota(jnp.int32, (nb, 2 * BS), 0)
            pad_rows = lax.broadcasted_iota(jnp.int32, (LANES - nb, 2 * BS), 0)
            if bounded:
                ksq_max = jnp.where(col_t[0:1] < BS, kn_ref[p, 0:1, 0:1], kn_ref[p, 0:1, 1:2])
            for j in range(nb):
                q_j = jnp.transpose(q_ref[0, j * BS:(j + 1) * BS, ps].astype(F32))
                qq = jnp.where(in_head, jnp.concatenate([q_j, q_j], axis=1), 0.0)
                qq_bf = qq.astype(BF16)
                gate = (jnp.dot(km_hi, qq_bf, preferred_element_type=F32)
                        + jnp.dot(km_lo, qq_bf, preferred_element_type=F32))
                cur = jnp.where(blk_t < j, gate, -jnp.inf)
                bias_t = jnp.full((nb, 2 * BS), MASK_VALUE, F32)
                for _ in range(MOBA_TOPK):
                    mx = jnp.max(cur, axis=0, keepdims=True)
                    first = jnp.min(jnp.where(cur == mx, blk_t, nb), axis=0, keepdims=True)
                    pick = (blk_t == first) & (mx > -jnp.inf)
                    bias_t = jnp.where(pick, 0.0, bias_t)
                    cur = jnp.where(pick, -jnp.inf, cur)
                bias_t = jnp.where(blk_t == j, 0.0, bias_t)
                if bounded:
                    bound = jnp.sqrt(jnp.sum(qq * qq, axis=0, keepdims=True) * ksq_max) * MOBA_BOUND_SLACK
                    tail = jnp.where(pad_rows == 0, -bound, 0.0)
                else:
                    tail = jnp.zeros((LANES - nb, 2 * BS), F32)
                qaug_ref[p, j] = jnp.concatenate(
                    [qq_bf, bias_t.astype(BF16), tail.astype(BF16)], axis=0)

    future_key = (lax.broadcasted_iota(jnp.int32, (BS, 2 * BS), 0)
                  > (lax.broadcasted_iota(jnp.int32, (BS, 2 * BS), 1) & (BS - 1)))

    def scores(j):
        return [jnp.dot(kaug_ref[p, j * BS:(j + 1) * BS, :], qaug_ref[p, i], preferred_element_type=F32)
                for p in range(NP)]

    CB = MOBA_CASE_BLOCKS
    for c in range(nb // CB):
        @pl.when(lax.shift_right_logical(i, CB.bit_length() - 1) == c)
        def _():
            m = [None] * NP
            pv = [None] * NP
            order = list(range((c + 1) * CB - 1, -1, -1))
            pending = [scores(j) for j in order[:MOBA_LOOKAHEAD]]
            for idx, j in enumerate(order):
                s_all = pending.pop(0)
                if idx + MOBA_LOOKAHEAD < len(order):
                    pending.append(scores(order[idx + MOBA_LOOKAHEAD]))
                for p in range(NP):
                    s = s_all[p]
                    if j >= c * CB:
                        s = jnp.where(future_key, jnp.where(i == j, MASK_VALUE, s), s)
                    alpha = None
                    if not bounded:
                        cm = jnp.max(s, axis=0, keepdims=True)
                        if m[p] is not None:
                            cm = jnp.maximum(m[p], cm)
                            alpha = jnp.exp(m[p] - cm)
                        m[p] = cm
                        s = s - cm
                    d = jnp.dot(vt_ref[p, :, j * BS:(j + 1) * BS], jnp.exp(s).astype(BF16),
                                preferred_element_type=F32)
                    if pv[p] is None:
                        pv[p] = d
                    elif alpha is None:
                        pv[p] = pv[p] + d
                    else:
                        pv[p] = alpha * pv[p] + d
            outs, lmin = [], None
            for p in range(NP):
                denom = pv[p][LANES:LANES + 1]
                outs += [pv[p][0:hd, 0:BS] / denom[:, 0:BS], pv[p][hd:LANES, BS:] / denom[:, BS:]]
                dmin = jnp.min(denom, axis=1, keepdims=True)
                lmin = dmin if lmin is None else jnp.minimum(lmin, dmin)
            o_ref[0] = jnp.transpose(jnp.concatenate(outs, axis=0))
            lmin_ref[0] = jnp.broadcast_to(lmin, lmin_ref.shape[1:])


def _moba_call(q, k, v, kmean, bounded):
    B, S, W = q.shape
    BS = MOBA_BLOCK
    NP = MOBA_PAIRS
    nb = S // BS
    ngroups = W // (NP * LANES)
    assert nb < LANES - 1 and W % (NP * LANES) == 0
    qspec = pl.BlockSpec((1, BS, NP * LANES), lambda b, g, i: (b, i, g))
    kvspec = pl.BlockSpec((1, S, NP * LANES), lambda b, g, i: (b, 0, g))
    return pl.pallas_call(
        functools.partial(_moba_kernel, bounded=bounded),
        out_shape=(jax.ShapeDtypeStruct((B, S, W), F32),
                   jax.ShapeDtypeStruct((B * ngroups * nb, SUBLANES, LANES), F32)),
        grid=(B, ngroups, nb),
        in_specs=[kvspec, kvspec, kvspec, pl.BlockSpec((1, nb, NP * LANES), lambda b, g, i: (b, 0, g))],
        out_specs=(qspec, pl.BlockSpec((1, SUBLANES, LANES), lambda b, g, i: ((b * ngroups + g) * nb + i, 0, 0))),
        scratch_shapes=[pltpu.VMEM((NP, S, 2 * LANES), BF16),
                        pltpu.VMEM((NP, VT_ROWS, S), BF16),
                        pltpu.VMEM((NP, nb, 2 * LANES, 2 * BS), BF16),
                        pltpu.VMEM((NP, SUBLANES, LANES), F32)],
        compiler_params=pltpu.CompilerParams(dimension_semantics=("arbitrary",) * 3,
                                             vmem_limit_bytes=VMEM_LIMIT_BYTES),
        name="moba_bounded" if bounded else "moba_exact",
    )(q, k, v, kmean)


def _moba(q, k, v, kmean):
    o_fast, lmin = _moba_call(q, k, v, kmean, True)
    ok = jnp.all(lmin[:, 0, 0] > MOBA_MIN_DENOM)
    return lax.cond(ok, lambda: o_fast, lambda: _moba_call(q, k, v, kmean, False)[0])


def _out_ffn2_kernel(x1_ref, oa_ref, om_ref, g2_ref, sh_ref, sc_ref, g3_ref, an_ref, wo_ref,
                     nw_ref, wg_ref, wu_ref, wd_ref, fn_ref, o_ref, a_ref):
    oa = _rmsnorm(oa_ref[...], an_ref[...]).astype(BF16)
    mix = (jnp.dot(oa, wo_ref[0:ATTN_WIDTH, :], preferred_element_type=F32)
           + jnp.dot(om_ref[...], wo_ref[ATTN_WIDTH:, :], preferred_element_type=F32))
    x2 = x1_ref[...] + (1.0 + g2_ref[0, 0]) * mix
    h = _rms_mod(x2, nw_ref[...], sh_ref[0, 0], sc_ref[0, 0])
    y = _swiglu(h.astype(BF16), wg_ref, wu_ref, wd_ref, a_ref)
    x3 = x2 + (0.5 * (1.0 + g3_ref[0, 0])) * y
    o_ref[...] = _rmsnorm(x3, fn_ref[...])


def _out_ffn2(x1, oa, om, mod4, attn_norm, w_out, norm_w, wg, wu, wd, final_norm, seq):
    T = x1.shape[0]
    tm = FFN_TOKENS
    tps = seq // tm
    tok = lambda w: pl.BlockSpec((tm, w), lambda i: (i, 0))
    return pl.pallas_call(
        _out_ffn2_kernel,
        out_shape=jax.ShapeDtypeStruct((T, D_MODEL), F32),
        grid=(T // tm,),
        in_specs=[tok(D_MODEL), tok(ATTN_WIDTH), tok(MLSTM_WIDTH),
                  _mod_spec(5, tps), _mod_spec(6, tps), _mod_spec(7, tps), _mod_spec(8, tps),
                  _resident((1, ATTN_WIDTH)), _resident((D_MODEL, D_MODEL)), _resident((1, D_MODEL)),
                  _resident((D_MODEL, D_FF)), _resident((D_MODEL, D_FF)), _resident((D_FF, D_MODEL)),
                  _resident((1, D_MODEL))],
        out_specs=tok(D_MODEL),
        scratch_shapes=[pltpu.VMEM((tm, D_FF), BF16)],
        compiler_params=pltpu.CompilerParams(dimension_semantics=("arbitrary",),
                                             vmem_limit_bytes=VMEM_LIMIT_BYTES),
        name="out_ffn2",
    )(x1, oa, om, mod4, mod4, mod4, mod4, attn_norm, w_out, norm_w, wg, wu, wd, final_norm)


def kernel(x, c, w_ada, b_ada, ffn1_norm, ffn1_w_gate, ffn1_w_up, ffn1_w_down, mix_norm, w_in, conv_w, conv_b,
           w_q_m, w_k_m, w_v_m, w_if, b_if, mlstm_norm, mlstm_skip, attn_norm, w_out,
           ffn2_norm, ffn2_w_gate, ffn2_w_up, ffn2_w_down, final_norm):
    B, S, D = x.shape
    T = B * S
    assert w_ada.shape[0] == 1, "only depth 1 is supported"
    bf = lambda a: a.astype(BF16)
    xf = x.reshape(T, D)
    for l in range(1):
        mod4 = _adaln(c, w_ada[l], b_ada[l][None, :]).reshape(N_MOD, B, 1, D)
        x1, q, k, v, xm, z, kmean = _ffn1_proj(
            xf, mod4, ffn1_norm[l][None, :], bf(ffn1_w_gate[l]), bf(ffn1_w_up[l]), bf(ffn1_w_down[l]),
            mix_norm[l][None, :], bf(w_in[l]), S)
        o_m = _mlstm(xm.reshape(B, S, MLSTM_WIDTH), z.reshape(B, S, MLSTM_WIDTH),
                     conv_w[l], conv_b[l][None, :],
                     bf(jnp.concatenate([w_q_m[l], w_k_m[l]], axis=-1)), bf(w_v_m[l]),
                     bf(w_if[l].T), b_if[l][:, None],
                     mlstm_norm[l][None, :], mlstm_skip[l][None, :])
        o_a = _moba(q.reshape(B, S, ATTN_WIDTH), k.reshape(B, S, ATTN_WIDTH), v.reshape(B, S, ATTN_WIDTH),
                    kmean.reshape(B, S // MOBA_BLOCK, ATTN_WIDTH))
        xf = _out_ffn2(x1, o_a.reshape(T, ATTN_WIDTH), o_m.reshape(T, MLSTM_WIDTH), mod4,
                       attn_norm[l][None, :], bf(w_out[l]), ffn2_norm[l][None, :],
                       bf(ffn2_w_gate[l]), bf(ffn2_w_up[l]), bf(ffn2_w_down[l]), final_norm[None, :], S)
    return xf.reshape(B, S, D)
```

```python
import functools

import jax
import jax.numpy as jnp
from jax import lax
from jax.experimental import pallas as pl
from jax.experimental.pallas import tpu as pltpu

F32 = jnp.float32
BF16 = jnp.bfloat16

D_MODEL = 1024
ATTN_HEADS = 8
ATTN_WIDTH = 512
ATTN_HEAD_DIM = 64
MLSTM_HEADS = 4
MLSTM_WIDTH = 512
MLSTM_HEAD_DIM = 128
IN_COLS = 3 * ATTN_WIDTH + 2 * MLSTM_WIDTH
CONV_WIDTH = 4
MOBA_BLOCK = 256
MOBA_TOPK = 3
D_FF = 2816
N_MOD = 9
EPS = 1e-6

LANES = 128
SUBLANES = 8
VMEM_LIMIT_BYTES = 56 * 1024 * 1024

FFN_TOKENS = 512
MXU_WIDTH = 256
FFN_CHUNKS = ((0, 6 * MXU_WIDTH), (6 * MXU_WIDTH, D_FF))
MLSTM_CHUNK = 256
MLSTM_STATE_ROWS = MLSTM_HEAD_DIM + 16
MLSTM_LOOKAHEAD = 2
MLSTM_SEQS = 4
MOBA_PAIRS = 2
MOBA_LOOKAHEAD = 2
MOBA_CASE_BLOCKS = 2
MOBA_BOUND_SLACK = 1.0 + 2.0 ** -6
MOBA_MIN_DENOM = 1e-25
VT_ROWS = LANES + 16
MASK_VALUE = -1e30

_NT = (((1,), (1,)), ((), ()))
_TN = (((0,), (0,)), ((), ()))


def _sigmoid(x):
    return 1.0 / (1.0 + jnp.exp(-x))


def _rmsnorm(x, w):
    return x * lax.rsqrt(jnp.mean(x * x, axis=-1, keepdims=True) + EPS) * w


def _rms_mod(x, w, shift, scale):
    return _rmsnorm(x, w) * (1.0 + scale) + shift


def _swiglu(h_bf, wg_ref, wu_ref, wd_ref, a_ref):
    for lo, hi in FFN_CHUNKS:
        sl = slice(lo, hi)
        g = jnp.dot(h_bf, wg_ref[:, sl], preferred_element_type=F32)
        u = jnp.dot(h_bf, wu_ref[:, sl], preferred_element_type=F32)
        a_ref[:, sl] = (g * _sigmoid(g) * u).astype(BF16)
    return jnp.dot(a_ref[...], wd_ref[...], preferred_element_type=F32)


def _adaln_kernel(c_ref, w_ref, b_ref, o_ref):
    o_ref[0] = jnp.dot(c_ref[...].astype(BF16), w_ref[...].astype(BF16),
                       preferred_element_type=F32) + b_ref[...]


def _adaln(c, w_ada, b_ada):
    B = c.shape[0]
    return pl.pallas_call(
        _adaln_kernel,
        out_shape=jax.ShapeDtypeStruct((N_MOD, B, D_MODEL), F32),
        grid=(N_MOD,),
        in_specs=[
            pl.BlockSpec((B, D_MODEL), lambda j: (0, 0)),
            pl.BlockSpec((D_MODEL, D_MODEL), lambda j: (0, j)),
            pl.BlockSpec((1, D_MODEL), lambda j: (0, j)),
        ],
        out_specs=pl.BlockSpec((1, B, D_MODEL), lambda j: (j, 0, 0)),
        compiler_params=pltpu.CompilerParams(dimension_semantics=("arbitrary",),
                                             vmem_limit_bytes=VMEM_LIMIT_BYTES),
        name="adaln",
    )(c, w_ada, b_ada)


def _resident(shape):
    nd = len(shape)
    return pl.BlockSpec(shape, lambda *_: (0,) * nd, pipeline_mode=pl.Buffered(1))


def _mod_spec(k, tiles_per_seq):
    return pl.BlockSpec((1, 1, 1, D_MODEL), lambda i: (k, i // tiles_per_seq, 0, 0))


def _ffn1_proj_kernel(x_ref, sh1_ref, sc1_ref, g1_ref, nw1_ref, wg_ref, wu_ref, wd_ref,
                      sh2_ref, sc2_ref, nw2_ref, win_ref,
                      x1_ref, q_ref, k_ref, v_ref, xm_ref, z_ref, km_ref, a_ref):
    x = x_ref[...]
    h = _rms_mod(x, nw1_ref[...], sh1_ref[0, 0], sc1_ref[0, 0])
    y = _swiglu(h.astype(BF16), wg_ref, wu_ref, wd_ref, a_ref)
    x1 = x + (0.5 * (1.0 + g1_ref[0, 0])) * y
    x1_ref[...] = x1

    h = _rms_mod(x1, nw2_ref[...], sh2_ref[0, 0], sc2_ref[0, 0]).astype(BF16)
    proj = jnp.dot(h, win_ref[...], preferred_element_type=F32)
    aw = ATTN_WIDTH
    q_ref[...] = (proj[:, 0:aw] * (ATTN_HEAD_DIM ** -0.5)).astype(BF16)
    k = proj[:, aw:2 * aw]
    k_ref[...] = k.astype(BF16)
    v_ref[...] = proj[:, 2 * aw:3 * aw].astype(BF16)
    xm_ref[...] = proj[:, 3 * aw:3 * aw + MLSTM_WIDTH]
    z_ref[...] = proj[:, 3 * aw + MLSTM_WIDTH:]
    for r in range(FFN_TOKENS // MOBA_BLOCK):
        km_ref[r] = jnp.mean(k[r * MOBA_BLOCK:(r + 1) * MOBA_BLOCK], axis=0, keepdims=True)


def _ffn1_proj(x2d, mod4, norm1_w, wg, wu, wd, norm2_w, w_in, seq):
    T = x2d.shape[0]
    tm = FFN_TOKENS
    tps = seq // tm
    nb = tm // MOBA_BLOCK
    tok = lambda w: pl.BlockSpec((tm, w), lambda i: (i, 0))
    return pl.pallas_call(
        _ffn1_proj_kernel,
        out_shape=(jax.ShapeDtypeStruct((T, D_MODEL), F32),
                   jax.ShapeDtypeStruct((T, ATTN_WIDTH), BF16),
                   jax.ShapeDtypeStruct((T, ATTN_WIDTH), BF16),
                   jax.ShapeDtypeStruct((T, ATTN_WIDTH), BF16),
                   jax.ShapeDtypeStruct((T, MLSTM_WIDTH), F32),
                   jax.ShapeDtypeStruct((T, MLSTM_WIDTH), F32),
                   jax.ShapeDtypeStruct((T // MOBA_BLOCK, 1, ATTN_WIDTH), F32)),
        grid=(T // tm,),
        in_specs=[tok(D_MODEL), _mod_spec(0, tps), _mod_spec(1, tps), _mod_spec(2, tps),
                  _resident((1, D_MODEL)),
                  _resident((D_MODEL, D_FF)), _resident((D_MODEL, D_FF)), _resident((D_FF, D_MODEL)),
                  _mod_spec(3, tps), _mod_spec(4, tps),
                  _resident((1, D_MODEL)), _resident((D_MODEL, IN_COLS))],
        out_specs=(tok(D_MODEL), tok(ATTN_WIDTH), tok(ATTN_WIDTH), tok(ATTN_WIDTH),
                   tok(MLSTM_WIDTH), tok(MLSTM_WIDTH),
                   pl.BlockSpec((nb, 1, ATTN_WIDTH), lambda i: (i, 0, 0))),
        scratch_shapes=[pltpu.VMEM((tm, D_FF), BF16)],
        compiler_params=pltpu.CompilerParams(dimension_semantics=("arbitrary",),
                                             vmem_limit_bytes=VMEM_LIMIT_BYTES),
        name="ffn1_proj",
    )(x2d, mod4, mod4, mod4, norm1_w, wg, wu, wd, mod4, mod4, norm2_w, w_in)


def _gate_transform(g, gate_idx):
    log_sig = jnp.minimum(g, 0.0) - jnp.log1p(jnp.exp(-jnp.abs(g)))
    return jnp.where(gate_idx >= MLSTM_HEADS, log_sig, g)


def _mlstm_kernel(xm_ref, z_ref, cw_ref, cb_ref, wqk_ref, wv_ref, wift_ref,
                  bifr_ref, mn_ref, ms_ref, o_ref, ct_ref, m_ref, carry_ref):
    L = MLSTM_CHUNK
    dh = MLSTM_HEAD_DIM
    H = MLSTM_HEADS

    @pl.when(pl.program_id(1) == 0)
    def _():
        ct_ref[...] = jnp.zeros_like(ct_ref)
        m_ref[...] = jnp.zeros_like(m_ref)
        carry_ref[...] = jnp.zeros_like(carry_ref)

    cw = cw_ref[...]
    row8 = lax.broadcasted_iota(jnp.int32, (SUBLANES, MLSTM_WIDTH), 0)
    s_idx = lax.broadcasted_iota(jnp.int32, (L, L), 0)
    t_idx = lax.broadcasted_iota(jnp.int32, (L, L), 1)
    causal_t = s_idx <= t_idx
    tri_t = causal_t.astype(F32)
    ones_rows = (lax.broadcasted_iota(jnp.int32, (MLSTM_STATE_ROWS - dh, L), 0) == 0).astype(F32)

    def front(sq):
        xm = xm_ref[sq]
        carry = carry_ref[sq]
        acc = xm * cw[CONV_WIDTH - 1:CONV_WIDTH] + cb_ref[...]
        for k in range(1, CONV_WIDTH):
            rolled = pltpu.roll(xm, k, 0)
            top = jnp.where(row8 < k, pltpu.roll(carry, k, 0), rolled[0:SUBLANES])
            shifted = jnp.concatenate([top, rolled[SUBLANES:]], axis=0)
            acc = acc + shifted * cw[CONV_WIDTH - 1 - k:CONV_WIDTH - k]
        carry_ref[sq] = xm[L - SUBLANES:L]
        xc = acc * _sigmoid(acc)

        xc_bf = xc.astype(BF16)
        xm_bf = xm.astype(BF16)
        qk, vv, cat = [], [], []
        for h in range(H):
            hs = slice(h * dh, (h + 1) * dh)
            qk_h = jnp.dot(xc_bf[:, hs], wqk_ref[h], preferred_element_type=F32)
            v_h = jnp.dot(xm_bf[:, hs], wv_ref[h], preferred_element_type=F32)
            qk.append(qk_h)
            vv.append(v_h)
            cat += [qk_h.astype(BF16), v_h.astype(BF16)]
        cat = jnp.concatenate(cat, axis=1)

        g_row = lax.dot_general(wift_ref[...], cat, _NT, preferred_element_type=F32) + bifr_ref[...]
        a_row = _gate_transform(g_row, lax.broadcasted_iota(jnp.int32, (2 * H, L), 0))
        b_row = jnp.dot(a_row, tri_t, preferred_element_type=F32, precision=lax.Precision.HIGHEST)
        c_row = jnp.concatenate([a_row[0:H] - b_row[H:2 * H], jnp.zeros((SUBLANES - H, L), F32)], axis=0)
        return dict(xc=xc, qk=qk, vv=vv, a_row=a_row, b_row=b_row, c_col=jnp.transpose(c_row))

    def head_scores(sq, h, f):
        qt_bf = jnp.transpose(f["qk"][h][:, :dh]).astype(BF16)
        k_bf = (f["qk"][h][:, dh:] * (dh ** -0.5)).astype(BF16)
        v_aug_t = jnp.concatenate([jnp.transpose(f["vv"][h]), ones_rows], axis=0)
        ct = ct_ref[sq, h]
        s_raw = jnp.dot(k_bf, qt_bf, preferred_element_type=F32)
        inter = jnp.dot(ct.astype(BF16), qt_bf, preferred_element_type=F32)
        return k_bf, v_aug_t, ct, s_raw, inter

    def head(sq, h, f, scores):
        hs = slice(h * dh, (h + 1) * dh)
        k_bf, v_aug_t, ct, s_raw, inter = scores
        c_c = f["c_col"][:, h:h + 1]
        b_r = f["b_row"][H + h:H + h + 1, :]
        i_r = f["a_row"][h:h + 1, :]
        m_prev = m_ref[sq, h, 0:1, 0:1]

        b_last = b_r[:, L - 1:L]
        log_g = b_last - b_r + i_r
        m_new = jnp.maximum(b_last + m_prev, jnp.max(log_g, axis=1, keepdims=True))
        w_g = jnp.exp(log_g - m_new)
        decay = jnp.exp(b_last + m_prev - m_new)
        upd = jnp.dot((v_aug_t * w_g).astype(BF16), k_bf, preferred_element_type=F32)
        ct_ref[sq, h] = decay * ct + upd
        m_ref[sq, h] = jnp.broadcast_to(m_new, (SUBLANES, LANES))

        log_d = jnp.where(causal_t, c_c + b_r, -jnp.inf)
        log_inter = b_r + m_prev
        m_t = jnp.maximum(log_inter, jnp.max(log_d, axis=0, keepdims=True))
        w_intra = jnp.exp(log_d - m_t)
        w_inter = jnp.exp(log_inter - m_t)
        s_t = s_raw * w_intra
        comb = (jnp.dot(v_aug_t.astype(BF16), s_t.astype(BF16), preferred_element_type=F32)
                + w_inter * inter)
        hh_t = comb[:dh] / jnp.maximum(jnp.abs(comb[dh:dh + 1]), jnp.exp(-m_t))

        mu = jnp.mean(hh_t, axis=0, keepdims=True)
        dlt = hh_t - mu
        var = jnp.mean(dlt * dlt, axis=0, keepdims=True)
        hn = jnp.transpose(dlt * lax.rsqrt(var + EPS)) * mn_ref[:, hs]
        zh = z_ref[sq, :, hs]
        return ((hn + ms_ref[:, hs] * f["xc"][:, hs]) * (zh * _sigmoid(zh))).astype(BF16)

    fronts = [front(sq) for sq in range(MLSTM_SEQS)]
    outs = [[] for _ in range(MLSTM_SEQS)]
    chains = [(sq, h) for h in range(H) for sq in range(MLSTM_SEQS)]
    pending = [head_scores(sq, h, fronts[sq]) for sq, h in chains[:MLSTM_LOOKAHEAD]]
    for idx, (sq, h) in enumerate(chains):
        scores = pending.pop(0)
        if idx + MLSTM_LOOKAHEAD < len(chains):
            nsq, nh = chains[idx + MLSTM_LOOKAHEAD]
            pending.append(head_scores(nsq, nh, fronts[nsq]))
        outs[sq].append(head(sq, h, fronts[sq], scores))
    for sq in range(MLSTM_SEQS):
        o_ref[sq] = jnp.concatenate(outs[sq], axis=1)


def _mlstm(xm, z, conv_w, conv_b, wqk, wv, wif_t, bif_r, mnorm, mskip):
    B, S, W = xm.shape
    L = MLSTM_CHUNK
    NS = MLSTM_SEQS
    assert B % NS == 0 and S % L == 0
    tok = pl.BlockSpec((NS, L, W), lambda b, j: (b, j, 0))
    return pl.pallas_call(
        _mlstm_kernel,
        out_shape=jax.ShapeDtypeStruct((B, S, W), BF16),
        grid=(B // NS, S // L),
        in_specs=[tok, tok,
                  _resident(conv_w.shape), _resident(conv_b.shape), _resident(wqk.shape), _resident(wv.shape),
                  _resident(wif_t.shape), _resident(bif_r.shape),
                  _resident(mnorm.shape), _resident(mskip.shape)],
        out_specs=tok,
        scratch_shapes=[pltpu.VMEM((NS, MLSTM_HEADS, MLSTM_STATE_ROWS, MLSTM_HEAD_DIM), F32),
                        pltpu.VMEM((NS, MLSTM_HEADS, SUBLANES, LANES), F32),
                        pltpu.VMEM((NS, SUBLANES, W), F32)],
        compiler_params=pltpu.CompilerParams(dimension_semantics=("arbitrary", "arbitrary"),
                                             vmem_limit_bytes=VMEM_LIMIT_BYTES),
        name="mlstm",
    )(xm, z, conv_w, conv_b, wqk, wv, wif_t, bif_r, mnorm, mskip)


def _moba_kernel(q_ref, k_ref, v_ref, km_ref, o_ref, lmin_ref, kaug_ref, vt_ref, qaug_ref, kn_ref, *, bounded):
    BS = MOBA_BLOCK
    NP = MOBA_PAIRS
    S = k_ref.shape[1]
    nb = km_ref.shape[1]
    hd = ATTN_HEAD_DIM
    i = pl.program_id(2)

    @pl.when(i == 0)
    def _():
        lane_s = lax.broadcasted_iota(jnp.int32, (S, LANES), 1)
        blk_s = lax.shift_right_logical(lax.broadcasted_iota(jnp.int32, (S, LANES), 0), BS.bit_length() - 1)
        k_tail = ((lane_s == blk_s) | (lane_s == nb)).astype(BF16)
        v_tail = (lax.broadcasted_iota(jnp.int32, (VT_ROWS - LANES, S), 0) == 0).astype(BF16)
        for p in range(NP):
            ps = slice(p * LANES, (p + 1) * LANES)
            kaug_ref[p, :, 0:LANES] = k_ref[0, :, ps]
            kaug_ref[p, :, LANES:] = k_tail
            vt_ref[p, 0:LANES, :] = jnp.transpose(v_ref[0, :, ps].astype(F32)).astype(BF16)
            vt_ref[p, LANES:, :] = v_tail
            if bounded:
                kf = k_ref[0, :, ps].astype(F32)
                head_of_dim = lax.shift_right_logical(lax.broadcasted_iota(jnp.int32, (LANES, LANES), 0),
                                                      hd.bit_length() - 1)
                ind = (head_of_dim == lax.broadcasted_iota(jnp.int32, (LANES, LANES), 1)).astype(BF16)
                ksq = jnp.dot((kf * kf).astype(BF16), ind, preferred_element_type=F32)
                kn_ref[p] = jnp.broadcast_to(jnp.max(ksq, axis=0, keepdims=True), kn_ref.shape[1:])

            km = km_ref[0, :, ps]
            km_hi = km.astype(BF16)
            km_lo = (km - km_hi.astype(F32)).astype(BF16)
            dim_t = lax.broadcasted_iota(jnp.int32, (LANES, 2 * BS), 0)
            col_t = lax.broadcasted_iota(jnp.int32, (LANES, 2 * BS), 1)
            in_head = (lax.shift_right_logical(dim_t, hd.bit_length() - 1)
                       == lax.shift_right_logical(col_t, BS.bit_length() - 1))
            blk_t = lax.broadcasted_iota(jnp.int32, (nb, 2 * BS), 0)
            pad_rows = lax.broadcasted_iota(jnp.int32, (LANES - nb, 2 * BS), 0)
            if bounded:
                ksq_max = jnp.where(col_t[0:1] < BS, kn_ref[p, 0:1, 0:1], kn_ref[p, 0:1, 1:2])
            for j in range(nb):
                q_j = jnp.transpose(q_ref[0, j * BS:(j + 1) * BS, ps].astype(F32))
                qq = jnp.where(in_head, jnp.concatenate([q_j, q_j], axis=1), 0.0)
                qq_bf = qq.astype(BF16)
                gate = (jnp.dot(km_hi, qq_bf, preferred_element_type=F32)
                        + jnp.dot(km_lo, qq_bf, preferred_element_type=F32))
                cur = jnp.where(blk_t < j, gate, -jnp.inf)
                bias_t = jnp.full((nb, 2 * BS), MASK_VALUE, F32)
                for _ in range(MOBA_TOPK):
                    mx = jnp.max(cur, axis=0, keepdims=True)
                    first = jnp.min(jnp.where(cur == mx, blk_t, nb), axis=0, keepdims=True)
                    pick = (blk_t == first) & (mx > -jnp.inf)
                    bias_t = jnp.where(pick, 0.0, bias_t)
                    cur = jnp.where(pick, -jnp.inf, cur)
                bias_t = jnp.where(blk_t == j, 0.0, bias_t)
                if bounded:
                    bound = jnp.sqrt(jnp.sum(qq * qq, axis=0, keepdims=True) * ksq_max) * MOBA_BOUND_SLACK
                    tail = jnp.where(pad_rows == 0, -bound, 0.0)
                else:
                    tail = jnp.zeros((LANES - nb, 2 * BS), F32)
                qaug_ref[p, j] = jnp.concatenate(
                    [qq_bf, bias_t.astype(BF16), tail.astype(BF16)], axis=0)

    future_key = (lax.broadcasted_iota(jnp.int32, (BS, 2 * BS), 0)
                  > (lax.broadcasted_iota(jnp.int32, (BS, 2 * BS), 1) & (BS - 1)))

    def scores(j):
        return [jnp.dot(kaug_ref[p, j * BS:(j + 1) * BS, :], qaug_ref[p, i], preferred_element_type=F32)
                for p in range(NP)]

    CB = MOBA_CASE_BLOCKS
    for c in range(nb // CB):
        @pl.when(lax.shift_right_logical(i, CB.bit_length() - 1) == c)
        def _():
            m = [None] * NP
            pv = [None] * NP
            order = list(range((c + 1) * CB - 1, -1, -1))
            pending = [scores(j) for j in order[:MOBA_LOOKAHEAD]]
            for idx, j in enumerate(order):
                s_all = pending.pop(0)
                if idx + MOBA_LOOKAHEAD < len(order):
                    pending.append(scores(order[idx + MOBA_LOOKAHEAD]))
                for p in range(NP):
                    s = s_all[p]
                    if j >= c * CB:
                        s = jnp.where(future_key, jnp.where(i == j, MASK_VALUE, s), s)
                    alpha = None
                    if not bounded:
                        cm = jnp.max(s, axis=0, keepdims=True)
                        if m[p] is not None:
                            cm = jnp.maximum(m[p], cm)
                            alpha = jnp.exp(m[p] - cm)
                        m[p] = cm
                        s = s - cm
                    d = jnp.dot(vt_ref[p, :, j * BS:(j + 1) * BS], jnp.exp(s).astype(BF16),
                                preferred_element_type=F32)
                    if pv[p] is None:
                        pv[p] = d
                    elif alpha is None:
                        pv[p] = pv[p] + d
                    else:
                        pv[p] = alpha * pv[p] + d
            outs, lmin = [], None
            for p in range(NP):
                denom = pv[p][LANES:LANES + 1]
                outs += [pv[p][0:hd, 0:BS] / denom[:, 0:BS], pv[p][hd:LANES, BS:] / denom[:, BS:]]
                dmin = jnp.min(denom, axis=1, keepdims=True)
                lmin = dmin if lmin is None else jnp.minimum(lmin, dmin)
            o_ref[0] = jnp.transpose(jnp.concatenate(outs, axis=0))
            lmin_ref[0] = jnp.broadcast_to(lmin, lmin_ref.shape[1:])


def _moba_call(q, k, v, kmean, bounded):
    B, S, W = q.shape
    BS = MOBA_BLOCK
    NP = MOBA_PAIRS
    nb = S // BS
    ngroups = W // (NP * LANES)
    assert nb < LANES - 1 and W % (NP * LANES) == 0
    qspec = pl.BlockSpec((1, BS, NP * LANES), lambda b, g, i: (b, i, g))
    kvspec = pl.BlockSpec((1, S, NP * LANES), lambda b, g, i: (b, 0, g))
    return pl.pallas_call(
        functools.partial(_moba_kernel, bounded=bounded),
        out_shape=(jax.ShapeDtypeStruct((B, S, W), F32),
                   jax.ShapeDtypeStruct((B * ngroups * nb, SUBLANES, LANES), F32)),
        grid=(B, ngroups, nb),
        in_specs=[kvspec, kvspec, kvspec, pl.BlockSpec((1, nb, NP * LANES), lambda b, g, i: (b, 0, g))],
        out_specs=(qspec, pl.BlockSpec((1, SUBLANES, LANES), lambda b, g, i: ((b * ngroups + g) * nb + i, 0, 0))),
        scratch_shapes=[pltpu.VMEM((NP, S, 2 * LANES), BF16),
                        pltpu.VMEM((NP, VT_ROWS, S), BF16),
                        pltpu.VMEM((NP, nb, 2 * LANES, 2 * BS), BF16),
                        pltpu.VMEM((NP, SUBLANES, LANES), F32)],
        compiler_params=pltpu.CompilerParams(dimension_semantics=("arbitrary",) * 3,
                                             vmem_limit_bytes=VMEM_LIMIT_BYTES),
        name="moba_bounded" if bounded else "moba_exact",
    )(q, k, v, kmean)


def _moba(q, k, v, kmean):
    o_fast, lmin = _moba_call(q, k, v, kmean, True)
    ok = jnp.all(lmin[:, 0, 0] > MOBA_MIN_DENOM)
    return lax.cond(ok, lambda: o_fast, lambda: _moba_call(q, k, v, kmean, False)[0])


def _out_ffn2_kernel(x1_ref, oa_ref, om_ref, g2_ref, sh_ref, sc_ref, g3_ref, an_ref, wo_ref,
                     nw_ref, wg_ref, wu_ref, wd_ref, fn_ref, o_ref, a_ref):
    oa = _rmsnorm(oa_ref[...], an_ref[...]).astype(BF16)
    mix = (jnp.dot(oa, wo_ref[0:ATTN_WIDTH, :], preferred_element_type=F32)
           + jnp.dot(om_ref[...], wo_ref[ATTN_WIDTH:, :], preferred_element_type=F32))
    x2 = x1_ref[...] + (1.0 + g2_ref[0, 0]) * mix
    h = _rms_mod(x2, nw_ref[...], sh_ref[0, 0], sc_ref[0, 0])
    y = _swiglu(h.astype(BF16), wg_ref, wu_ref, wd_ref, a_ref)
    x3 = x2 + (0.5 * (1.0 + g3_ref[0, 0])) * y
    o_ref[...] = _rmsnorm(x3, fn_ref[...])


def _out_ffn2(x1, oa, om, mod4, attn_norm, w_out, norm_w, wg, wu, wd, final_norm, seq):
    T = x1.shape[0]
    tm = FFN_TOKENS
    tps = seq // tm
    tok = lambda w: pl.BlockSpec((tm, w), lambda i: (i, 0))
    return pl.pallas_call(
        _out_ffn2_kernel,
        out_shape=jax.ShapeDtypeStruct((T, D_MODEL), F32),
        grid=(T // tm,),
        in_specs=[tok(D_MODEL), tok(ATTN_WIDTH), tok(MLSTM_WIDTH),
                  _mod_spec(5, tps), _mod_spec(6, tps), _mod_spec(7, tps), _mod_spec(8, tps),
                  _resident((1, ATTN_WIDTH)), _resident((D_MODEL, D_MODEL)), _resident((1, D_MODEL)),
                  _resident((D_MODEL, D_FF)), _resident((D_MODEL, D_FF)), _resident((D_FF, D_MODEL)),
                  _resident((1, D_MODEL))],
        out_specs=tok(D_MODEL),
        scratch_shapes=[pltpu.VMEM((tm, D_FF), BF16)],
        compiler_params=pltpu.CompilerParams(dimension_semantics=("arbitrary",),
                                             vmem_limit_bytes=VMEM_LIMIT_BYTES),
        name="out_ffn2",
    )(x1, oa, om, mod4, mod4, mod4, mod4, attn_norm, w_out, norm_w, wg, wu, wd, final_norm)


def kernel(x, c, w_ada, b_ada, ffn1_norm, ffn1_w_gate, ffn1_w_up, ffn1_w_down, mix_norm, w_in, conv_w, conv_b,
           w_q_m, w_k_m, w_v_m, w_if, b_if, mlstm_norm, mlstm_skip, attn_norm, w_out,
           ffn2_norm, ffn2_w_gate, ffn2_w_up, ffn2_w_down, final_norm):
    B, S, D = x.shape
    T = B * S
    assert w_ada.shape[0] == 1, "only depth 1 is supported"
    bf = lambda a: a.astype(BF16)
    xf = x.reshape(T, D)
    for l in range(1):
        mod4 = _adaln(c, w_ada[l], b_ada[l][None, :]).reshape(N_MOD, B, 1, D)
        x1, q, k, v, xm, z, kmean = _ffn1_proj(
            xf, mod4, ffn1_norm[l][None, :], bf(ffn1_w_gate[l]), bf(ffn1_w_up[l]), bf(ffn1_w_down[l]),
            mix_norm[l][None, :], bf(w_in[l]), S)
        o_m = _mlstm(xm.reshape(B, S, MLSTM_WIDTH), z.reshape(B, S, MLSTM_WIDTH),
                     conv_w[l], conv_b[l][None, :],
                     bf(jnp.concatenate([w_q_m[l], w_k_m[l]], axis=-1)), bf(w_v_m[l]),
                     bf(w_if[l].T), b_if[l][:, None],
                     mlstm_norm[l][None, :], mlstm_skip[l][None, :])
        o_a = _moba(q.reshape(B, S, ATTN_WIDTH), k.reshape(B, S, ATTN_WIDTH), v.reshape(B, S, ATTN_WIDTH),
                    kmean.reshape(B, S // MOBA_BLOCK, ATTN_WIDTH))
        xf = _out_ffn2(x1, o_a.reshape(T, ATTN_WIDTH), o_m.reshape(T, MLSTM_WIDTH), mod4,
                       attn_norm[l][None, :], bf(w_out[l]), ffn2_norm[l][None, :],
                       bf(ffn2_w_gate[l]), bf(ffn2_w_up[l]), bf(ffn2_w_down[l]), final_norm[None, :], S)
    return xf.reshape(B, S, D)
```

```python
import functools

import jax
import jax.numpy as jnp
from jax import lax
from jax.experimental import pallas as pl
from jax.experimental.pallas import tpu as pltpu

F32 = jnp.float32
BF16 = jnp.bfloat16

D_MODEL = 1024
ATTN_HEADS = 8
ATTN_WIDTH = 512
ATTN_HEAD_DIM = 64
MLSTM_HEADS = 4
MLSTM_WIDTH = 512
MLSTM_HEAD_DIM = 128
IN_COLS = 3 * ATTN_WIDTH + 2 * MLSTM_WIDTH
CONV_WIDTH = 4
MOBA_BLOCK = 256
MOBA_TOPK = 3
D_FF = 2816
N_MOD = 9
EPS = 1e-6

LANES = 128
SUBLANES = 8
VMEM_LIMIT_BYTES = 56 * 1024 * 1024

FFN_TOKENS = 512
MXU_WIDTH = 256
FFN_CHUNKS = ((0, 6 * MXU_WIDTH), (6 * MXU_WIDTH, D_FF))
MLSTM_CHUNK = 256
MLSTM_STATE_ROWS = MLSTM_HEAD_DIM + 16
MLSTM_LOOKAHEAD = 2
MLSTM_SEQS = 8
MOBA_PAIRS = 2
MOBA_LOOKAHEAD = 2
MOBA_CASE_BLOCKS = 2
MOBA_BOUND_SLACK = 1.0 + 2.0 ** -6
MOBA_MIN_DENOM = 1e-25
VT_ROWS = LANES + 16
MASK_VALUE = -1e30

_NT = (((1,), (1,)), ((), ()))
_TN = (((0,), (0,)), ((), ()))


def _sigmoid(x):
    return 1.0 / (1.0 + jnp.exp(-x))


def _rmsnorm(x, w):
    return x * lax.rsqrt(jnp.mean(x * x, axis=-1, keepdims=True) + EPS) * w


def _rms_mod(x, w, shift, scale):
    return _rmsnorm(x, w) * (1.0 + scale) + shift


def _swiglu(h_bf, wg_ref, wu_ref, wd_ref, a_ref):
    for lo, hi in FFN_CHUNKS:
        sl = slice(lo, hi)
        g = jnp.dot(h_bf, wg_ref[:, sl], preferred_element_type=F32)
        u = jnp.dot(h_bf, wu_ref[:, sl], preferred_element_type=F32)
        a_ref[:, sl] = (g * _sigmoid(g) * u).astype(BF16)
    return jnp.dot(a_ref[...], wd_ref[...], preferred_element_type=F32)


def _adaln_kernel(c_ref, w_ref, b_ref, o_ref):
    o_ref[0] = jnp.dot(c_ref[...].astype(BF16), w_ref[...].astype(BF16),
                       preferred_element_type=F32) + b_ref[...]


def _adaln(c, w_ada, b_ada):
    B = c.shape[0]
    return pl.pallas_call(
        _adaln_kernel,
        out_shape=jax.ShapeDtypeStruct((N_MOD, B, D_MODEL), F32),
        grid=(N_MOD,),
        in_specs=[
            pl.BlockSpec((B, D_MODEL), lambda j: (0, 0)),
            pl.BlockSpec((D_MODEL, D_MODEL), lambda j: (0, j)),
            pl.BlockSpec((1, D_MODEL), lambda j: (0, j)),
        ],
        out_specs=pl.BlockSpec((1, B, D_MODEL), lambda j: (j, 0, 0)),
        compiler_params=pltpu.CompilerParams(dimension_semantics=("arbitrary",),
                                             vmem_limit_bytes=VMEM_LIMIT_BYTES),
        name="adaln",
    )(c, w_ada, b_ada)


def _resident(shape):
    nd = len(shape)
    return pl.BlockSpec(shape, lambda *_: (0,) * nd, pipeline_mode=pl.Buffered(1))


def _mod_spec(k, tiles_per_seq):
    return pl.BlockSpec((1, 1, 1, D_MODEL), lambda i: (k, i // tiles_per_seq, 0, 0))


def _ffn1_proj_kernel(x_ref, sh1_ref, sc1_ref, g1_ref, nw1_ref, wg_ref, wu_ref, wd_ref,
                      sh2_ref, sc2_ref, nw2_ref, win_ref,
                      x1_ref, q_ref, k_ref, v_ref, xm_ref, z_ref, km_ref, a_ref):
    x = x_ref[...]
    h = _rms_mod(x, nw1_ref[...], sh1_ref[0, 0], sc1_ref[0, 0])
    y = _swiglu(h.astype(BF16), wg_ref, wu_ref, wd_ref, a_ref)
    x1 = x + (0.5 * (1.0 + g1_ref[0, 0])) * y
    x1_ref[...] = x1

    h = _rms_mod(x1, nw2_ref[...], sh2_ref[0, 0], sc2_ref[0, 0]).astype(BF16)
    proj = jnp.dot(h, win_ref[...], preferred_element_type=F32)
    aw = ATTN_WIDTH
    q_ref[...] = (proj[:, 0:aw] * (ATTN_HEAD_DIM ** -0.5)).astype(BF16)
    k = proj[:, aw:2 * aw]
    k_ref[...] = k.astype(BF16)
    v_ref[...] = proj[:, 2 * aw:3 * aw].astype(BF16)
    xm_ref[...] = proj[:, 3 * aw:3 * aw + MLSTM_WIDTH]
    z_ref[...] = proj[:, 3 * aw + MLSTM_WIDTH:]
    for r in range(FFN_TOKENS // MOBA_BLOCK):
        km_ref[r] = jnp.mean(k[r * MOBA_BLOCK:(r + 1) * MOBA_BLOCK], axis=0, keepdims=True)


def _ffn1_proj(x2d, mod4, norm1_w, wg, wu, wd, norm2_w, w_in, seq):
    T = x2d.shape[0]
    tm = FFN_TOKENS
    tps = seq // tm
    nb = tm // MOBA_BLOCK
    tok = lambda w: pl.BlockSpec((tm, w), lambda i: (i, 0))
    return pl.pallas_call(
        _ffn1_proj_kernel,
        out_shape=(jax.ShapeDtypeStruct((T, D_MODEL), F32),
                   jax.ShapeDtypeStruct((T, ATTN_WIDTH), BF16),
                   jax.ShapeDtypeStruct((T, ATTN_WIDTH), BF16),
                   jax.ShapeDtypeStruct((T, ATTN_WIDTH), BF16),
                   jax.ShapeDtypeStruct((T, MLSTM_WIDTH), F32),
                   jax.ShapeDtypeStruct((T, MLSTM_WIDTH), F32),
                   jax.ShapeDtypeStruct((T // MOBA_BLOCK, 1, ATTN_WIDTH), F32)),
        grid=(T // tm,),
        in_specs=[tok(D_MODEL), _mod_spec(0, tps), _mod_spec(1, tps), _mod_spec(2, tps),
                  _resident((1, D_MODEL)),
                  _resident((D_MODEL, D_FF)), _resident((D_MODEL, D_FF)), _resident((D_FF, D_MODEL)),
                  _mod_spec(3, tps), _mod_spec(4, tps),
                  _resident((1, D_MODEL)), _resident((D_MODEL, IN_COLS))],
        out_specs=(tok(D_MODEL), tok(ATTN_WIDTH), tok(ATTN_WIDTH), tok(ATTN_WIDTH),
                   tok(MLSTM_WIDTH), tok(MLSTM_WIDTH),
                   pl.BlockSpec((nb, 1, ATTN_WIDTH), lambda i: (i, 0, 0))),
        scratch_shapes=[pltpu.VMEM((tm, D_FF), BF16)],
        compiler_params=pltpu.CompilerParams(dimension_semantics=("arbitrary",),
                                             vmem_limit_bytes=VMEM_LIMIT_BYTES),
        name="ffn1_proj",
    )(x2d, mod4, mod4, mod4, norm1_w, wg, wu, wd, mod4, mod4, norm2_w, w_in)


def _gate_transform(g, gate_idx):
    log_sig = jnp.minimum(g, 0.0) - jnp.log1p(jnp.exp(-jnp.abs(g)))
    return jnp.where(gate_idx >= MLSTM_HEADS, log_sig, g)


def _mlstm_kernel(xm_ref, z_ref, cw_ref, cb_ref, wqk_ref, wv_ref, wift_ref,
                  bifr_ref, mn_ref, ms_ref, o_ref, ct_ref, m_ref, carry_ref):
    L = MLSTM_CHUNK
    dh = MLSTM_HEAD_DIM
    H = MLSTM_HEADS

    @pl.when(pl.program_id(1) == 0)
    def _():
        ct_ref[...] = jnp.zeros_like(ct_ref)
        m_ref[...] = jnp.zeros_like(m_ref)
        carry_ref[...] = jnp.zeros_like(carry_ref)

    cw = cw_ref[...]
    row8 = lax.broadcasted_iota(jnp.int32, (SUBLANES, MLSTM_WIDTH), 0)
    s_idx = lax.broadcasted_iota(jnp.int32, (L, L), 0)
    t_idx = lax.broadcasted_iota(jnp.int32, (L, L), 1)
    causal_t = s_idx <= t_idx
    tri_t = causal_t.astype(F32)
    ones_rows = (lax.broadcasted_iota(jnp.int32, (MLSTM_STATE_ROWS - dh, L), 0) == 0).astype(F32)

    def front_proj(sq):
        xm = xm_ref[sq]
        carry = carry_ref[sq]
        acc = xm * cw[CONV_WIDTH - 1:CONV_WIDTH] + cb_ref[...]
        for k in range(1, CONV_WIDTH):
            rolled = pltpu.roll(xm, k, 0)
            top = jnp.where(row8 < k, pltpu.roll(carry, k, 0), rolled[0:SUBLANES])
            shifted = jnp.concatenate([top, rolled[SUBLANES:]], axis=0)
            acc = acc + shifted * cw[CONV_WIDTH - 1 - k:CONV_WIDTH - k]
        carry_ref[sq] = xm[L - SUBLANES:L]
        xc = acc * _sigmoid(acc)

        xc_bf = xc.astype(BF16)
        xm_bf = xm.astype(BF16)
        qk, vv, cat = [], [], []
        for h in range(H):
            hs = slice(h * dh, (h + 1) * dh)
            qk_h = jnp.dot(xc_bf[:, hs], wqk_ref[h], preferred_element_type=F32)
            v_h = jnp.dot(xm_bf[:, hs], wv_ref[h], preferred_element_type=F32)
            qk.append(qk_h)
            vv.append(v_h)
            cat += [qk_h.astype(BF16), v_h.astype(BF16)]
        return dict(xc=xc, qk=qk, vv=vv, cat=jnp.concatenate(cat, axis=1))

    def front_gates(f):
        g_row = lax.dot_general(wift_ref[...], f["cat"], _NT, preferred_element_type=F32) + bifr_ref[...]
        f["a_row"] = _gate_transform(g_row, lax.broadcasted_iota(jnp.int32, (2 * H, L), 0))

    def front_cumsum(f):
        a_row = f["a_row"]
        b_row = jnp.dot(a_row, tri_t, preferred_element_type=F32, precision=lax.Precision.HIGHEST)
        c_row = jnp.concatenate([a_row[0:H] - b_row[H:2 * H], jnp.zeros((SUBLANES - H, L), F32)], axis=0)
        f["b_row"] = b_row
        f["c_col"] = jnp.transpose(c_row)

    def head_scores(sq, h, f):
        qt_bf = jnp.transpose(f["qk"][h][:, :dh]).astype(BF16)
        k_bf = (f["qk"][h][:, dh:] * (dh ** -0.5)).astype(BF16)
        v_aug_t = jnp.concatenate([jnp.transpose(f["vv"][h]), ones_rows], axis=0)
        ct = ct_ref[sq, h]
        s_raw = jnp.dot(k_bf, qt_bf, preferred_element_type=F32)
        inter = jnp.dot(ct.astype(BF16), qt_bf, preferred_element_type=F32)
        return k_bf, v_aug_t, ct, s_raw, inter

    def head(sq, h, f, scores):
        hs = slice(h * dh, (h + 1) * dh)
        k_bf, v_aug_t, ct, s_raw, inter = scores
        c_c = f["c_col"][:, h:h + 1]
        b_r = f["b_row"][H + h:H + h + 1, :]
        i_r = f["a_row"][h:h + 1, :]
        m_prev = m_ref[sq, h, 0:1, 0:1]

        b_last = b_r[:, L - 1:L]
        log_g = b_last - b_r + i_r
        m_new = jnp.maximum(b_last + m_prev, jnp.max(log_g, axis=1, keepdims=True))
        w_g = jnp.exp(log_g - m_new)
        decay = jnp.exp(b_last + m_prev - m_new)
        upd = jnp.dot((v_aug_t * w_g).astype(BF16), k_bf, preferred_element_type=F32)
        ct_ref[sq, h] = decay * ct + upd
        m_ref[sq, h] = jnp.broadcast_to(m_new, (SUBLANES, LANES))

        log_d = jnp.where(causal_t, c_c + b_r, -jnp.inf)
        log_inter = b_r + m_prev
        m_t = jnp.maximum(log_inter, jnp.max(log_d, axis=0, keepdims=True))
        w_intra = jnp.exp(log_d - m_t)
        w_inter = jnp.exp(log_inter - m_t)
        s_t = s_raw * w_intra
        comb = (jnp.dot(v_aug_t.astype(BF16), s_t.astype(BF16), preferred_element_type=F32)
                + w_inter * inter)
        hh_t = comb[:dh] / jnp.maximum(jnp.abs(comb[dh:dh + 1]), jnp.exp(-m_t))

        mu = jnp.mean(hh_t, axis=0, keepdims=True)
        dlt = hh_t - mu
        var = jnp.mean(dlt * dlt, axis=0, keepdims=True)
        hn = jnp.transpose(dlt * lax.rsqrt(var + EPS)) * mn_ref[:, hs]
        zh = z_ref[sq, :, hs]
        return ((hn + ms_ref[:, hs] * f["xc"][:, hs]) * (zh * _sigmoid(zh))).astype(BF16)

    fronts = [front_proj(sq) for sq in range(MLSTM_SEQS)]
    for f in fronts:
        front_gates(f)
    for f in fronts:
        front_cumsum(f)
    outs = [[] for _ in range(MLSTM_SEQS)]
    chains = [(sq, h) for h in range(H) for sq in range(MLSTM_SEQS)]
    pending = [head_scores(sq, h, fronts[sq]) for sq, h in chains[:MLSTM_LOOKAHEAD]]
    for idx, (sq, h) in enumerate(chains):
        scores = pending.pop(0)
        if idx + MLSTM_LOOKAHEAD < len(chains):
            nsq, nh = chains[idx + MLSTM_LOOKAHEAD]
            pending.append(head_scores(nsq, nh, fronts[nsq]))
        outs[sq].append(head(sq, h, fronts[sq], scores))
    for sq in range(MLSTM_SEQS):
        o_ref[sq] = jnp.concatenate(outs[sq], axis=1)


def _mlstm(xm, z, conv_w, conv_b, wqk, wv, wif_t, bif_r, mnorm, mskip):
    B, S, W = xm.shape
    L = MLSTM_CHUNK
    NS = MLSTM_SEQS
    assert B % NS == 0 and S % L == 0
    tok = pl.BlockSpec((NS, L, W), lambda b, j: (b, j, 0))
    return pl.pallas_call(
        _mlstm_kernel,
        out_shape=jax.ShapeDtypeStruct((B, S, W), BF16),
        grid=(B // NS, S // L),
        in_specs=[tok, tok,
                  _resident(conv_w.shape), _resident(conv_b.shape), _resident(wqk.shape), _resident(wv.shape),
                  _resident(wif_t.shape), _resident(bif_r.shape),
                  _resident(mnorm.shape), _resident(mskip.shape)],
        out_specs=tok,
        scratch_shapes=[pltpu.VMEM((NS, MLSTM_HEADS, MLSTM_STATE_ROWS, MLSTM_HEAD_DIM), F32),
                        pltpu.VMEM((NS, MLSTM_HEADS, SUBLANES, LANES), F32),
                        pltpu.VMEM((NS, SUBLANES, W), F32)],
        compiler_params=pltpu.CompilerParams(dimension_semantics=("arbitrary", "arbitrary"),
                                             vmem_limit_bytes=VMEM_LIMIT_BYTES),
        name="mlstm",
    )(xm, z, conv_w, conv_b, wqk, wv, wif_t, bif_r, mnorm, mskip)


def _moba_kernel(q_ref, k_ref, v_ref, km_ref, o_ref, lmin_ref, kaug_ref, vt_ref, qaug_ref, kn_ref, *, bounded):
    BS = MOBA_BLOCK
    NP = MOBA_PAIRS
    S = k_ref.shape[1]
    nb = km_ref.shape[1]
    hd = ATTN_HEAD_DIM
    i = pl.program_id(2)

    @pl.when(i == 0)
    def _():
        lane_s = lax.broadcasted_iota(jnp.int32, (S, LANES), 1)
        blk_s = lax.shift_right_logical(lax.broadcasted_iota(jnp.int32, (S, LANES), 0), BS.bit_length() - 1)
        k_tail = ((lane_s == blk_s) | (lane_s == nb)).astype(BF16)
        v_tail = (lax.broadcasted_iota(jnp.int32, (VT_ROWS - LANES, S), 0) == 0).astype(BF16)
        for p in range(NP):
            ps = slice(p * LANES, (p + 1) * LANES)
            kaug_ref[p, :, 0:LANES] = k_ref[0, :, ps]
            kaug_ref[p, :, LANES:] = k_tail
            vt_ref[p, 0:LANES, :] = jnp.transpose(v_ref[0, :, ps].astype(F32)).astype(BF16)
            vt_ref[p, LANES:, :] = v_tail
            if bounded:
                kf = k_ref[0, :, ps].astype(F32)
                head_of_dim = lax.shift_right_logical(lax.broadcasted_iota(jnp.int32, (LANES, LANES), 0),
                                                      hd.bit_length() - 1)
                ind = (head_of_dim == lax.broadcasted_iota(jnp.int32, (LANES, LANES), 1)).astype(BF16)
                ksq = jnp.dot((kf * kf).astype(BF16), ind, preferred_element_type=F32)
                kn_ref[p] = jnp.broadcast_to(jnp.max(ksq, axis=0, keepdims=True), kn_ref.shape[1:])

            km = km_ref[0, :, ps]
            km_hi = km.astype(BF16)
            km_lo = (km - km_hi.astype(F32)).astype(BF16)
            dim_t = lax.broadcasted_iota(jnp.int32, (LANES, 2 * BS), 0)
            col_t = lax.broadcasted_iota(jnp.int32, (LANES, 2 * BS), 1)
            in_head = (lax.shift_right_logical(dim_t, hd.bit_length() - 1)
                       == lax.shift_right_logical(col_t, BS.bit_length() - 1))
            blk_t = lax.broadcasted_iota(jnp.int32, (nb, 2 * BS), 0)
            pad_rows = lax.broadcasted_iota(jnp.int32, (LANES - nb, 2 * BS), 0)
            if bounded:
                ksq_max = jnp.where(col_t[0:1] < BS, kn_ref[p, 0:1, 0:1], kn_ref[p, 0:1, 1:2])
            for j in range(nb):
                q_j = jnp.transpose(q_ref[0, j * BS:(j + 1) * BS, ps].astype(F32))
                qq = jnp.where(in_head, jnp.concatenate([q_j, q_j], axis=1), 0.0)
                qq_bf = qq.astype(BF16)
                gate = (jnp.dot(km_hi, qq_bf, preferred_element_type=F32)
                        + jnp.dot(km_lo, qq_bf, preferred_element_type=F32))
                cur = jnp.where(blk_t < j, gate, -jnp.inf)
                bias_t = jnp.full((nb, 2 * BS), MASK_VALUE, F32)
                for _ in range(MOBA_TOPK):
                    mx = jnp.max(cur, axis=0, keepdims=True)
                    first = jnp.min(jnp.where(cur == mx, blk_t, nb), axis=0, keepdims=True)
                    pick = (blk_t == first) & (mx > -jnp.inf)
                    bias_t = jnp.where(pick, 0.0, bias_t)
                    cur = jnp.where(pick, -jnp.inf, cur)
                bias_t = jnp.where(blk_t == j, 0.0, bias_t)
                if bounded:
                    bound = jnp.sqrt(jnp.sum(qq * qq, axis=0, keepdims=True) * ksq_max) * MOBA_BOUND_SLACK
                    tail = jnp.where(pad_rows == 0, -bound, 0.0)
                else:
                    tail = jnp.zeros((LANES - nb, 2 * BS), F32)
                qaug_ref[p, j] = jnp.concatenate(
                    [qq_bf, bias_t.astype(BF16), tail.astype(BF16)], axis=0)

    future_key = (lax.broadcasted_iota(jnp.int32, (BS, 2 * BS), 0)
                  > (lax.broadcasted_iota(jnp.int32, (BS, 2 * BS), 1) & (BS - 1)))

    def scores(j):
        return [jnp.dot(kaug_ref[p, j * BS:(j + 1) * BS, :], qaug_ref[p, i], preferred_element_type=F32)
                for p in range(NP)]

    CB = MOBA_CASE_BLOCKS
    for c in range(nb // CB):
        @pl.when(lax.shift_right_logical(i, CB.bit_length() - 1) == c)
        def _():
            m = [None] * NP
            pv = [None] * NP
            order = list(range((c + 1) * CB - 1, -1, -1))
            pending = [scores(j) for j in order[:MOBA_LOOKAHEAD]]
            for idx, j in enumerate(order):
                s_all = pending.pop(0)
                if idx + MOBA_LOOKAHEAD < len(order):
                    pending.append(scores(order[idx + MOBA_LOOKAHEAD]))
                for p in range(NP):
                    s = s_all[p]
                    if j >= c * CB:
                        s = jnp.where(future_key, jnp.where(i == j, MASK_VALUE, s), s)
                    alpha = None
                    if not bounded:
                        cm = jnp.max(s, axis=0, keepdims=True)
                        if m[p] is not None:
                            cm = jnp.maximum(m[p], cm)
                            alpha = jnp.exp(m[p] - cm)
                        m[p] = cm
                        s = s - cm
                    d = jnp.dot(vt_ref[p, :, j * BS:(j + 1) * BS], jnp.exp(s).astype(BF16),
                                preferred_element_type=F32)
                    if pv[p] is None:
                        pv[p] = d
                    elif alpha is None:
                        pv[p] = pv[p] + d
                    else:
                        pv[p] = alpha * pv[p] + d
            outs, lmin = [], None
            for p in range(NP):
                denom = pv[p][LANES:LANES + 1]
                outs += [pv[p][0:hd, 0:BS] / denom[:, 0:BS], pv[p][hd:LANES, BS:] / denom[:, BS:]]
                dmin = jnp.min(denom, axis=1, keepdims=True)
                lmin = dmin if lmin is None else jnp.minimum(lmin, dmin)
            o_ref[0] = jnp.transpose(jnp.concatenate(outs, axis=0))
            lmin_ref[0] = jnp.broadcast_to(lmin, lmin_ref.shape[1:])


def _moba_call(q, k, v, kmean, bounded):
    B, S, W = q.shape
    BS = MOBA_BLOCK
    NP = MOBA_PAIRS
    nb = S // BS
    ngroups = W // (NP * LANES)
    assert nb < LANES - 1 and W % (NP * LANES) == 0
    qspec = pl.BlockSpec((1, BS, NP * LANES), lambda b, g, i: (b, i, g))
    kvspec = pl.BlockSpec((1, S, NP * LANES), lambda b, g, i: (b, 0, g))
    return pl.pallas_call(
        functools.partial(_moba_kernel, bounded=bounded),
        out_shape=(jax.ShapeDtypeStruct((B, S, W), F32),
                   jax.ShapeDtypeStruct((B * ngroups * nb, SUBLANES, LANES), F32)),
        grid=(B, ngroups, nb),
        in_specs=[kvspec, kvspec, kvspec, pl.BlockSpec((1, nb, NP * LANES), lambda b, g, i: (b, 0, g))],
        out_specs=(qspec, pl.BlockSpec((1, SUBLANES, LANES), lambda b, g, i: ((b * ngroups + g) * nb + i, 0, 0))),
        scratch_shapes=[pltpu.VMEM((NP, S, 2 * LANES), BF16),
                        pltpu.VMEM((NP, VT_ROWS, S), BF16),
                        pltpu.VMEM((NP, nb, 2 * LANES, 2 * BS), BF16),
                        pltpu.VMEM((NP, SUBLANES, LANES), F32)],
        compiler_params=pltpu.CompilerParams(dimension_semantics=("arbitrary",) * 3,
                                             vmem_limit_bytes=VMEM_LIMIT_BYTES),
        name="moba_bounded" if bounded else "moba_exact",
    )(q, k, v, kmean)


def _moba(q, k, v, kmean):
    o_fast, lmin = _moba_call(q, k, v, kmean, True)
    ok = jnp.all(lmin[:, 0, 0] > MOBA_MIN_DENOM)
    return lax.cond(ok, lambda: o_fast, lambda: _moba_call(q, k, v, kmean, False)[0])


def _out_ffn2_kernel(x1_ref, oa_ref, om_ref, g2_ref, sh_ref, sc_ref, g3_ref, an_ref, wo_ref,
                     nw_ref, wg_ref, wu_ref, wd_ref, fn_ref, o_ref, a_ref):
    oa = _rmsnorm(oa_ref[...], an_ref[...]).astype(BF16)
    mix = (jnp.dot(oa, wo_ref[0:ATTN_WIDTH, :], preferred_element_type=F32)
           + jnp.dot(om_ref[...], wo_ref[ATTN_WIDTH:, :], preferred_element_type=F32))
    x2 = x1_ref[...] + (1.0 + g2_ref[0, 0]) * mix
    h = _rms_mod(x2, nw_ref[...], sh_ref[0, 0], sc_ref[0, 0])
    y = _swiglu(h.astype(BF16), wg_ref, wu_ref, wd_ref, a_ref)
    x3 = x2 + (0.5 * (1.0 + g3_ref[0, 0])) * y
    o_ref[...] = _rmsnorm(x3, fn_ref[...])


def _out_ffn2(x1, oa, om, mod4, attn_norm, w_out, norm_w, wg, wu, wd, final_norm, seq):
    T = x1.shape[0]
    tm = FFN_TOKENS
    tps = seq // tm
    tok = lambda w: pl.BlockSpec((tm, w), lambda i: (i, 0))
    return pl.pallas_call(
        _out_ffn2_kernel,
        out_shape=jax.ShapeDtypeStruct((T, D_MODEL), F32),
        grid=(T // tm,),
        in_specs=[tok(D_MODEL), tok(ATTN_WIDTH), tok(MLSTM_WIDTH),
                  _mod_spec(5, tps), _mod_spec(6, tps), _mod_spec(7, tps), _mod_spec(8, tps),
                  _resident((1, ATTN_WIDTH)), _resident((D_MODEL, D_MODEL)), _resident((1, D_MODEL)),
                  _resident((D_MODEL, D_FF)), _resident((D_MODEL, D_FF)), _resident((D_FF, D_MODEL)),
                  _resident((1, D_MODEL))],
        out_specs=tok(D_MODEL),
        scratch_shapes=[pltpu.VMEM((tm, D_FF), BF16)],
        compiler_params=pltpu.CompilerParams(dimension_semantics=("arbitrary",),
                                             vmem_limit_bytes=VMEM_LIMIT_BYTES),
        name="out_ffn2",
    )(x1, oa, om, mod4, mod4, mod4, mod4, attn_norm, w_out, norm_w, wg, wu, wd, final_norm)


def kernel(x, c, w_ada, b_ada, ffn1_norm, ffn1_w_gate, ffn1_w_up, ffn1_w_down, mix_norm, w_in, conv_w, conv_b,
           w_q_m, w_k_m, w_v_m, w_if, b_if, mlstm_norm, mlstm_skip, attn_norm, w_out,
           ffn2_norm, ffn2_w_gate, ffn2_w_up, ffn2_w_down, final_norm):
    B, S, D = x.shape
    T = B * S
    assert w_ada.shape[0] == 1, "only depth 1 is supported"
    bf = lambda a: a.astype(BF16)
    xf = x.reshape(T, D)
    for l in range(1):
        mod4 = _adaln(c, w_ada[l], b_ada[l][None, :]).reshape(N_MOD, B, 1, D)
        x1, q, k, v, xm, z, kmean = _ffn1_proj(
            xf, mod4, ffn1_norm[l][None, :], bf(ffn1_w_gate[l]), bf(ffn1_w_up[l]), bf(ffn1_w_down[l]),
            mix_norm[l][None, :], bf(w_in[l]), S)
        o_m = _mlstm(xm.reshape(B, S, MLSTM_WIDTH), z.reshape(B, S, MLSTM_WIDTH),
                     conv_w[l], conv_b[l][None, :],
                     bf(jnp.concatenate([w_q_m[l], w_k_m[l]], axis=-1)), bf(w_v_m[l]),
                     bf(w_if[l].T), b_if[l][:, None],
                     mlstm_norm[l][None, :], mlstm_skip[l][None, :])
        o_a = _moba(q.reshape(B, S, ATTN_WIDTH), k.reshape(B, S, ATTN_WIDTH), v.reshape(B, S, ATTN_WIDTH),
                    kmean.reshape(B, S // MOBA_BLOCK, ATTN_WIDTH))
        xf = _out_ffn2(x1, o_a.reshape(T, ATTN_WIDTH), o_m.reshape(T, MLSTM_WIDTH), mod4,
                       attn_norm[l][None, :], bf(w_out[l]), ffn2_norm[l][None, :],
                       bf(ffn2_w_gate[l]), bf(ffn2_w_up[l]), bf(ffn2_w_down[l]), final_norm[None, :], S)
    return xf.reshape(B, S, D)
```

```python
import functools

import jax
import jax.numpy as jnp
from jax import lax
from jax.experimental import pallas as pl
from jax.experimental.pallas import tpu as pltpu

F32 = jnp.float32
BF16 = jnp.bfloat16

D_MODEL = 1024
ATTN_HEADS = 8
ATTN_WIDTH = 512
ATTN_HEAD_DIM = 64
MLSTM_HEADS = 4
MLSTM_WIDTH = 512
MLSTM_HEAD_DIM = 128
IN_COLS = 3 * ATTN_WIDTH + 2 * MLSTM_WIDTH
CONV_WIDTH = 4
MOBA_BLOCK = 256
MOBA_TOPK = 3
D_FF = 2816
N_MOD = 9
EPS = 1e-6

LANES = 128
SUBLANES = 8
VMEM_LIMIT_BYTES = 56 * 1024 * 1024

FFN_TOKENS = 512
MXU_WIDTH = 256
FFN_CHUNKS = ((0, 6 * MXU_WIDTH), (6 * MXU_WIDTH, D_FF))
MLSTM_CHUNK = 256
MLSTM_STATE_ROWS = MLSTM_HEAD_DIM + 16
MLSTM_LOOKAHEAD = 2
MLSTM_SEQS = 8
MOBA_PAIRS = 2
MOBA_LOOKAHEAD = 2
MOBA_CASE_BLOCKS = 2
MOBA_BOUND_SLACK = 1.0 + 2.0 ** -6
MOBA_MIN_DENOM = 1e-25
VT_ROWS = LANES + 16
MASK_VALUE = -1e30

_NT = (((1,), (1,)), ((), ()))


def _sigmoid(x):
    return 1.0 / (1.0 + jnp.exp(-x))


def _rmsnorm(x, w):
    return x * lax.rsqrt(jnp.mean(x * x, axis=-1, keepdims=True) + EPS) * w


def _rms_mod(x, w, shift, scale):
    return _rmsnorm(x, w) * (1.0 + scale) + shift


def _swiglu(h_bf, wg_ref, wu_ref, wd_ref, a_ref):
    for lo, hi in FFN_CHUNKS:
        sl = slice(lo, hi)
        g = jnp.dot(h_bf, wg_ref[:, sl], preferred_element_type=F32)
        u = jnp.dot(h_bf, wu_ref[:, sl], preferred_element_type=F32)
        a_ref[:, sl] = (g * _sigmoid(g) * u).astype(BF16)
    return jnp.dot(a_ref[...], wd_ref[...], preferred_element_type=F32)


def _adaln_kernel(c_ref, w_ref, b_ref, o_ref):
    o_ref[0] = jnp.dot(c_ref[...].astype(BF16), w_ref[...].astype(BF16),
                       preferred_element_type=F32) + b_ref[...]


def _adaln(c, w_ada, b_ada):
    B = c.shape[0]
    return pl.pallas_call(
        _adaln_kernel,
        out_shape=jax.ShapeDtypeStruct((N_MOD, B, D_MODEL), F32),
        grid=(N_MOD,),
        in_specs=[
            pl.BlockSpec((B, D_MODEL), lambda j: (0, 0)),
            pl.BlockSpec((D_MODEL, D_MODEL), lambda j: (0, j)),
            pl.BlockSpec((1, D_MODEL), lambda j: (0, j)),
        ],
        out_specs=pl.BlockSpec((1, B, D_MODEL), lambda j: (j, 0, 0)),
        compiler_params=pltpu.CompilerParams(dimension_semantics=("arbitrary",),
                                             vmem_limit_bytes=VMEM_LIMIT_BYTES),
        name="adaln",
    )(c, w_ada, b_ada)


def _resident(shape):
    nd = len(shape)
    return pl.BlockSpec(shape, lambda *_: (0,) * nd, pipeline_mode=pl.Buffered(1))


def _mod_spec(k, tiles_per_seq):
    return pl.BlockSpec((1, 1, 1, D_MODEL), lambda i: (k, i // tiles_per_seq, 0, 0))


def _ffn1_proj_kernel(x_ref, sh1_ref, sc1_ref, g1_ref, nw1_ref, wg_ref, wu_ref, wd_ref,
                      sh2_ref, sc2_ref, nw2_ref, win_ref,
                      x1_ref, q_ref, k_ref, v_ref, xm_ref, z_ref, km_ref, a_ref):
    x = x_ref[...]
    h = _rms_mod(x, nw1_ref[...], sh1_ref[0, 0], sc1_ref[0, 0])
    y = _swiglu(h.astype(BF16), wg_ref, wu_ref, wd_ref, a_ref)
    x1 = x + (0.5 * (1.0 + g1_ref[0, 0])) * y
    x1_ref[...] = x1

    h = _rms_mod(x1, nw2_ref[...], sh2_ref[0, 0], sc2_ref[0, 0]).astype(BF16)
    proj = jnp.dot(h, win_ref[...], preferred_element_type=F32)
    aw = ATTN_WIDTH
    q_ref[...] = (proj[:, 0:aw] * (ATTN_HEAD_DIM ** -0.5)).astype(BF16)
    k = proj[:, aw:2 * aw]
    k_ref[...] = k.astype(BF16)
    v_ref[...] = proj[:, 2 * aw:3 * aw].astype(BF16)
    xm_ref[...] = proj[:, 3 * aw:3 * aw + MLSTM_WIDTH]
    z_ref[...] = proj[:, 3 * aw + MLSTM_WIDTH:]
    for r in range(FFN_TOKENS // MOBA_BLOCK):
        km_ref[r] = jnp.mean(k[r * MOBA_BLOCK:(r + 1) * MOBA_BLOCK], axis=0, keepdims=True)


def _ffn1_proj(x2d, mod4, norm1_w, wg, wu, wd, norm2_w, w_in, seq):
    T = x2d.shape[0]
    tm = FFN_TOKENS
    tps = seq // tm
    nb = tm // MOBA_BLOCK
    tok = lambda w: pl.BlockSpec((tm, w), lambda i: (i, 0))
    return pl.pallas_call(
        _ffn1_proj_kernel,
        out_shape=(jax.ShapeDtypeStruct((T, D_MODEL), F32),
                   jax.ShapeDtypeStruct((T, ATTN_WIDTH), BF16),
                   jax.ShapeDtypeStruct((T, ATTN_WIDTH), BF16),
                   jax.ShapeDtypeStruct((T, ATTN_WIDTH), BF16),
                   jax.ShapeDtypeStruct((T, MLSTM_WIDTH), F32),
                   jax.ShapeDtypeStruct((T, MLSTM_WIDTH), F32),
                   jax.ShapeDtypeStruct((T // MOBA_BLOCK, 1, ATTN_WIDTH), F32)),
        grid=(T // tm,),
        in_specs=[tok(D_MODEL), _mod_spec(0, tps), _mod_spec(1, tps), _mod_spec(2, tps),
                  _resident((1, D_MODEL)),
                  _resident((D_MODEL, D_FF)), _resident((D_MODEL, D_FF)), _resident((D_FF, D_MODEL)),
                  _mod_spec(3, tps), _mod_spec(4, tps),
                  _resident((1, D_MODEL)), _resident((D_MODEL, IN_COLS))],
        out_specs=(tok(D_MODEL), tok(ATTN_WIDTH), tok(ATTN_WIDTH), tok(ATTN_WIDTH),
                   tok(MLSTM_WIDTH), tok(MLSTM_WIDTH),
                   pl.BlockSpec((nb, 1, ATTN_WIDTH), lambda i: (i, 0, 0))),
        scratch_shapes=[pltpu.VMEM((tm, D_FF), BF16)],
        compiler_params=pltpu.CompilerParams(dimension_semantics=("arbitrary",),
                                             vmem_limit_bytes=VMEM_LIMIT_BYTES),
        name="ffn1_proj",
    )(x2d, mod4, mod4, mod4, norm1_w, wg, wu, wd, mod4, mod4, norm2_w, w_in)


def _gate_transform(g, gate_idx):
    log_sig = jnp.minimum(g, 0.0) - jnp.log1p(jnp.exp(-jnp.abs(g)))
    return jnp.where(gate_idx >= MLSTM_HEADS, log_sig, g)


def _mlstm_kernel(xm_ref, z_ref, cw_ref, cb_ref, wqk_ref, wv_ref, wift_ref,
                  bifr_ref, mn_ref, ms_ref, o_ref, ct_ref, m_ref, carry_ref):
    L = MLSTM_CHUNK
    dh = MLSTM_HEAD_DIM
    H = MLSTM_HEADS

    @pl.when(pl.program_id(1) == 0)
    def _():
        ct_ref[...] = jnp.zeros_like(ct_ref)
        m_ref[...] = jnp.zeros_like(m_ref)
        carry_ref[...] = jnp.zeros_like(carry_ref)

    cw = cw_ref[...]
    row8 = lax.broadcasted_iota(jnp.int32, (SUBLANES, MLSTM_WIDTH), 0)
    s_idx = lax.broadcasted_iota(jnp.int32, (L, L), 0)
    t_idx = lax.broadcasted_iota(jnp.int32, (L, L), 1)
    causal_t = s_idx <= t_idx
    tri_t = causal_t.astype(F32)
    ones_rows = (lax.broadcasted_iota(jnp.int32, (MLSTM_STATE_ROWS - dh, L), 0) == 0).astype(F32)

    def front_proj(sq):
        xm = xm_ref[sq]
        carry = carry_ref[sq]
        acc = xm * cw[CONV_WIDTH - 1:CONV_WIDTH] + cb_ref[...]
        for k in range(1, CONV_WIDTH):
            rolled = pltpu.roll(xm, k, 0)
            top = jnp.where(row8 < k, pltpu.roll(carry, k, 0), rolled[0:SUBLANES])
            shifted = jnp.concatenate([top, rolled[SUBLANES:]], axis=0)
            acc = acc + shifted * cw[CONV_WIDTH - 1 - k:CONV_WIDTH - k]
        carry_ref[sq] = xm[L - SUBLANES:L]
        xc = acc * _sigmoid(acc)

        xc_bf = xc.astype(BF16)
        xm_bf = xm.astype(BF16)
        qk, vv, cat = [], [], []
        for h in range(H):
            hs = slice(h * dh, (h + 1) * dh)
            qk_h = jnp.dot(xc_bf[:, hs], wqk_ref[h], preferred_element_type=F32)
            v_h = jnp.dot(xm_bf[:, hs], wv_ref[h], preferred_element_type=F32)
            qk.append(qk_h)
            vv.append(v_h)
            cat += [qk_h.astype(BF16), v_h.astype(BF16)]
        return dict(xc=xc, qk=qk, vv=vv, cat=jnp.concatenate(cat, axis=1))

    def front_gates(f):
        g_row = lax.dot_general(wift_ref[...], f["cat"], _NT, preferred_element_type=F32) + bifr_ref[...]
        f["a_row"] = _gate_transform(g_row, lax.broadcasted_iota(jnp.int32, (2 * H, L), 0))

    def front_cumsum(f):
        a_row = f["a_row"]
        b_row = jnp.dot(a_row, tri_t, preferred_element_type=F32, precision=lax.Precision.HIGHEST)
        c_row = jnp.concatenate([a_row[0:H] - b_row[H:2 * H], jnp.zeros((SUBLANES - H, L), F32)], axis=0)
        f["b_row"] = b_row
        f["c_col"] = jnp.transpose(c_row)

    def head_scores(sq, h, f):
        qt_bf = jnp.transpose(f["qk"][h][:, :dh]).astype(BF16)
        k_bf = (f["qk"][h][:, dh:] * (dh ** -0.5)).astype(BF16)
        v_aug_t = jnp.concatenate([jnp.transpose(f["vv"][h]), ones_rows], axis=0)
        ct = ct_ref[sq, h]
        s_raw = jnp.dot(k_bf, qt_bf, preferred_element_type=F32)
        inter = jnp.dot(ct.astype(BF16), qt_bf, preferred_element_type=F32)
        return k_bf, v_aug_t, ct, s_raw, inter

    def head(sq, h, f, scores):
        hs = slice(h * dh, (h + 1) * dh)
        k_bf, v_aug_t, ct, s_raw, inter = scores
        c_c = f["c_col"][:, h:h + 1]
        b_r = f["b_row"][H + h:H + h + 1, :]
        i_r = f["a_row"][h:h + 1, :]
        m_prev = m_ref[sq, h, 0:1, 0:1]

        b_last = b_r[:, L - 1:L]
        log_g = b_last - b_r + i_r
        m_new = jnp.maximum(b_last + m_prev, jnp.max(log_g, axis=1, keepdims=True))
        w_g = jnp.exp(log_g - m_new)
        decay = jnp.exp(b_last + m_prev - m_new)
        upd = jnp.dot((v_aug_t * w_g).astype(BF16), k_bf, preferred_element_type=F32)
        ct_ref[sq, h] = decay * ct + upd
        m_ref[sq, h] = jnp.broadcast_to(m_new, (SUBLANES, LANES))

        log_d = jnp.where(causal_t, c_c + b_r, -jnp.inf)
        log_inter = b_r + m_prev
        m_t = jnp.maximum(log_inter, jnp.max(log_d, axis=0, keepdims=True))
        w_intra = jnp.exp(log_d - m_t)
        w_inter = jnp.exp(log_inter - m_t)
        s_t = s_raw * w_intra
        comb = (jnp.dot(v_aug_t.astype(BF16), s_t.astype(BF16), preferred_element_type=F32)
                + w_inter * inter)
        hh_t = comb[:dh] / jnp.maximum(jnp.abs(comb[dh:dh + 1]), jnp.exp(-m_t))

        mu = jnp.mean(hh_t, axis=0, keepdims=True)
        dlt = hh_t - mu
        var = jnp.mean(dlt * dlt, axis=0, keepdims=True)
        hn = jnp.transpose(dlt * lax.rsqrt(var + EPS)) * mn_ref[:, hs]
        zh = z_ref[sq, :, hs]
        return ((hn + ms_ref[:, hs] * f["xc"][:, hs]) * (zh * _sigmoid(zh))).astype(BF16)

    fronts = [front_proj(sq) for sq in range(MLSTM_SEQS)]
    for f in fronts:
        front_gates(f)
    for f in fronts:
        front_cumsum(f)
    outs = [[] for _ in range(MLSTM_SEQS)]
    chains = [(sq, h) for sq in range(MLSTM_SEQS) for h in range(H)]
    pending = [head_scores(sq, h, fronts[sq]) for sq, h in chains[:MLSTM_LOOKAHEAD]]
    for idx, (sq, h) in enumerate(chains):
        scores = pending.pop(0)
        if idx + MLSTM_LOOKAHEAD < len(chains):
            nsq, nh = chains[idx + MLSTM_LOOKAHEAD]
            pending.append(head_scores(nsq, nh, fronts[nsq]))
        outs[sq].append(head(sq, h, fronts[sq], scores))
    for sq in range(MLSTM_SEQS):
        o_ref[sq] = jnp.concatenate(outs[sq], axis=1)


def _mlstm(xm, z, conv_w, conv_b, wqk, wv, wif_t, bif_r, mnorm, mskip):
    B, S, W = xm.shape
    L = MLSTM_CHUNK
    NS = MLSTM_SEQS
    assert B % NS == 0 and S % L == 0
    tok = pl.BlockSpec((NS, L, W), lambda b, j: (b, j, 0))
    return pl.pallas_call(
        _mlstm_kernel,
        out_shape=jax.ShapeDtypeStruct((B, S, W), BF16),
        grid=(B // NS, S // L),
        in_specs=[tok, tok,
                  _resident(conv_w.shape), _resident(conv_b.shape), _resident(wqk.shape), _resident(wv.shape),
                  _resident(wif_t.shape), _resident(bif_r.shape),
                  _resident(mnorm.shape), _resident(mskip.shape)],
        out_specs=tok,
        scratch_shapes=[pltpu.VMEM((NS, MLSTM_HEADS, MLSTM_STATE_ROWS, MLSTM_HEAD_DIM), F32),
                        pltpu.VMEM((NS, MLSTM_HEADS, SUBLANES, LANES), F32),
                        pltpu.VMEM((NS, SUBLANES, W), F32)],
        compiler_params=pltpu.CompilerParams(dimension_semantics=("arbitrary", "arbitrary"),
                                             vmem_limit_bytes=VMEM_LIMIT_BYTES),
        name="mlstm",
    )(xm, z, conv_w, conv_b, wqk, wv, wif_t, bif_r, mnorm, mskip)


def _moba_kernel(q_ref, k_ref, v_ref, km_ref, o_ref, lmin_ref, kaug_ref, vt_ref, qaug_ref, kn_ref, *, bounded):
    BS = MOBA_BLOCK
    NP = MOBA_PAIRS
    S = k_ref.shape[1]
    nb = km_ref.shape[1]
    hd = ATTN_HEAD_DIM
    i = pl.program_id(2)

    @pl.when(i == 0)
    def _():
        lmin_ref[...] = jnp.full(lmin_ref.shape, jnp.inf, F32)
        lane_s = lax.broadcasted_iota(jnp.int32, (S, LANES), 1)
        blk_s = lax.shift_right_logical(lax.broadcasted_iota(jnp.int32, (S, LANES), 0), BS.bit_length() - 1)
        k_tail =((lane_s == blk_s) | (lane_s == nb)).astype(BF16)
        v_tail = (lax.broadcasted_iota(jnp.int32, (VT_ROWS - LANES, S), 0) == 0).astype(BF16)
        for p in range(NP):
            ps = slice(p * LANES, (p + 1) * LANES)
            kaug_ref[p, :, 0:LANES] = k_ref[0, :, ps]
            kaug_ref[p, :, LANES:] = k_tail
            vt_ref[p, 0:LANES, :] = jnp.transpose(v_ref[0, :, ps].astype(F32)).astype(BF16)
            vt_ref[p, LANES:, :] = v_tail
            if bounded:
                kf = k_ref[0, :, ps].astype(F32)
                head_of_dim = lax.shift_right_logical(lax.broadcasted_iota(jnp.int32, (LANES, LANES), 0),
                                                      hd.bit_length() - 1)
                ind = (head_of_dim == lax.broadcasted_iota(jnp.int32, (LANES, LANES), 1)).astype(BF16)
                ksq = jnp.dot((kf * kf).astype(BF16), ind, preferred_element_type=F32)
                kn_ref[p] = jnp.broadcast_to(jnp.max(ksq, axis=0, keepdims=True), kn_ref.shape[1:])

            km = km_ref[0, :, ps]
            km_hi = km.astype(BF16)
            km_lo = (km - km_hi.astype(F32)).astype(BF16)
            dim_t = lax.broadcasted_iota(jnp.int32, (LANES, 2 * BS), 0)
            col_t = lax.broadcasted_iota(jnp.int32, (LANES, 2 * BS), 1)
            in_head = (lax.shift_right_logical(dim_t, hd.bit_length() - 1)
                       == lax.shift_right_logical(col_t, BS.bit_length() - 1))
            blk_t = lax.broadcasted_iota(jnp.int32, (nb, 2 * BS), 0)
            pad_rows = lax.broadcasted_iota(jnp.int32, (LANES - nb, 2 * BS), 0)
            if bounded:
                ksq_max = jnp.where(col_t[0:1] < BS, kn_ref[p, 0:1, 0:1], kn_ref[p, 0:1, 1:2])
            for j in range(nb):
                q_j = jnp.transpose(q_ref[0, j * BS:(j + 1) * BS, ps].astype(F32))
                qq = jnp.where(in_head, jnp.concatenate([q_j, q_j], axis=1), 0.0)
                qq_bf = qq.astype(BF16)
                gate = (jnp.dot(km_hi, qq_bf, preferred_element_type=F32)
                        + jnp.dot(km_lo, qq_bf, preferred_element_type=F32))
                cur = jnp.where(blk_t < j, gate, -jnp.inf)
                bias_t = jnp.full((nb, 2 * BS), MASK_VALUE, F32)
                for _ in range(MOBA_TOPK):
                    mx = jnp.max(cur, axis=0, keepdims=True)
                    first = jnp.min(jnp.where(cur == mx, blk_t, nb), axis=0, keepdims=True)
                    pick = (blk_t == first) & (mx > -jnp.inf)
                    bias_t = jnp.where(pick, 0.0, bias_t)
                    cur = jnp.where(pick, -jnp.inf, cur)
                bias_t = jnp.where(blk_t == j, 0.0, bias_t)
                if bounded:
                    bound = jnp.sqrt(jnp.sum(qq * qq, axis=0, keepdims=True) * ksq_max) * MOBA_BOUND_SLACK
                    tail = jnp.where(pad_rows == 0, -bound, 0.0)
                else:
                    tail = jnp.zeros((LANES - nb, 2 * BS), F32)
                qaug_ref[p, j] = jnp.concatenate(
                    [qq_bf, bias_t.astype(BF16), tail.astype(BF16)], axis=0)

    future_key = (lax.broadcasted_iota(jnp.int32, (BS, 2 * BS), 0)
                  > (lax.broadcasted_iota(jnp.int32, (BS, 2 * BS), 1) & (BS - 1)))

    def scores(j):
        return [jnp.dot(kaug_ref[p, j * BS:(j + 1) * BS, :], qaug_ref[p, i], preferred_element_type=F32)
                for p in range(NP)]

    CB = MOBA_CASE_BLOCKS
    for c in range(nb // CB):
        @pl.when(lax.shift_right_logical(i, CB.bit_length() - 1) == c)
        def _():
            m = [None] * NP
            pv = [None] * NP
            order = list(range((c + 1) * CB - 1, -1, -1))
            pending = [scores(j) for j in order[:MOBA_LOOKAHEAD]]
            for idx, j in enumerate(order):
                s_all = pending.pop(0)
                if idx + MOBA_LOOKAHEAD < len(order):
                    pending.append(scores(order[idx + MOBA_LOOKAHEAD]))
                for p in range(NP):
                    s = s_all[p]
                    if j >= c * CB:
                        s = jnp.where(future_key, jnp.where(i == j, MASK_VALUE, s), s)
                    alpha = None
                    if not bounded:
                        cm = jnp.max(s, axis=0, keepdims=True)
                        if m[p] is not None:
                            cm = jnp.maximum(m[p], cm)
                            alpha = jnp.exp(m[p] - cm)
                        m[p] = cm
                        s = s - cm
                    d = jnp.dot(vt_ref[p, :, j * BS:(j + 1) * BS], jnp.exp(s).astype(BF16),
                                preferred_element_type=F32)
                    if pv[p] is None:
                        pv[p] = d
                    elif alpha is None:
                        pv[p] = pv[p] + d
                    else:
                        pv[p] = alpha * pv[p] + d
            outs, lmin = [], None
            for p in range(NP):
                denom = pv[p][LANES:LANES + 1]
                outs += [pv[p][0:hd, 0:BS] / denom[:, 0:BS], pv[p][hd:LANES, BS:] / denom[:, BS:]]
                dmin = jnp.min(denom, axis=1, keepdims=True)
                lmin = dmin if lmin is None else jnp.minimum(lmin, dmin)
            o_ref[0] = jnp.transpose(jnp.concatenate(outs, axis=0))
            lmin_ref[0] = jnp.minimum(lmin_ref[0], jnp.broadcast_to(lmin, lmin_ref.shape[1:]))


def _moba_call(q, k, v, kmean, bounded):
    B, S, W = q.shape
    BS = MOBA_BLOCK
    NP = MOBA_PAIRS
    nb = S // BS
    ngroups = W // (NP * LANES)
    assert nb < LANES - 1 and W % (NP * LANES) == 0
    qspec = pl.BlockSpec((1, BS, NP * LANES), lambda b, g, i: (b, i, g))
    kvspec = pl.BlockSpec((1, S, NP * LANES), lambda b, g, i: (b, 0, g))
    return pl.pallas_call(
        functools.partial(_moba_kernel, bounded=bounded),
        out_shape=(jax.ShapeDtypeStruct((B, S, W), F32),
                   jax.ShapeDtypeStruct((B * ngroups, SUBLANES, LANES), F32)),
        grid=(B, ngroups, nb),
        in_specs=[kvspec, kvspec, kvspec, pl.BlockSpec((1, nb, NP * LANES), lambda b, g, i: (b, 0, g))],
        out_specs=(qspec, pl.BlockSpec((1, SUBLANES, LANES), lambda b, g, i: (b * ngroups + g, 0, 0))),
        scratch_shapes=[pltpu.VMEM((NP, S, 2 * LANES), BF16),
                        pltpu.VMEM((NP, VT_ROWS, S), BF16),
                        pltpu.VMEM((NP, nb, 2 * LANES, 2 * BS), BF16),
                        pltpu.VMEM((NP, SUBLANES, LANES), F32)],
        compiler_params=pltpu.CompilerParams(dimension_semantics=("arbitrary",) * 3,
                                             vmem_limit_bytes=VMEM_LIMIT_BYTES),
        name="moba_bounded" if bounded else "moba_exact",
    )(q, k, v, kmean)


def _moba(q, k, v, kmean):
    o_fast, lmin = _moba_call(q, k, v, kmean, True)
    ok = jnp.all(lmin[:, 0, 0] > MOBA_MIN_DENOM)
    return lax.cond(ok, lambda: o_fast, lambda: _moba_call(q, k, v, kmean, False)[0])


def _out_ffn2_kernel(x1_ref, oa_ref, om_ref, g2_ref, sh_ref, sc_ref, g3_ref, an_ref, wo_ref,
                     nw_ref, wg_ref, wu_ref, wd_ref, fn_ref, o_ref, a_ref):
    oa = _rmsnorm(oa_ref[...], an_ref[...]).astype(BF16)
    mix = (jnp.dot(oa, wo_ref[0:ATTN_WIDTH, :], preferred_element_type=F32)
           + jnp.dot(om_ref[...], wo_ref[ATTN_WIDTH:, :], preferred_element_type=F32))
    x2 = x1_ref[...] + (1.0 + g2_ref[0, 0]) * mix
    h = _rms_mod(x2, nw_ref[...], sh_ref[0, 0], sc_ref[0, 0])
    y = _swiglu(h.astype(BF16), wg_ref, wu_ref, wd_ref, a_ref)
    x3 = x2 + (0.5 * (1.0 + g3_ref[0, 0])) * y
    o_ref[...] = _rmsnorm(x3, fn_ref[...])


def _out_ffn2(x1, oa, om, mod4, attn_norm, w_out, norm_w, wg, wu, wd, final_norm, seq):
    T = x1.shape[0]
    tm = FFN_TOKENS
    tps = seq // tm
    tok = lambda w: pl.BlockSpec((tm, w), lambda i: (i, 0))
    return pl.pallas_call(
        _out_ffn2_kernel,
        out_shape=jax.ShapeDtypeStruct((T, D_MODEL), F32),
        grid=(T // tm,),
        in_specs=[tok(D_MODEL), tok(ATTN_WIDTH), tok(MLSTM_WIDTH),
                  _mod_spec(5, tps), _mod_spec(6, tps), _mod_spec(7, tps), _mod_spec(8, tps),
                  _resident((1, ATTN_WIDTH)), _resident((D_MODEL, D_MODEL)), _resident((1, D_MODEL)),
                  _resident((D_MODEL, D_FF)), _resident((D_MODEL, D_FF)), _resident((D_FF, D_MODEL)),
                  _resident((1, D_MODEL))],
        out_specs=tok(D_MODEL),
        scratch_shapes=[pltpu.VMEM((tm, D_FF), BF16)],
        compiler_params=pltpu.CompilerParams(dimension_semantics=("arbitrary",),
                                             vmem_limit_bytes=VMEM_LIMIT_BYTES),
        name="out_ffn2",
    )(x1, oa, om, mod4, mod4, mod4, mod4, attn_norm, w_out, norm_w, wg, wu, wd, final_norm)


def kernel(x, c, w_ada, b_ada, ffn1_norm, ffn1_w_gate, ffn1_w_up, ffn1_w_down, mix_norm, w_in, conv_w, conv_b,
           w_q_m, w_k_m, w_v_m, w_if, b_if, mlstm_norm, mlstm_skip, attn_norm, w_out,
           ffn2_norm, ffn2_w_gate, ffn2_w_up, ffn2_w_down, final_norm):
    B, S, D = x.shape
    T = B * S
    assert w_ada.shape[0] == 1, "only depth 1 is supported"
    bf = lambda a: a.astype(BF16)
    l = 0
    mod4 = _adaln(c, w_ada[l], b_ada[l][None, :]).reshape(N_MOD, B, 1, D)
    x1, q, k, v, xm, z, kmean = _ffn1_proj(
        x.reshape(T, D), mod4, ffn1_norm[l][None, :], bf(ffn1_w_gate[l]), bf(ffn1_w_up[l]), bf(ffn1_w_down[l]),
        mix_norm[l][None, :], bf(w_in[l]), S)
    o_m = _mlstm(xm.reshape(B, S, MLSTM_WIDTH), z.reshape(B, S, MLSTM_WIDTH),
                 conv_w[l], conv_b[l][None, :],
                 bf(jnp.concatenate([w_q_m[l], w_k_m[l]], axis=-1)), bf(w_v_m[l]),
                 bf(w_if[l].T), b_if[l][:, None],
                 mlstm_norm[l][None, :], mlstm_skip[l][None, :])
    o_a = _moba(q.reshape(B, S, ATTN_WIDTH), k.reshape(B, S, ATTN_WIDTH), v.reshape(B, S, ATTN_WIDTH),
                kmean.reshape(B, S // MOBA_BLOCK, ATTN_WIDTH))
    out = _out_ffn2(x1, o_a.reshape(T, ATTN_WIDTH), o_m.reshape(T, MLSTM_WIDTH), mod4,
                    attn_norm[l][None, :], bf(w_out[l]), ffn2_norm[l][None, :],
                    bf(ffn2_w_gate[l]), bf(ffn2_w_up[l]), bf(ffn2_w_down[l]), final_norm[None, :], S)
    return out.reshape(B, S, D)
```
